```python
import functools
import jax, jax.numpy as jnp
from jax import lax
import numpy as np

D_MODEL = 1024
BATCH = 32
SEQ = 256
DEPTH = 2
DEC_BATCH = 8
DEC_SEQ = 4096
PAST_LEN = 256

GRID_W = 64
D_MIX = D_MODEL
POOL_WIDTH = D_MIX // 2
POOL_GROUPS = 4
POOL_GROUP_W = POOL_WIDTH // POOL_GROUPS
POOL_WINDOWS = (2, 4, 8, 16)
NA_WIDTH = D_MIX - POOL_WIDTH
HEAD_DIM = 64
NA_HEADS = NA_WIDTH // HEAD_DIM
NA_WIN_R = 8
NA_WIN_C = 16
KEY_COLS = 2 * NA_WIN_C
N_COL_BLOCKS = GRID_W // NA_WIN_C
IN_WIDTH = POOL_WIDTH + 3 * NA_WIDTH
D_FF = 2816
N_EXPERTS = 8
TOP_K = 2
D_EXPERT = 3584
N_DENSE_LAYERS = (DEPTH + 1) // 2
N_MOE_LAYERS = DEPTH // 2
CTX_Q_BLOCK = 128
N_MOD = 6
NORM_EPS = 1e-6

kernel_name = "hybrid_pool_natten_dit_step"


def _rmsnorm(x, g):
    x32 = x.astype(jnp.float32)
    y = x32 * lax.rsqrt(jnp.mean(x32 * x32, axis=-1, keepdims=True) + NORM_EPS)
    return (y * g.astype(jnp.float32)).astype(x.dtype)


def _modulation(cond, w_ada, b_ada):
    m = jax.nn.silu(cond) @ w_ada + b_ada
    return [t[:, None, :] for t in jnp.split(m, N_MOD, axis=-1)]


def _mix_projections(h, w_in, q_g, k_g):
    B, L, _ = h.shape
    u = h @ w_in
    pool_u = u[..., :POOL_WIDTH]
    q, k, v = jnp.split(u[..., POOL_WIDTH:], 3, axis=-1)
    shp = (B, L, NA_HEADS, HEAD_DIM)
    return pool_u, _rmsnorm(q.reshape(shp), q_g), _rmsnorm(k.reshape(shp), k_g), v.reshape(shp)


def _multiscale_pool(u, w_pool, pool_scale):
    B, L, _ = u.shape
    ug = u.reshape(B, L, POOL_GROUPS, POOL_GROUP_W).astype(jnp.float32)
    cs = jnp.concatenate([jnp.zeros((B, 1, POOL_GROUPS, POOL_GROUP_W), jnp.float32),
                          jnp.cumsum(ug, axis=1)], axis=1)
    half = np.array(POOL_WINDOWS) // 2
    pos = np.arange(L)[:, None]
    lo = np.clip(pos - half, 0, L)
    hi = np.clip(pos + half, 0, L)
    grp = np.arange(POOL_GROUPS)[None, :]
    count = (hi - lo).astype(np.float32)[None, :, :, None]
    mean = (cs[:, hi, grp] - cs[:, lo, grp]) / count
    pooled = (mean - ug).astype(u.dtype)
    out = jnp.einsum('blgc,gcd->blgd', pooled, w_pool)
    return out.reshape(B, L, POOL_WIDTH) * pool_scale


def _context_attention(q, k, v):
    B, L, H, dh = q.shape
    scale = dh ** -0.5
    qb = q.reshape(B, L // CTX_Q_BLOCK, CTX_Q_BLOCK, H, dh).swapaxes(0, 1)

    def block(qi):
        s = jnp.einsum('bqhd,bkhd->bhqk', qi, k).astype(jnp.float32) * scale
        p = jax.nn.softmax(s, axis=-1).astype(v.dtype)
        return jnp.einsum('bhqk,bkhd->bqhd', p, v)

    o = lax.map(block, qb)
    return o.swapaxes(0, 1).reshape(B, L, H * dh)


def _neighbourhood_attention(q, k, v, ctx_k, ctx_v, rpb):
    B, L, H, dh = q.shape
    rows = L // GRID_W
    win_r = min(NA_WIN_R, rows)
    scale = dh ** -0.5
    qg = q.reshape(B, rows, N_COL_BLOCKS, NA_WIN_C, H, dh)
    kg = k.reshape(B, rows, GRID_W, H, dh)
    vg = v.reshape(B, rows, GRID_W, H, dh)
    col_base = np.clip(np.arange(N_COL_BLOCKS) * NA_WIN_C - NA_WIN_C // 2, 0, GRID_W - KEY_COLS)
    col_idx = col_base[:, None] + np.arange(KEY_COLS)
    q_col = np.arange(GRID_W).reshape(N_COL_BLOCKS, NA_WIN_C)
    q_start = np.clip(q_col - NA_WIN_C // 2, 0, GRID_W - NA_WIN_C)
    kc = col_idx[:, None, :]
    valid = (kc >= q_start[:, :, None]) & (kc < q_start[:, :, None] + NA_WIN_C)
    dc_idx = np.clip(kc - q_col[:, :, None], -(NA_WIN_C - 1), NA_WIN_C - 1) + NA_WIN_C - 1
    rpb_c = rpb[:, :, dc_idx]
    n_loc = win_r * KEY_COLS

    def row_step(r):
        r0 = jnp.clip(r - win_r // 2, 0, rows - win_r)
        k_rows = lax.dynamic_slice_in_dim(kg, r0, win_r, axis=1)[:, :, col_idx]
        v_rows = lax.dynamic_slice_in_dim(vg, r0, win_r, axis=1)[:, :, col_idx]
        q_r = lax.dynamic_index_in_dim(qg, r, axis=1, keepdims=False)
        dr_idx = r0 + jnp.arange(win_r) - r + NA_WIN_R - 1
        bias = jnp.take(rpb_c, dr_idx, axis=1).transpose(0, 2, 3, 1, 4)
        s_loc = jnp.einsum('bcqhd,bickhd->bhcqik', q_r, k_rows).astype(jnp.float32) * scale
        s_loc = jnp.where(valid[:, :, None, :], s_loc + bias[None].astype(jnp.float32), -jnp.inf)
        s_ctx = jnp.einsum('bcqhd,bmhd->bhcqm', q_r, ctx_k).astype(jnp.float32) * scale
        s = jnp.concatenate([s_loc.reshape(B, H, N_COL_BLOCKS, NA_WIN_C, n_loc), s_ctx], axis=-1)
        p = jax.nn.softmax(s, axis=-1).astype(v.dtype)
        p_loc = p[..., :n_loc].reshape(B, H, N_COL_BLOCKS, NA_WIN_C, win_r, KEY_COLS)
        p_ctx = p[..., n_loc:]
        return (jnp.einsum('bhcqik,bickhd->bcqhd', p_loc, v_rows)
                + jnp.einsum('bhcqm,bmhd->bcqhd', p_ctx, ctx_v))

    o = lax.map(row_step, jnp.arange(rows))
    return o.transpose(1, 0, 2, 3, 4, 5).reshape(B, L, H * dh)


def _swiglu(h, wg, wu, wd):
    return (jax.nn.silu(h @ wg) * (h @ wu)) @ wd


def _moe(h, router, wg, wu, wd):
    logits = (h @ router).astype(jnp.float32)
    top_v, top_i = lax.top_k(logits, TOP_K)
    gates = jax.nn.softmax(top_v, axis=-1)
    combine = jnp.sum(jax.nn.one_hot(top_i, N_EXPERTS, dtype=jnp.float32) * gates[..., None],
                      axis=-2).astype(h.dtype)
    out = jnp.zeros_like(h)
    for e in range(N_EXPERTS):
        out = out + combine[..., e:e + 1] * _swiglu(h, wg[e], wu[e], wd[e])
    return out


def setup_inputs(seed: int = 0) -> dict:
    key = jax.random.key(seed)
    ks = jax.random.split(key, 26)
    f32 = jnp.float32

    def nrm(k, shape, std):
        return jax.random.normal(k, shape, f32) * std

    return {
        'x_prompt': nrm(ks[0], (BATCH, SEQ, D_MODEL), 1.0),
        'x_sample': nrm(ks[1], (DEC_BATCH, DEC_SEQ, D_MODEL), 1.0),
        'cache_k': nrm(ks[2], (DEC_BATCH, DEPTH, PAST_LEN, NA_HEADS, HEAD_DIM), 1.0),
        'cache_v': nrm(ks[3], (DEC_BATCH, DEPTH, PAST_LEN, NA_HEADS, HEAD_DIM), 1.0),
        'c': nrm(ks[4], (DEC_BATCH, D_MODEL), 1.0),
        'c_ctx': nrm(ks[5], (D_MODEL,), 1.0),
        'norm1_g': 1.0 + nrm(ks[6], (DEPTH, D_MODEL), 0.05),
        'norm2_g': 1.0 + nrm(ks[7], (DEPTH, D_MODEL), 0.05),
        'w_ada': nrm(ks[8], (DEPTH, D_MODEL, N_MOD * D_MODEL), 0.5 * D_MODEL ** -0.5),
        'b_ada': nrm(ks[9], (DEPTH, N_MOD * D_MODEL), 0.02),
        'w_in': nrm(ks[10], (DEPTH, D_MODEL, IN_WIDTH), D_MODEL ** -0.5),
        'q_norm_g': 1.0 + nrm(ks[11], (DEPTH, HEAD_DIM), 0.05),
        'k_norm_g': 1.0 + nrm(ks[12], (DEPTH, HEAD_DIM), 0.05),
        'w_pool': nrm(ks[13], (DEPTH, POOL_GROUPS, POOL_GROUP_W, POOL_GROUP_W), POOL_GROUP_W ** -0.5),
        'pool_scale': 1.0 + nrm(ks[14], (DEPTH, POOL_WIDTH), 0.05),
        'rpb': nrm(ks[15], (DEPTH, NA_HEADS, 2 * NA_WIN_R - 1, 2 * NA_WIN_C - 1), 0.1),
        'w_out': nrm(ks[16], (DEPTH, D_MIX, D_MODEL), D_MIX ** -0.5),
        'ffn_w_gate': nrm(ks[17], (N_DENSE_LAYERS, D_MODEL, D_FF), D_MODEL ** -0.5),
        'ffn_w_up': nrm(ks[18], (N_DENSE_LAYERS, D_MODEL, D_FF), D_MODEL ** -0.5),
        'ffn_w_down': nrm(ks[19], (N_DENSE_LAYERS, D_FF, D_MODEL), D_FF ** -0.5),
        'moe_router': nrm(ks[20], (N_MOE_LAYERS, D_MODEL, N_EXPERTS), D_MODEL ** -0.5),
        'moe_w_gate': nrm(ks[21], (N_MOE_LAYERS, N_EXPERTS, D_MODEL, D_EXPERT), D_MODEL ** -0.5),
        'moe_w_up': nrm(ks[22], (N_MOE_LAYERS, N_EXPERTS, D_MODEL, D_EXPERT), D_MODEL ** -0.5),
        'moe_w_down': nrm(ks[23], (N_MOE_LAYERS, N_EXPERTS, D_EXPERT, D_MODEL), D_EXPERT ** -0.5),
    }


def reference(x_prompt, x_sample, cache_k, cache_v, c, c_ctx, norm1_g, norm2_g, w_ada, b_ada,
              w_in, q_norm_g, k_norm_g, w_pool, pool_scale, rpb, w_out,
              ffn_w_gate, ffn_w_up, ffn_w_down, moe_router, moe_w_gate, moe_w_up, moe_w_down):

    def layer(x, cond, l, attn_fn):
        sh1, sc1, g1, sh2, sc2, g2 = _modulation(cond, w_ada[l], b_ada[l])
        h = _rmsnorm(x, norm1_g[l]) * (1.0 + sc1) + sh1
        pool_u, q, k, v = _mix_projections(h, w_in[l], q_norm_g[l], k_norm_g[l])
        mix = jnp.concatenate([_multiscale_pool(pool_u, w_pool[l], pool_scale[l]),
                               attn_fn(q, k, v)], axis=-1) @ w_out[l]
        x = x + g1 * mix
        h = _rmsnorm(x, norm2_g[l]) * (1.0 + sc2) + sh2
        if l % 2 == 0:
            i = l // 2
            f = _swiglu(h, ffn_w_gate[i], ffn_w_up[i], ffn_w_down[i])
        else:
            i = l // 2
            f = _moe(h, moe_router[i], moe_w_gate[i], moe_w_up[i], moe_w_down[i])
        return x + g2 * f, k, v

    xp = x_prompt
    ks, vs = [], []
    for l in range(DEPTH):
        xp, k_l, v_l = layer(xp, c_ctx[None, :], l, _context_attention)
        ks.append(k_l)
        vs.append(v_l)
    y_prompt = xp
    new_cache_k = jnp.stack(ks, axis=1)
    new_cache_v = jnp.stack(vs, axis=1)

    xs = x_sample
    for l in range(DEPTH):
        attn_fn = functools.partial(_neighbourhood_attention, ctx_k=cache_k[:, l],
                                    ctx_v=cache_v[:, l], rpb=rpb[l])
        xs, _, _ = layer(xs, c, l, attn_fn)
    y_sample = xs

    return (y_prompt, y_sample, new_cache_k, new_cache_v)
```

```python
import functools

import numpy as np
import jax
import jax.numpy as jnp
from jax import lax
from jax.experimental import pallas as pl
from jax.experimental.pallas import tpu as pltpu

F32 = jnp.float32
BF16 = jnp.bfloat16

GRID_W = 64
POOL_WINDOWS = (2, 4, 8, 16)
HEAD_DIM = 64
NA_WIN_R = 8
NA_WIN_C = 16
N_MOD = 6
TOP_K = 2
NORM_EPS = 1e-6

LANES = 128
SUBLANES = 8
VMEM_LIMIT_BYTES = 56 * 1024 * 1024

TOKEN_TILE = 512
POOL_HALO = 8
FF_CHUNK = 256
MOE_ROW_TILE = 1024
MOE_F_CHUNK = 512
COMBINE_TILE = 256
ADA_COL_TILE = 1024
COND_ROWS = 16
ROUTER_LANES = 128


def _params(n_axes=1):
    return pltpu.CompilerParams(
        dimension_semantics=("arbitrary",) * n_axes,
        vmem_limit_bytes=VMEM_LIMIT_BYTES,
    )


def _resident(shape, index_map):
    return pl.BlockSpec(shape, index_map, pipeline_mode=pl.Buffered(1))


def _ada_kernel(c_ref, w_ref, b_ref, o_ref):
    c = c_ref[...]
    s = c * jax.nn.sigmoid(c)
    o_ref[...] = jnp.dot(s.astype(BF16), w_ref[...].astype(BF16),
                         preferred_element_type=F32) + b_ref[...]


def _ada(cond, w_ada, b_ada):
    depth, d, width = w_ada.shape
    return pl.pallas_call(
        _ada_kernel,
        grid=(depth, width // ADA_COL_TILE),
        in_specs=[
            pl.BlockSpec((COND_ROWS, d), lambda l, j: (0, 0)),
            pl.BlockSpec((None, d, ADA_COL_TILE), lambda l, j: (l, 0, j)),
            pl.BlockSpec((None, 1, ADA_COL_TILE), lambda l, j: (l, 0, j)),
        ],
        out_specs=pl.BlockSpec((None, COND_ROWS, ADA_COL_TILE), lambda l, j: (l, 0, j)),
        out_shape=jax.ShapeDtypeStruct((depth, COND_ROWS, width), F32),
        compiler_params=_params(2),
        name="ada",
    )(cond, w_ada, b_ada.reshape(depth, 1, width))


def _premix_kernel(x_ref, mod_ref, g_ref, w_ref, qg_ref, kg_ref, hsum_ref,
                   pu_ref, q_ref, k_ref, v_ref, kf_ref, vf_ref, *,
                   n_prompt_tiles, pool_w, na_w):
    i = pl.program_id(0)
    x = x_ref[...]
    inv = lax.rsqrt(jnp.mean(x * x, axis=-1, keepdims=True) + NORM_EPS)
    mod = mod_ref[...]
    h = (x * inv * g_ref[...]) * (1.0 + mod[1:2]) + mod[0:1]
    u = jnp.dot(h.astype(BF16), w_ref[...], preferred_element_type=F32)
    pu_ref[...] = u[:, :pool_w]
    q = u[:, pool_w:pool_w + na_w]
    k = u[:, pool_w + na_w:pool_w + 2 * na_w]
    v = u[:, pool_w + 2 * na_w:]

    def head_norm(t, g):
        ms = jnp.dot((t * t).astype(BF16), hsum_ref[...],
                     preferred_element_type=F32) * (1.0 / HEAD_DIM)
        return t * lax.rsqrt(ms + NORM_EPS) * g

    qn = head_norm(q, qg_ref[...])
    kn = head_norm(k, kg_ref[...])
    q_ref[...] = (qn * (HEAD_DIM ** -0.5)).astype(BF16)
    k_ref[...] = kn.astype(BF16)
    v_ref[...] = v.astype(BF16)

    @pl.when(i < n_prompt_tiles)
    def _():
        kf_ref[...] = kn
        vf_ref[...] = v


def _premix(x, mod, l, norm_g, w_in, q_g, k_g, hsum, n_prompt, cond_row):
    n, d = x.shape
    pool_w = d // 2
    na_w = d - pool_w
    n_tiles = n // TOKEN_TILE
    n_prompt_tiles = n_prompt // TOKEN_TILE
    last_p = n_prompt_tiles - 1
    tok = lambda w: pl.BlockSpec((TOKEN_TILE, w), lambda i: (i, 0))
    ptok = pl.BlockSpec((TOKEN_TILE, na_w), lambda i: (jnp.minimum(i, last_p), 0))
    kernel = functools.partial(_premix_kernel, n_prompt_tiles=n_prompt_tiles,
                               pool_w=pool_w, na_w=na_w)
    return pl.pallas_call(
        kernel,
        grid=(n_tiles,),
        in_specs=[
            tok(d),
            pl.BlockSpec((None, None, N_MOD, d), lambda i: (l, cond_row(i), 0, 0)),
            pl.BlockSpec((1, d), lambda i: (0, 0)),
            _resident((d, w_in.shape[1]), lambda i: (0, 0)),
            pl.BlockSpec((1, na_w), lambda i: (0, 0)),
            pl.BlockSpec((1, na_w), lambda i: (0, 0)),
            _resident((na_w, na_w), lambda i: (0, 0)),
        ],
        out_specs=[tok(pool_w), tok(na_w), tok(na_w), tok(na_w), ptok, ptok],
        out_shape=[
            jax.ShapeDtypeStruct((n, pool_w), F32),
            jax.ShapeDtypeStruct((n, na_w), BF16),
            jax.ShapeDtypeStruct((n, na_w), BF16),
            jax.ShapeDtypeStruct((n, na_w), BF16),
            jax.ShapeDtypeStruct((n_prompt, na_w), F32),
            jax.ShapeDtypeStruct((n_prompt, na_w), F32),
        ],
        compiler_params=_params(),
        name=f"premix{l}",
    )(x, mod, norm_g, w_in, q_g, k_g, hsum)


def _pair_attention(q2, parts):
    nq = q2.shape[0]
    lo = lax.broadcasted_iota(jnp.int32, q2.shape, 1) < HEAD_DIM
    zero = jnp.zeros_like(q2)
    qbd = jnp.concatenate([jnp.where(lo, q2, zero), jnp.where(lo, zero, q2)], axis=0)
    nt = (((1,), (1,)), ((), ()))
    scores = []
    for k, _, bias in parts:
        s = lax.dot_general(k, qbd, nt, preferred_element_type=F32)
        scores.append(s if bias is None else s + bias)
    m = functools.reduce(jnp.maximum, [jnp.max(s, axis=0, keepdims=True) for s in scores])
    probs = [jnp.exp(s - m) for s in scores]
    denom = functools.reduce(jnp.add, [jnp.sum(p, axis=0, keepdims=True) for p in probs])
    r = 1.0 / denom
    tn = (((0,), (0,)), ((), ()))
    o = functools.reduce(jnp.add, [
        lax.dot_general((p * r).astype(BF16), v, tn, preferred_element_type=F32)
        for p, (_, v, _) in zip(probs, parts)])
    lo_out = lax.broadcasted_iota(jnp.int32, (nq, LANES), 1) < HEAD_DIM
    return jnp.where(lo_out, o[:nq], o[nq:])


def _attn_kernel(q_ref, k_ref, v_ref, ck_ref, cv_ref, bias_ref, o_ref, *,
                 n_prompt_tiles, seq, dec_seq, n_pairs):
    i = pl.program_id(0)
    tiles_per_seq = dec_seq // TOKEN_TILE
    rows_per_tile = TOKEN_TILE // GRID_W
    rows = dec_seq // GRID_W
    win_keys = NA_WIN_R * GRID_W
    block_off = (i * TOKEN_TILE) % dec_seq

    @pl.when(i < n_prompt_tiles)
    def _():
        for s in range(TOKEN_TILE // seq):
            start = pl.multiple_of(block_off + s * seq, seq)
            for hp in range(n_pairs):
                lanes = slice(hp * LANES, (hp + 1) * LANES)
                q2 = q_ref[s * seq:(s + 1) * seq, lanes]
                k2 = k_ref[pl.ds(start, seq), lanes]
                v2 = v_ref[pl.ds(start, seq), lanes]
                out = _pair_attention(q2, [(k2, v2, None)])
                o_ref[s * seq:(s + 1) * seq, lanes] = out.astype(o_ref.dtype)

    @pl.when(i >= n_prompt_tiles)
    def _():
        row0 = ((i - n_prompt_tiles) % tiles_per_seq) * rows_per_tile

        def row_body(rl, carry):
            r = row0 + rl
            r0 = jnp.clip(r - NA_WIN_R // 2, 0, rows - NA_WIN_R)
            d0 = r0 - r + NA_WIN_R - 1
            kstart = pl.multiple_of(r0 * GRID_W, GRID_W)
            qstart = pl.multiple_of(rl * GRID_W, GRID_W)
            for hp in range(n_pairs):
                lanes = slice(hp * LANES, (hp + 1) * LANES)
                q2 = q_ref[pl.ds(qstart, GRID_W), lanes]
                kl = k_ref[pl.ds(kstart, win_keys), lanes]
                vl = v_ref[pl.ds(kstart, win_keys), lanes]
                bias = bias_ref[hp, pl.ds(d0, NA_WIN_R)].reshape(win_keys, LANES)
                out = _pair_attention(q2, [(kl, vl, bias),
                                           (ck_ref[:, lanes], cv_ref[:, lanes], None)])
                o_ref[pl.ds(qstart, GRID_W), lanes] = out.astype(o_ref.dtype)
            return carry

        lax.fori_loop(0, rows_per_tile, row_body, 0)


def _attention(q, k, v, ctx_k, ctx_v, bias, l, n_prompt, seq, dec_seq):
    n, na_w = q.shape
    n_tiles = n // TOKEN_TILE
    n_prompt_tiles = n_prompt // TOKEN_TILE
    tiles_per_seq = dec_seq // TOKEN_TILE
    past = ctx_k.shape[2]
    n_pairs = na_w // LANES
    kv_spec = pl.BlockSpec((dec_seq, na_w), lambda i: (i * TOKEN_TILE // dec_seq, 0))
    ctx_spec = pl.BlockSpec(
        (None, None, past, na_w),
        lambda i: (jnp.maximum(i - n_prompt_tiles, 0) // tiles_per_seq, l, 0, 0))
    kernel = functools.partial(_attn_kernel, n_prompt_tiles=n_prompt_tiles, seq=seq,
                               dec_seq=dec_seq, n_pairs=n_pairs)
    return pl.pallas_call(
        kernel,
        grid=(n_tiles,),
        in_specs=[
            pl.BlockSpec((TOKEN_TILE, na_w), lambda i: (i, 0)),
            kv_spec, kv_spec, ctx_spec, ctx_spec,
            _resident(bias.shape, lambda i: (0, 0, 0, 0)),
        ],
        out_specs=pl.BlockSpec((TOKEN_TILE, na_w), lambda i: (i, 0)),
        out_shape=jax.ShapeDtypeStruct((n, na_w), BF16),
        compiler_params=_params(),
        name=f"attn{l}",
    )(q, k, v, ctx_k, ctx_v, bias)


def _relative_bias_table(rpb_l):
    heads = rpb_l.shape[0]
    kc = np.arange(GRID_W)[:, None]
    qc = np.arange(GRID_W)[None, :]
    q_start = np.clip(qc - NA_WIN_C // 2, 0, GRID_W - NA_WIN_C)
    valid = (kc >= q_start) & (kc < q_start + NA_WIN_C)
    dc_idx = np.clip(kc - qc, -(NA_WIN_C - 1), NA_WIN_C - 1) + NA_WIN_C - 1
    t = rpb_l[:, :, dc_idx]
    t = jnp.where(valid[None, None], t.astype(F32), -jnp.inf)
    t = t.reshape(heads // 2, 2, 2 * NA_WIN_R - 1, GRID_W, GRID_W)
    return t.transpose(0, 2, 3, 1, 4).reshape(heads // 2, 2 * NA_WIN_R - 1, GRID_W, 2 * GRID_W)


def _mod_static(t, m):
    return t & (m - 1) if m & (m - 1) == 0 else lax.rem(t, m)


def _pool_mix(i, pu_ref, pp_ref, pn_ref, ext_ref, wp_ref, ps_ref, *, n_prompt_tiles, seq, dec_seq):
    tm = TOKEN_TILE
    ext_ref[0:POOL_HALO] = pp_ref[...]
    ext_ref[POOL_HALO:POOL_HALO + tm] = pu_ref[...]
    ext_ref[POOL_HALO + tm:] = pn_ref[...]
    is_prompt = i < n_prompt_tiles
    seq_len = jnp.where(is_prompt, seq, dec_seq)
    t = i * tm + lax.broadcasted_iota(jnp.int32, (tm, LANES), 0)
    pos = jnp.where(is_prompt, _mod_static(t, seq), _mod_static(t, dec_seq))
    outs = []
    for g, window in enumerate(POOL_WINDOWS):
        half = window // 2
        lanes = slice(g * LANES, (g + 1) * LANES)
        total = jnp.zeros((tm, LANES), F32)
        for j in range(-half, half):
            term = ext_ref[POOL_HALO + j:POOL_HALO + j + tm, lanes]
            ok = (pos + j >= 0) if j < 0 else (pos + j < seq_len)
            total = total + jnp.where(ok, term, 0.0)
        count = (jnp.minimum(pos + half, seq_len) - jnp.maximum(pos - half, 0)).astype(F32)
        pooled = total / count - pu_ref[:, lanes]
        outs.append(jnp.dot(pooled.astype(BF16), wp_ref[g], preferred_element_type=F32))
    return jnp.concatenate(outs, axis=-1) * ps_ref[...]


def _mix_and_norm(i, x_ref, pu_ref, pp_ref, pn_ref, at_ref, mod_ref, wp_ref, ps_ref, wo_ref,
                  g2_ref, ext_ref, **seq_info):
    pool_out = _pool_mix(i, pu_ref, pp_ref, pn_ref, ext_ref, wp_ref, ps_ref, **seq_info)
    mixed = jnp.concatenate([pool_out.astype(BF16), at_ref[...]], axis=-1)
    mix = jnp.dot(mixed, wo_ref[...], preferred_element_type=F32)
    mod = mod_ref[...]
    x2 = x_ref[...] + mod[2:3] * mix
    inv = lax.rsqrt(jnp.mean(x2 * x2, axis=-1, keepdims=True) + NORM_EPS)
    h2 = (x2 * inv * g2_ref[...]) * (1.0 + mod[4:5]) + mod[3:4]
    return x2, h2, mod[5:6]


def _postmix_ffn_kernel(x_ref, pu_ref, pp_ref, pn_ref, at_ref, mod_ref, wp_ref, ps_ref, wo_ref,
                        g2_ref, wg_ref, wu_ref, wd_ref, y_ref, ext_ref, h_ref, acc_ref,
                        **seq_info):
    i = pl.program_id(0)
    x2, h2, gate2 = _mix_and_norm(i, x_ref, pu_ref, pp_ref, pn_ref, at_ref, mod_ref, wp_ref,
                                  ps_ref, wo_ref, g2_ref, ext_ref, **seq_info)
    h_ref[...] = h2.astype(BF16)
    acc_ref[...] = jnp.zeros_like(acc_ref)

    def chunk(c, carry):
        h = h_ref[...]
        a = jnp.dot(h, wg_ref[c], preferred_element_type=F32)
        b = jnp.dot(h, wu_ref[c], preferred_element_type=F32)
        act = (a * jax.nn.sigmoid(a) * b).astype(BF16)
        acc_ref[...] += jnp.dot(act, wd_ref[c], preferred_element_type=F32)
        return carry

    lax.fori_loop(0, wg_ref.shape[0], chunk, 0)
    y_ref[...] = x2 + gate2 * acc_ref[...]


def _postmix_router_kernel(x_ref, pu_ref, pp_ref, pn_ref, at_ref, mod_ref, wp_ref, ps_ref, wo_ref,
                           g2_ref, rh_ref, rl_ref, x2_ref, hr_ref, rout_ref, g0_ref, g1_ref,
                           ext_ref, *, n_experts, **seq_info):
    i = pl.program_id(0)
    x2, h2, _ = _mix_and_norm(i, x_ref, pu_ref, pp_ref, pn_ref, at_ref, mod_ref, wp_ref,
                              ps_ref, wo_ref, g2_ref, ext_ref, **seq_info)
    x2_ref[...] = x2
    tm = TOKEN_TILE
    for s in range(h2.shape[1] // LANES):
        hr_ref[pl.ds(s, tm, stride=SUBLANES), :] = h2[:, s * LANES:(s + 1) * LANES]
    hi = h2.astype(BF16)
    lo = (h2 - hi.astype(F32)).astype(BF16)
    logits = (jnp.dot(hi, rh_ref[...], preferred_element_type=F32)
              + jnp.dot(lo, rh_ref[...], preferred_element_type=F32)
              + jnp.dot(hi, rl_ref[...], preferred_element_type=F32))
    lane = lax.broadcasted_iota(jnp.int32, logits.shape, 1)
    logits = jnp.where(lane < n_experts, logits, -jnp.inf)
    m1 = jnp.max(logits, axis=-1, keepdims=True)
    i1 = jnp.min(jnp.where(logits == m1, lane, ROUTER_LANES), axis=-1, keepdims=True)
    rest = jnp.where(lane == i1, -jnp.inf, logits)
    m2 = jnp.max(rest, axis=-1, keepdims=True)
    i2 = jnp.min(jnp.where(rest == m2, lane, ROUTER_LANES), axis=-1, keepdims=True)
    e = jnp.exp(m2 - m1)
    gate_a = 1.0 / (1.0 + e)
    gate_b = e / (1.0 + e)
    rout_ref[...] = jnp.where(lane == 0, i1.astype(F32), jnp.where(lane == 1, i2.astype(F32), 0.0))
    g0_ref[...] = jnp.broadcast_to(gate_a, g0_ref.shape)
    g1_ref[...] = jnp.broadcast_to(gate_b, g1_ref.shape)


def _postmix_specs(n, d, pool_w, na_w, l, cond_row):
    n_halo_blocks = n // POOL_HALO
    per_tile = TOKEN_TILE // POOL_HALO
    tok = lambda w: pl.BlockSpec((TOKEN_TILE, w), lambda i: (i, 0))
    return [
        tok(d),
        tok(pool_w),
        pl.BlockSpec((POOL_HALO, pool_w), lambda i: (jnp.maximum(i * per_tile - 1, 0), 0)),
        pl.BlockSpec((POOL_HALO, pool_w),
                     lambda i: (jnp.minimum((i + 1) * per_tile, n_halo_blocks - 1), 0)),
        tok(na_w),
        pl.BlockSpec((None, None, N_MOD, d), lambda i: (l, cond_row(i), 0, 0)),
        _resident((len(POOL_WINDOWS), LANES, LANES), lambda i: (0, 0, 0)),
        pl.BlockSpec((1, pool_w), lambda i: (0, 0)),
        _resident((d, d), lambda i: (0, 0)),
        pl.BlockSpec((1, d), lambda i: (0, 0)),
    ]


def _postmix_ffn(x, pu, attn, mod, l, w_pool, pool_scale, w_out, norm_g, wg, wu, wd,
                 seq_info, cond_row):
    n, d = x.shape
    pool_w = pu.shape[1]
    n_chunks = wg.shape[0]
    kernel = functools.partial(_postmix_ffn_kernel, **seq_info)
    return pl.pallas_call(
        kernel,
        grid=(n // TOKEN_TILE,),
        in_specs=_postmix_specs(n, d, pool_w, attn.shape[1], l, cond_row) + [
            _resident((n_chunks, d, FF_CHUNK), lambda i: (0, 0, 0)),
            _resident((n_chunks, d, FF_CHUNK), lambda i: (0, 0, 0)),
            _resident((n_chunks, FF_CHUNK, d), lambda i: (0, 0, 0)),
        ],
        out_specs=pl.BlockSpec((TOKEN_TILE, d), lambda i: (i, 0)),
        out_shape=jax.ShapeDtypeStruct((n, d), F32),
        scratch_shapes=[
            pltpu.VMEM((TOKEN_TILE + 2 * POOL_HALO, pool_w), F32),
            pltpu.VMEM((TOKEN_TILE, d), BF16),
            pltpu.VMEM((TOKEN_TILE, d), F32),
        ],
        compiler_params=_params(),
        name=f"postmix_ffn{l}",
    )(x, pu, pu, pu, attn, mod, w_pool, pool_scale, w_out, norm_g, wg, wu, wd)


def _postmix_router(x, pu, attn, mod, l, w_pool, pool_scale, w_out, norm_g, r_hi, r_lo,
                    n_experts, seq_info, cond_row):
    n, d = x.shape
    pool_w = pu.shape[1]
    kernel = functools.partial(_postmix_router_kernel, n_experts=n_experts, **seq_info)
    row_tile = TOKEN_TILE * d // LANES
    return pl.pallas_call(
        kernel,
        grid=(n // TOKEN_TILE,),
        in_specs=_postmix_specs(n, d, pool_w, attn.shape[1], l, cond_row) + [
            pl.BlockSpec((d, ROUTER_LANES), lambda i: (0, 0)),
            pl.BlockSpec((d, ROUTER_LANES), lambda i: (0, 0)),
        ],
        out_specs=[
            pl.BlockSpec((TOKEN_TILE, d), lambda i: (i, 0)),
            pl.BlockSpec((row_tile, LANES), lambda i: (i, 0)),
            pl.BlockSpec((TOKEN_TILE, ROUTER_LANES), lambda i: (i, 0)),
            pl.BlockSpec((TOKEN_TILE, LANES), lambda i: (i, 0)),
            pl.BlockSpec((TOKEN_TILE, LANES), lambda i: (i, 0)),
        ],
        out_shape=[
            jax.ShapeDtypeStruct((n, d), F32),
            jax.ShapeDtypeStruct((n * d // LANES, LANES), F32),
            jax.ShapeDtypeStruct((n, ROUTER_LANES), F32),
            jax.ShapeDtypeStruct((n, LANES), F32),
            jax.ShapeDtypeStruct((n, LANES), F32),
        ],
        scratch_shapes=[pltpu.VMEM((TOKEN_TILE + 2 * POOL_HALO, pool_w), F32)],
        compiler_params=_params(),
        name=f"postmix_router{l}",
    )(x, pu, pu, pu, attn, mod, w_pool, pool_scale, w_out, norm_g, r_hi, r_lo)


def _moe_kernel(te_ref, na_ref, rt_ref, h_hbm, wg_ref, wu_ref, wd_ref, y_ref,
                xbuf, xb16, acc_ref, sem, *, chunks_per_row):
    j = pl.program_id(0)
    f = pl.program_id(1)
    n_f = pl.num_programs(1)
    tme = MOE_ROW_TILE
    rows8 = tme * chunks_per_row
    n_active = na_ref[0]
    active = j < n_active
    slot = j % 2

    def gather(tile, slot_):
        base = tile * tme

        def body(r, carry):
            tok = rt_ref[base + r]
            pltpu.make_async_copy(
                h_hbm.at[pl.ds(pl.multiple_of(tok * chunks_per_row, chunks_per_row), chunks_per_row)],
                xbuf.at[slot_, pl.ds(pl.multiple_of(r * chunks_per_row, chunks_per_row), chunks_per_row)],
                sem.at[slot_]).start()
            return carry

        lax.fori_loop(0, tme, body, 0)

    @pl.when((f == 0) & (j == 0) & active)
    def _():
        gather(0, 0)

    @pl.when((f == 0) & active)
    def _():
        pltpu.make_async_copy(h_hbm.at[pl.ds(0, rows8)], xbuf.at[slot], sem.at[slot]).wait()

        @pl.when(j + 1 < n_active)
        def _():
            gather(j + 1, 1 - slot)

        for s in range(chunks_per_row):
            xb16[:, s * LANES:(s + 1) * LANES] = xbuf[slot, pl.ds(s, tme, stride=chunks_per_row), :].astype(BF16)
        acc_ref[...] = jnp.zeros_like(acc_ref)

    @pl.when(active)
    def _():
        x = xb16[...]
        a = jnp.dot(x, wg_ref[...], preferred_element_type=F32)
        b = jnp.dot(x, wu_ref[...], preferred_element_type=F32)
        act = (a * jax.nn.sigmoid(a) * b).astype(BF16)
        acc_ref[...] += jnp.dot(act, wd_ref[...], preferred_element_type=F32)

    @pl.when((f == n_f - 1) & active)
    def _():
        for s in range(chunks_per_row):
            y_ref[pl.ds(s, tme, stride=chunks_per_row), :] = acc_ref[:, s * LANES:(s + 1) * LANES]

    @pl.when((f == n_f - 1) & jnp.logical_not(active))
    def _():
        y_ref[...] = jnp.zeros_like(y_ref)


def _moe(tile_expert, n_active, row_token, h_rows, wg, wu, wd, d):
    n_tiles = tile_expert.shape[0]
    n_exp, _, d_exp = wg.shape
    n_f = d_exp // MOE_F_CHUNK
    chunks_per_row = d // LANES
    rows8 = MOE_ROW_TILE * chunks_per_row

    def f_idx(j, f, na):
        return jnp.where(j < na[0], f, n_f - 1)

    grid_spec = pltpu.PrefetchScalarGridSpec(
        num_scalar_prefetch=3,
        grid=(n_tiles, n_f),
        in_specs=[
            pl.BlockSpec(memory_space=pl.ANY),
            pl.BlockSpec((None, d, MOE_F_CHUNK), lambda j, f, te, na, rt: (te[j], 0, f_idx(j, f, na))),
            pl.BlockSpec((None, d, MOE_F_CHUNK), lambda j, f, te, na, rt: (te[j], 0, f_idx(j, f, na))),
            pl.BlockSpec((None, MOE_F_CHUNK, d), lambda j, f, te, na, rt: (te[j], f_idx(j, f, na), 0)),
        ],
        out_specs=pl.BlockSpec((rows8, LANES), lambda j, f, te, na, rt: (j, 0)),
        scratch_shapes=[
            pltpu.VMEM((2, rows8, LANES), F32),
            pltpu.VMEM((MOE_ROW_TILE, d), BF16),
            pltpu.VMEM((MOE_ROW_TILE, d), F32),
            pltpu.SemaphoreType.DMA((2,)),
        ],
    )
    kernel = functools.partial(_moe_kernel, chunks_per_row=chunks_per_row)
    return pl.pallas_call(
        kernel,
        grid_spec=grid_spec,
        out_shape=jax.ShapeDtypeStruct((n_tiles * rows8, LANES), F32),
        compiler_params=_params(2),
        name="moe",
    )(tile_expert, n_active, row_token, h_rows, wg, wu, wd)


def _combine_kernel(pos_ref, y_hbm, x2_ref, g0_ref, g1_ref, mod_ref, o_ref, ybuf, sem, *,
                    chunks_per_row):
    i = pl.program_id(0)
    n_i = pl.num_programs(0)
    tmc = COMBINE_TILE
    n_rows = TOP_K * tmc
    slot = i % 2

    def gather(tile, slot_):
        base = tile * n_rows

        def body(r, carry):
            p = pos_ref[base + r]
            pltpu.make_async_copy(
                y_hbm.at[pl.ds(pl.multiple_of(p * chunks_per_row, chunks_per_row), chunks_per_row)],
                ybuf.at[slot_, pl.ds(pl.multiple_of(r * chunks_per_row, chunks_per_row), chunks_per_row)],
                sem.at[slot_]).start()
            return carry

        lax.fori_loop(0, n_rows, body, 0)

    @pl.when(i == 0)
    def _():
        gather(0, 0)

    pltpu.make_async_copy(y_hbm.at[pl.ds(0, n_rows * chunks_per_row)], ybuf.at[slot],
                          sem.at[slot]).wait()

    @pl.when(i + 1 < n_i)
    def _():
        gather(i + 1, 1 - slot)

    gate2 = mod_ref[...][5:6]
    g0 = g0_ref[...]
    g1 = g1_ref[...]
    stride = TOP_K * chunks_per_row
    for s in range(chunks_per_row):
        lanes = slice(s * LANES, (s + 1) * LANES)
        ya = ybuf[slot, pl.ds(s, tmc, stride=stride), :]
        yb = ybuf[slot, pl.ds(chunks_per_row + s, tmc, stride=stride), :]
        o_ref[:, lanes] = x2_ref[:, lanes] + gate2[:, lanes] * (g0 * ya + g1 * yb)


def _combine(pos, y_rows, x2, g0, g1, mod, l, cond_row_c):
    n, d = x2.shape
    chunks_per_row = d // LANES
    grid_spec = pltpu.PrefetchScalarGridSpec(
        num_scalar_prefetch=1,
        grid=(n // COMBINE_TILE,),
        in_specs=[
            pl.BlockSpec(memory_space=pl.ANY),
            pl.BlockSpec((COMBINE_TILE, d), lambda i, p: (i, 0)),
            pl.BlockSpec((COMBINE_TILE, LANES), lambda i, p: (i, 0)),
            pl.BlockSpec((COMBINE_TILE, LANES), lambda i, p: (i, 0)),
            pl.BlockSpec((None, None, N_MOD, d), lambda i, p: (l, cond_row_c(i), 0, 0)),
        ],
        out_specs=pl.BlockSpec((COMBINE_TILE, d), lambda i, p: (i, 0)),
        scratch_shapes=[
            pltpu.VMEM((2, TOP_K * COMBINE_TILE * chunks_per_row, LANES), F32),
            pltpu.SemaphoreType.DMA((2,)),
        ],
    )
    kernel = functools.partial(_combine_kernel, chunks_per_row=chunks_per_row)
    return pl.pallas_call(
        kernel,
        grid_spec=grid_spec,
        out_shape=jax.ShapeDtypeStruct((n, d), F32),
        compiler_params=_params(),
        name="combine",
    )(pos, y_rows, x2, g0, g1, mod)


def _dispatch_plan(expert_ids, n_experts, n_tiles):
    n = expert_ids.shape[0]
    e_flat = expert_ids.reshape(-1)
    onehot = (e_flat[:, None] == jnp.arange(n_experts)[None, :]).astype(jnp.int32)
    csum = jnp.cumsum(onehot, axis=0)
    rank = jnp.take_along_axis(csum, e_flat[:, None], axis=1)[:, 0] - 1
    counts = csum[-1]
    tiles = (counts + MOE_ROW_TILE - 1) // MOE_ROW_TILE
    tile_end = jnp.cumsum(tiles)
    row_off = (tile_end - tiles) * MOE_ROW_TILE
    pos = row_off[e_flat] + rank
    n_active = tile_end[-1]
    tile_ids = jnp.minimum(jnp.arange(n_tiles), n_active - 1)
    tile_expert = jnp.searchsorted(tile_end, tile_ids, side="right").astype(jnp.int32)
    tokens = jnp.arange(TOP_K * n, dtype=jnp.int32) // TOP_K
    row_token = jnp.zeros((n_tiles * MOE_ROW_TILE,), jnp.int32).at[pos].set(
        tokens, unique_indices=True)
    return pos.astype(jnp.int32), row_token, tile_expert, n_active.astype(jnp.int32).reshape(1)


def kernel(x_prompt, x_sample, cache_k, cache_v, c, c_ctx, norm1_g, norm2_g, w_ada, b_ada, w_in,
           q_norm_g, k_norm_g, w_pool, pool_scale, rpb, w_out, ffn_w_gate, ffn_w_up, ffn_w_down,
           moe_router, moe_w_gate, moe_w_up, moe_w_down):
    batch, seq, d = x_prompt.shape
    dec_batch, dec_seq, _ = x_sample.shape
    depth = w_in.shape[0]
    heads = cache_k.shape[3]
    na_w = heads * HEAD_DIM
    pool_w = d - na_w
    n_prompt = batch * seq
    n = n_prompt + dec_batch * dec_seq
    n_experts = moe_router.shape[2]
    assert pool_w == len(POOL_WINDOWS) * LANES and na_w % LANES == 0
    assert TOKEN_TILE % seq == 0 and n_prompt % dec_seq == 0 and dec_seq % TOKEN_TILE == 0
    assert dec_seq // GRID_W >= NA_WIN_R and dec_batch < COND_ROWS
    n_prompt_tiles = n_prompt // TOKEN_TILE
    tiles_per_seq = dec_seq // TOKEN_TILE
    seq_info = dict(n_prompt_tiles=n_prompt_tiles, seq=seq, dec_seq=dec_seq)

    def cond_row_for(tile):
        def f(i):
            start = i * tile
            return jnp.where(start < n_prompt, dec_batch, (start - n_prompt) // dec_seq)
        return f

    cond_row = cond_row_for(TOKEN_TILE)

    cond = jnp.zeros((COND_ROWS, d), F32).at[:dec_batch].set(c).at[dec_batch].set(c_ctx)
    mod = _ada(cond, w_ada, b_ada).reshape(depth, COND_ROWS, N_MOD, d)

    hsum = jnp.asarray(np.kron(np.eye(heads), np.ones((HEAD_DIM, HEAD_DIM))), BF16)
    ctx_k = cache_k.reshape(dec_batch, depth, cache_k.shape[2], na_w).astype(BF16)
    ctx_v = cache_v.reshape(dec_batch, depth, cache_v.shape[2], na_w).astype(BF16)

    x = jnp.concatenate([x_prompt.reshape(n_prompt, d), x_sample.reshape(-1, d)], axis=0)
    new_k, new_v = [], []
    for l in range(depth):
        pu, q, k, v, kf, vf = _premix(
            x, mod, l, norm1_g[l][None], w_in[l].astype(BF16),
            jnp.tile(q_norm_g[l], heads)[None], jnp.tile(k_norm_g[l], heads)[None], hsum,
            n_prompt, cond_row)
        new_k.append(kf.reshape(batch, seq, heads, HEAD_DIM))
        new_v.append(vf.reshape(batch, seq, heads, HEAD_DIM))
        attn = _attention(q, k, v, ctx_k, ctx_v, _relative_bias_table(rpb[l]), l,
                          n_prompt, seq, dec_seq)
        mix_args = (mod, l, w_pool[l].astype(BF16), pool_scale[l][None], w_out[l].astype(BF16),
                    norm2_g[l][None])
        if l % 2 == 0:
            li = l // 2
            d_ff = ffn_w_gate.shape[2]
            n_chunks = d_ff // FF_CHUNK
            wg = ffn_w_gate[li].astype(BF16).reshape(d, n_chunks, FF_CHUNK).transpose(1, 0, 2)
            wu = ffn_w_up[li].astype(BF16).reshape(d, n_chunks, FF_CHUNK).transpose(1, 0, 2)
            wd = ffn_w_down[li].astype(BF16).reshape(n_chunks, FF_CHUNK, d)
            x = _postmix_ffn(x, pu, attn, *mix_args, wg, wu, wd, seq_info, cond_row)
        else:
            li = l // 2
            router = jnp.zeros((d, ROUTER_LANES), F32).at[:, :n_experts].set(moe_router[li])
            r_hi = router.astype(BF16)
            r_lo = (router - r_hi.astype(F32)).astype(BF16)
            x2, h_rows, rout, g0, g1 = _postmix_router(
                x, pu, attn, *mix_args, r_hi, r_lo, n_experts, seq_info, cond_row)
            expert_ids = rout[:, :TOP_K].astype(jnp.int32)
            n_moe_tiles = (TOP_K * n + n_experts * (MOE_ROW_TILE - 1)) // MOE_ROW_TILE
            pos, row_token, tile_expert, n_active = _dispatch_plan(expert_ids, n_experts, n_moe_tiles)
            y_rows = _moe(tile_expert, n_active, row_token, h_rows,
                          moe_w_gate[li].astype(BF16), moe_w_up[li].astype(BF16),
                          moe_w_down[li].astype(BF16), d)
            x = _combine(pos, y_rows, x2, g0, g1, mod, l, cond_row_for(COMBINE_TILE))

    y_prompt = x[:n_prompt].reshape(batch, seq, d)
    y_sample = x[n_prompt:].reshape(dec_batch, dec_seq, d)
    return (y_prompt, y_sample, jnp.stack(new_k, axis=1), jnp.stack(new_v, axis=1))
```

```python
import functools

import numpy as np
import jax
import jax.numpy as jnp
from jax import lax
from jax.experimental import pallas as pl
from jax.experimental.pallas import tpu as pltpu

F32 = jnp.float32
BF16 = jnp.bfloat16

GRID_W = 64
POOL_WINDOWS = (2, 4, 8, 16)
HEAD_DIM = 64
NA_WIN_R = 8
NA_WIN_C = 16
N_MOD = 6
TOP_K = 2
NORM_EPS = 1e-6
LOG2_E = 1.4426950408889634

LANES = 128
SUBLANES = 8
VMEM_LIMIT_BYTES = 56 * 1024 * 1024

TOKEN_TILE = 512
POOL_HALO = 8
FF_CHUNK = 256
MOE_ROW_TILE = 1008
MOE_F_CHUNK = 512
ADA_COL_TILE = 1024
COND_ROWS = 16
ROUTER_LANES = 128


def _params(n_axes=1):
    return pltpu.CompilerParams(
        dimension_semantics=("arbitrary",) * n_axes,
        vmem_limit_bytes=VMEM_LIMIT_BYTES,
    )


def _resident(shape, index_map):
    return pl.BlockSpec(shape, index_map, pipeline_mode=pl.Buffered(1))


def _ada_kernel(c_ref, w_ref, b_ref, o_ref):
    c = c_ref[...]
    s = c * jax.nn.sigmoid(c)
    o_ref[...] = jnp.dot(s.astype(BF16), w_ref[...].astype(BF16),
                         preferred_element_type=F32) + b_ref[...]


def _ada(cond, w_ada, b_ada):
    depth, d, width = w_ada.shape
    return pl.pallas_call(
        _ada_kernel,
        grid=(depth, width // ADA_COL_TILE),
        in_specs=[
            pl.BlockSpec((COND_ROWS, d), lambda l, j: (0, 0)),
            pl.BlockSpec((None, d, ADA_COL_TILE), lambda l, j: (l, 0, j)),
            pl.BlockSpec((None, 1, ADA_COL_TILE), lambda l, j: (l, 0, j)),
        ],
        out_specs=pl.BlockSpec((None, COND_ROWS, ADA_COL_TILE), lambda l, j: (l, 0, j)),
        out_shape=jax.ShapeDtypeStruct((depth, COND_ROWS, width), F32),
        compiler_params=_params(2),
        name="ada",
    )(cond, w_ada, b_ada.reshape(depth, 1, width))


def _premix_kernel(x_ref, mod_ref, g_ref, w_ref, qg_ref, kg_ref, hsum_ref,
                   pu_ref, q_ref, k_ref, v_ref, kf_ref, vf_ref, *,
                   n_prompt_tiles, pool_w, na_w):
    i = pl.program_id(0)
    x = x_ref[...]
    inv = lax.rsqrt(jnp.mean(x * x, axis=-1, keepdims=True) + NORM_EPS)
    mod = mod_ref[...]
    h = (x * inv * g_ref[...]) * (1.0 + mod[1:2]) + mod[0:1]
    u = jnp.dot(h.astype(BF16), w_ref[...], preferred_element_type=F32)
    pu_ref[...] = u[:, :pool_w]
    q = u[:, pool_w:pool_w + na_w]
    k = u[:, pool_w + na_w:pool_w + 2 * na_w]
    v = u[:, pool_w + 2 * na_w:]

    def head_norm(t, g):
        ms = jnp.dot((t * t).astype(BF16), hsum_ref[...],
                     preferred_element_type=F32) * (1.0 / HEAD_DIM)
        return t * lax.rsqrt(ms + NORM_EPS) * g

    qn = head_norm(q, qg_ref[...])
    kn = head_norm(k, kg_ref[...])
    q_ref[...] = (qn * (HEAD_DIM ** -0.5 * LOG2_E)).astype(BF16)
    k_ref[...] = kn.astype(BF16)
    v_ref[...] = v.astype(BF16)

    @pl.when(i < n_prompt_tiles)
    def _():
        kf_ref[...] = kn
        vf_ref[...] = v


def _premix(x, mod, l, norm_g, w_in, q_g, k_g, hsum, n_prompt, cond_row):
    n, d = x.shape
    pool_w = d // 2
    na_w = d - pool_w
    n_tiles = n // TOKEN_TILE
    n_prompt_tiles = n_prompt // TOKEN_TILE
    last_p = n_prompt_tiles - 1
    tok = lambda w: pl.BlockSpec((TOKEN_TILE, w), lambda i: (i, 0))
    ptok = pl.BlockSpec((TOKEN_TILE, na_w), lambda i: (jnp.minimum(i, last_p), 0))
    kernel = functools.partial(_premix_kernel, n_prompt_tiles=n_prompt_tiles,
                               pool_w=pool_w, na_w=na_w)
    return pl.pallas_call(
        kernel,
        grid=(n_tiles,),
        in_specs=[
            tok(d),
            pl.BlockSpec((None, None, N_MOD, d), lambda i: (l, cond_row(i), 0, 0)),
            pl.BlockSpec((1, d), lambda i: (0, 0)),
            _resident((d, w_in.shape[1]), lambda i: (0, 0)),
            pl.BlockSpec((1, na_w), lambda i: (0, 0)),
            pl.BlockSpec((1, na_w), lambda i: (0, 0)),
            _resident((na_w, na_w), lambda i: (0, 0)),
        ],
        out_specs=[tok(pool_w), tok(na_w), tok(na_w), tok(na_w), ptok, ptok],
        out_shape=[
            jax.ShapeDtypeStruct((n, pool_w), F32),
            jax.ShapeDtypeStruct((n, na_w), BF16),
            jax.ShapeDtypeStruct((n, na_w), BF16),
            jax.ShapeDtypeStruct((n, na_w), BF16),
            jax.ShapeDtypeStruct((n_prompt, na_w), F32),
            jax.ShapeDtypeStruct((n_prompt, na_w), F32),
        ],
        compiler_params=_params(),
        name=f"premix{l}",
    )(x, mod, norm_g, w_in, q_g, k_g, hsum)


_NT = (((1,), (1,)), ((), ()))
_TN = (((0,), (0,)), ((), ()))


def _block_diag_queries(q2):
    lo = lax.broadcasted_iota(jnp.int32, q2.shape, 1) < HEAD_DIM
    zero = jnp.zeros_like(q2)
    return jnp.concatenate([jnp.where(lo, q2, zero), jnp.where(lo, zero, q2)], axis=0)


def _pick_head_blocks(o, nq):
    lo = lax.broadcasted_iota(jnp.int32, (nq, LANES), 1) < HEAD_DIM
    return jnp.where(lo, o[:nq], o[nq:])


def _pair_attention(q2, k, v):
    nq = q2.shape[0]
    s = lax.dot_general(k, _block_diag_queries(q2), _NT, preferred_element_type=F32)
    p = jnp.exp2(s - jnp.max(s, axis=0, keepdims=True))
    r = 1.0 / jnp.sum(p, axis=0, keepdims=True)
    o = lax.dot_general((p * r).astype(BF16), v, _TN, preferred_element_type=F32)
    return _pick_head_blocks(o, nq)


def _attn_kernel(q_ref, k_ref, v_ref, ck_ref, cv_ref, bias_ref, o_ref, s_ref, p_ref, *,
                 n_prompt_tiles, seq, dec_seq, n_pairs):
    i = pl.program_id(0)
    tiles_per_seq = dec_seq // TOKEN_TILE
    rows_per_tile = TOKEN_TILE // GRID_W
    rows = dec_seq // GRID_W
    win_keys = NA_WIN_R * GRID_W
    block_off = (i * TOKEN_TILE) % dec_seq

    @pl.when(i < n_prompt_tiles)
    def _():
        for s in range(TOKEN_TILE // seq):
            start = pl.multiple_of(block_off + s * seq, seq)
            for hp in range(n_pairs):
                lanes = slice(hp * LANES, (hp + 1) * LANES)
                out = _pair_attention(q_ref[s * seq:(s + 1) * seq, lanes],
                                      k_ref[pl.ds(start, seq), lanes],
                                      v_ref[pl.ds(start, seq), lanes])
                o_ref[s * seq:(s + 1) * seq, lanes] = out.astype(o_ref.dtype)

    @pl.when(i >= n_prompt_tiles)
    def _():
        row0 = ((i - n_prompt_tiles) % tiles_per_seq) * rows_per_tile

        def row_body(rl, carry):
            r = row0 + rl
            r0 = jnp.clip(r - NA_WIN_R // 2, 0, rows - NA_WIN_R)
            d0 = r0 - r + NA_WIN_R - 1
            kstart = pl.multiple_of(r0 * GRID_W, GRID_W)
            qstart = pl.multiple_of(rl * GRID_W, GRID_W)
            for hp in range(n_pairs):
                lanes = slice(hp * LANES, (hp + 1) * LANES)
                qbd = _block_diag_queries(q_ref[pl.ds(qstart, GRID_W), lanes])
                bias = bias_ref[hp, pl.ds(d0, NA_WIN_R)].reshape(win_keys, LANES)
                s_ref[hp, 0:win_keys] = lax.dot_general(
                    k_ref[pl.ds(kstart, win_keys), lanes], qbd, _NT,
                    preferred_element_type=F32) + bias
                s_ref[hp, win_keys:] = lax.dot_general(
                    ck_ref[:, lanes], qbd, _NT, preferred_element_type=F32)
            for hp in range(n_pairs):
                s = s_ref[hp]
                p = jnp.exp2(s - jnp.max(s, axis=0, keepdims=True))
                rr = 1.0 / jnp.sum(p, axis=0, keepdims=True)
                p_ref[hp] = (p * rr).astype(BF16)
            for hp in range(n_pairs):
                lanes = slice(hp * LANES, (hp + 1) * LANES)
                o = (lax.dot_general(p_ref[hp, 0:win_keys], v_ref[pl.ds(kstart, win_keys), lanes],
                                     _TN, preferred_element_type=F32)
                     + lax.dot_general(p_ref[hp, win_keys:], cv_ref[:, lanes], _TN,
                                       preferred_element_type=F32))
                o_ref[pl.ds(qstart, GRID_W), lanes] = _pick_head_blocks(o, GRID_W).astype(o_ref.dtype)
            return carry

        lax.fori_loop(0, rows_per_tile, row_body, 0)


def _attention(q, k, v, ctx_k, ctx_v, bias, l, n_prompt, seq, dec_seq):
    n, na_w = q.shape
    n_tiles = n // TOKEN_TILE
    n_prompt_tiles = n_prompt // TOKEN_TILE
    tiles_per_seq = dec_seq // TOKEN_TILE
    past = ctx_k.shape[2]
    n_pairs = na_w // LANES
    n_keys = NA_WIN_R * GRID_W + past
    kv_spec = pl.BlockSpec((dec_seq, na_w), lambda i: (i * TOKEN_TILE // dec_seq, 0))
    ctx_spec = pl.BlockSpec(
        (None, None, past, na_w),
        lambda i: (jnp.maximum(i - n_prompt_tiles, 0) // tiles_per_seq, l, 0, 0))
    kernel = functools.partial(_attn_kernel, n_prompt_tiles=n_prompt_tiles, seq=seq,
                               dec_seq=dec_seq, n_pairs=n_pairs)
    return pl.pallas_call(
        kernel,
        grid=(n_tiles,),
        in_specs=[
            pl.BlockSpec((TOKEN_TILE, na_w), lambda i: (i, 0)),
            kv_spec, kv_spec, ctx_spec, ctx_spec,
            _resident(bias.shape, lambda i: (0, 0, 0, 0)),
        ],
        out_specs=pl.BlockSpec((TOKEN_TILE, na_w), lambda i: (i, 0)),
        out_shape=jax.ShapeDtypeStruct((n, na_w), BF16),
        scratch_shapes=[pltpu.VMEM((n_pairs, n_keys, LANES), F32),
                        pltpu.VMEM((n_pairs, n_keys, LANES), BF16)],
        compiler_params=_params(),
        name=f"attn{l}",
    )(q, k, v, ctx_k, ctx_v, bias)


def _relative_bias_table(rpb_l):
    heads = rpb_l.shape[0]
    kc = np.arange(GRID_W)[:, None]
    qc = np.arange(GRID_W)[None, :]
    q_start = np.clip(qc - NA_WIN_C // 2, 0, GRID_W - NA_WIN_C)
    valid = (kc >= q_start) & (kc < q_start + NA_WIN_C)
    dc_idx = np.clip(kc - qc, -(NA_WIN_C - 1), NA_WIN_C - 1) + NA_WIN_C - 1
    t = rpb_l[:, :, dc_idx]
    t = jnp.where(valid[None, None], t.astype(F32) * LOG2_E, -jnp.inf)
    t = t.reshape(heads // 2, 2, 2 * NA_WIN_R - 1, GRID_W, GRID_W)
    return t.transpose(0, 2, 3, 1, 4).reshape(heads // 2, 2 * NA_WIN_R - 1, GRID_W, 2 * GRID_W)


def _mod_static(t, m):
    return t & (m - 1) if m & (m - 1) == 0 else lax.rem(t, m)


def _pool_mix(i, pu_ref, pp_ref, pn_ref, ext_ref, wp_ref, ps_ref, *, n_prompt_tiles, seq, dec_seq):
    tm = TOKEN_TILE
    ext_ref[0:POOL_HALO] = pp_ref[...]
    ext_ref[POOL_HALO:POOL_HALO + tm] = pu_ref[...]
    ext_ref[POOL_HALO + tm:] = pn_ref[...]
    is_prompt = i < n_prompt_tiles
    seq_len = jnp.where(is_prompt, seq, dec_seq)
    t = i * tm + lax.broadcasted_iota(jnp.int32, (tm, LANES), 0)
    pos = jnp.where(is_prompt, _mod_static(t, seq), _mod_static(t, dec_seq))
    outs = []
    for g, window in enumerate(POOL_WINDOWS):
        half = window // 2
        lanes = slice(g * LANES, (g + 1) * LANES)
        total = jnp.zeros((tm, LANES), F32)
        for j in range(-half, half):
            term = ext_ref[POOL_HALO + j:POOL_HALO + j + tm, lanes]
            ok = (pos + j >= 0) if j < 0 else (pos + j < seq_len)
            total = total + jnp.where(ok, term, 0.0)
        count = (jnp.minimum(pos + half, seq_len) - jnp.maximum(pos - half, 0)).astype(F32)
        pooled = total / count - pu_ref[:, lanes]
        outs.append(jnp.dot(pooled.astype(BF16), wp_ref[g], preferred_element_type=F32))
    return jnp.concatenate(outs, axis=-1) * ps_ref[...]


def _mix_and_norm(i, x_ref, pu_ref, pp_ref, pn_ref, at_ref, mod_ref, wp_ref, ps_ref, wo_ref,
                  g2_ref, ext_ref, **seq_info):
    pool_out = _pool_mix(i, pu_ref, pp_ref, pn_ref, ext_ref, wp_ref, ps_ref, **seq_info)
    mixed = jnp.concatenate([pool_out.astype(BF16), at_ref[...]], axis=-1)
    mix = jnp.dot(mixed, wo_ref[...], preferred_element_type=F32)
    mod = mod_ref[...]
    x2 = x_ref[...] + mod[2:3] * mix
    inv = lax.rsqrt(jnp.mean(x2 * x2, axis=-1, keepdims=True) + NORM_EPS)
    h2 = (x2 * inv * g2_ref[...]) * (1.0 + mod[4:5]) + mod[3:4]
    return x2, h2, mod[5:6]


def _postmix_ffn_kernel(x_ref, pu_ref, pp_ref, pn_ref, at_ref, mod_ref, wp_ref, ps_ref, wo_ref,
                        g2_ref, wg_ref, wu_ref, wd_ref, y_ref, ext_ref, h_ref, acc_ref,
                        **seq_info):
    i = pl.program_id(0)
    x2, h2, gate2 = _mix_and_norm(i, x_ref, pu_ref, pp_ref, pn_ref, at_ref, mod_ref, wp_ref,
                                  ps_ref, wo_ref, g2_ref, ext_ref, **seq_info)
    h_ref[...] = h2.astype(BF16)
    acc_ref[...] = jnp.zeros_like(acc_ref)

    def chunk(c, carry):
        h = h_ref[...]
        a = jnp.dot(h, wg_ref[c], preferred_element_type=F32)
        b = jnp.dot(h, wu_ref[c], preferred_element_type=F32)
        act = (a * jax.nn.sigmoid(a) * b).astype(BF16)
        acc_ref[...] += jnp.dot(act, wd_ref[c], preferred_element_type=F32)
        return carry

    lax.fori_loop(0, wg_ref.shape[0], chunk, 0, unroll=True)
    y_ref[...] = x2 + gate2 * acc_ref[...]


def _postmix_router_kernel(x_ref, pu_ref, pp_ref, pn_ref, at_ref, mod_ref, wp_ref, ps_ref, wo_ref,
                           g2_ref, rh_ref, rl_ref, x2_ref, hr_ref, rout_ref, g0_ref, g1_ref,
                           ext_ref, *, n_experts, **seq_info):
    i = pl.program_id(0)
    x2, h2, _ = _mix_and_norm(i, x_ref, pu_ref, pp_ref, pn_ref, at_ref, mod_ref, wp_ref,
                              ps_ref, wo_ref, g2_ref, ext_ref, **seq_info)
    x2_ref[...] = x2
    tm = TOKEN_TILE
    for s in range(h2.shape[1] // LANES):
        hr_ref[pl.ds(s, tm, stride=SUBLANES), :] = h2[:, s * LANES:(s + 1) * LANES]
    hi = h2.astype(BF16)
    lo = (h2 - hi.astype(F32)).astype(BF16)
    logits = (jnp.dot(hi, rh_ref[...], preferred_element_type=F32)
              + jnp.dot(lo, rh_ref[...], preferred_element_type=F32)
              + jnp.dot(hi, rl_ref[...], preferred_element_type=F32))
    lane = lax.broadcasted_iota(jnp.int32, logits.shape, 1)
    logits = jnp.where(lane < n_experts, logits, -jnp.inf)
    m1 = jnp.max(logits, axis=-1, keepdims=True)
    i1 = jnp.min(jnp.where(logits == m1, lane, ROUTER_LANES), axis=-1, keepdims=True)
    rest = jnp.where(lane == i1, -jnp.inf, logits)
    m2 = jnp.max(rest, axis=-1, keepdims=True)
    i2 = jnp.min(jnp.where(rest == m2, lane, ROUTER_LANES), axis=-1, keepdims=True)
    e = jnp.exp(m2 - m1)
    gate_a = 1.0 / (1.0 + e)
    gate_b = e / (1.0 + e)
    rout_ref[...] = jnp.where(lane == 0, i1.astype(F32), jnp.where(lane == 1, i2.astype(F32), 0.0))
    g0_ref[...] = jnp.broadcast_to(gate_a, g0_ref.shape)
    g1_ref[...] = jnp.broadcast_to(gate_b, g1_ref.shape)


def _postmix_specs(n, d, pool_w, na_w, l, cond_row):
    n_halo_blocks = n // POOL_HALO
    per_tile = TOKEN_TILE // POOL_HALO
    tok = lambda w: pl.BlockSpec((TOKEN_TILE, w), lambda i: (i, 0))
    return [
        tok(d),
        tok(pool_w),
        pl.BlockSpec((POOL_HALO, pool_w), lambda i: (jnp.maximum(i * per_tile - 1, 0), 0)),
        pl.BlockSpec((POOL_HALO, pool_w),
                     lambda i: (jnp.minimum((i + 1) * per_tile, n_halo_blocks - 1), 0)),
        tok(na_w),
        pl.BlockSpec((None, None, N_MOD, d), lambda i: (l, cond_row(i), 0, 0)),
        _resident((len(POOL_WINDOWS), LANES, LANES), lambda i: (0, 0, 0)),
        pl.BlockSpec((1, pool_w), lambda i: (0, 0)),
        _resident((d, d), lambda i: (0, 0)),
        pl.BlockSpec((1, d), lambda i: (0, 0)),
    ]


def _postmix_ffn(x, pu, attn, mod, l, w_pool, pool_scale, w_out, norm_g, wg, wu, wd,
                 seq_info, cond_row):
    n, d = x.shape
    pool_w = pu.shape[1]
    n_chunks = wg.shape[0]
    kernel = functools.partial(_postmix_ffn_kernel, **seq_info)
    return pl.pallas_call(
        kernel,
        grid=(n // TOKEN_TILE,),
        in_specs=_postmix_specs(n, d, pool_w, attn.shape[1], l, cond_row) + [
            _resident((n_chunks, d, FF_CHUNK), lambda i: (0, 0, 0)),
            _resident((n_chunks, d, FF_CHUNK), lambda i: (0, 0, 0)),
            _resident((n_chunks, FF_CHUNK, d), lambda i: (0, 0, 0)),
        ],
        out_specs=pl.BlockSpec((TOKEN_TILE, d), lambda i: (i, 0)),
        out_shape=jax.ShapeDtypeStruct((n, d), F32),
        scratch_shapes=[
            pltpu.VMEM((TOKEN_TILE + 2 * POOL_HALO, pool_w), F32),
            pltpu.VMEM((TOKEN_TILE, d), BF16),
            pltpu.VMEM((TOKEN_TILE, d), F32),
        ],
        compiler_params=_params(),
        name=f"postmix_ffn{l}",
    )(x, pu, pu, pu, attn, mod, w_pool, pool_scale, w_out, norm_g, wg, wu, wd)


def _postmix_router(x, pu, attn, mod, l, w_pool, pool_scale, w_out, norm_g, r_hi, r_lo,
                    n_experts, seq_info, cond_row):
    n, d = x.shape
    pool_w = pu.shape[1]
    kernel = functools.partial(_postmix_router_kernel, n_experts=n_experts, **seq_info)
    row_tile = TOKEN_TILE * d // LANES
    return pl.pallas_call(
        kernel,
        grid=(n // TOKEN_TILE,),
        in_specs=_postmix_specs(n, d, pool_w, attn.shape[1], l, cond_row) + [
            pl.BlockSpec((d, ROUTER_LANES), lambda i: (0, 0)),
            pl.BlockSpec((d, ROUTER_LANES), lambda i: (0, 0)),
        ],
        out_specs=[
            pl.BlockSpec((TOKEN_TILE, d), lambda i: (i, 0)),
            pl.BlockSpec((row_tile, LANES), lambda i: (i, 0)),
            pl.BlockSpec((TOKEN_TILE, ROUTER_LANES), lambda i: (i, 0)),
            pl.BlockSpec((TOKEN_TILE, LANES), lambda i: (i, 0)),
            pl.BlockSpec((TOKEN_TILE, LANES), lambda i: (i, 0)),
        ],
        out_shape=[
            jax.ShapeDtypeStruct((n, d), F32),
            jax.ShapeDtypeStruct((n * d // LANES, LANES), F32),
            jax.ShapeDtypeStruct((n, ROUTER_LANES), F32),
            jax.ShapeDtypeStruct((n, LANES), F32),
            jax.ShapeDtypeStruct((n, LANES), F32),
        ],
        scratch_shapes=[pltpu.VMEM((TOKEN_TILE + 2 * POOL_HALO, pool_w), F32)],
        compiler_params=_params(),
        name=f"postmix_router{l}",
    )(x, pu, pu, pu, attn, mod, w_pool, pool_scale, w_out, norm_g, r_hi, r_lo)


def _moe_kernel(te_ref, na_ref, rf_ref, h_hbm, wg_ref, wu_ref, wd_ref, y_hbm,
                xbuf, ybuf, xb16, acc_ref, gsem, ssem, zsem, *,
                cpr, rows_per_step, n_flat, n_dump_tiles):
    j = pl.program_id(0)
    f = pl.program_id(1)
    n_f = pl.num_programs(1)
    tme = MOE_ROW_TILE
    tile_rows = tme * cpr
    n_active = na_ref[0]
    last_token = n_flat // TOP_K - 1

    def gather_row(tile, slot, row):
        v = rf_ref[(tile + 1) * tme + row]
        tok = jnp.minimum(lax.shift_right_logical(v, 1), last_token)
        return pltpu.make_async_copy(
            h_hbm.at[pl.ds(pl.multiple_of(tok * cpr, cpr), cpr)],
            xbuf.at[slot, pl.ds(pl.multiple_of(row * cpr, cpr), cpr)],
            gsem.at[slot])

    def scatter_row(tile, slot, row):
        v = rf_ref[(tile + 1) * tme + row]
        return pltpu.make_async_copy(
            ybuf.at[slot, pl.ds(pl.multiple_of(row * cpr, cpr), cpr)],
            y_hbm.at[pl.ds(pl.multiple_of(v * cpr, cpr), cpr)],
            ssem.at[slot])

    def wait_gather(slot):
        pltpu.make_async_copy(h_hbm.at[pl.ds(0, tile_rows)], xbuf.at[slot], gsem.at[slot]).wait()

    def wait_scatter(slot):
        pltpu.make_async_copy(ybuf.at[slot], y_hbm.at[pl.ds(0, tile_rows)], ssem.at[slot]).wait()

    def dump_fill(t):
        return pltpu.make_async_copy(
            ybuf.at[1], y_hbm.at[pl.ds((n_flat + t * tme) * cpr, tile_rows)], zsem)

    @pl.when((f == 0) & (j == 0))
    def _():
        ybuf[1] = jnp.zeros(ybuf.shape[1:], ybuf.dtype)
        for t in range(n_dump_tiles):
            dump_fill(t).start()
        for t in range(n_dump_tiles):
            dump_fill(t).wait()

        def body(r, carry):
            gather_row(0, 0, r).start()
            return carry

        lax.fori_loop(0, tme, body, 0)

    for par in range(2):
        @pl.when((j < n_active) & (j % 2 == par))
        def _(par=par):
            @pl.when(f == 0)
            def _():
                wait_gather(par)
                for s in range(cpr):
                    xb16[:, s * LANES:(s + 1) * LANES] = (
                        xbuf[par, pl.ds(s, tme, stride=cpr), :].astype(BF16))
                acc_ref[...] = jnp.zeros_like(acc_ref)

            for t in range(rows_per_step):
                gather_row(j + 1, 1 - par, t * n_f + f).start()
                scatter_row(j - 1, 1 - par, t * n_f + f).start()
            x = xb16[...]
            a = jnp.dot(x, wg_ref[...], preferred_element_type=F32)
            b = jnp.dot(x, wu_ref[...], preferred_element_type=F32)
            act = (a * jax.nn.sigmoid(a) * b).astype(BF16)
            acc_ref[...] += jnp.dot(act, wd_ref[...], preferred_element_type=F32)

            @pl.when(f == n_f - 1)
            def _():
                @pl.when(j >= 1)
                def _():
                    wait_scatter(par)

                for s in range(cpr):
                    ybuf[par, pl.ds(s, tme, stride=cpr), :] = acc_ref[:, s * LANES:(s + 1) * LANES]

        @pl.when((j == n_active) & (f == 0) & (j % 2 == par))
        def _(par=par):
            def body(r, carry):
                scatter_row(j - 1, 1 - par, r).start()
                return carry

            lax.fori_loop(0, tme, body, 0)
            wait_gather(par)
            wait_scatter(par)
            wait_scatter(1 - par)


def _moe(tile_expert, n_active, row_flat, h_rows, wg, wu, wd, d, n_flat, n_dump_tiles):
    n_steps = tile_expert.shape[0]
    d_exp = wg.shape[2]
    n_f = d_exp // MOE_F_CHUNK
    cpr = d // LANES
    tile_rows = MOE_ROW_TILE * cpr
    assert MOE_ROW_TILE % n_f == 0

    def f_idx(j, f, na):
        return jnp.where(j < na[0], f, n_f - 1)

    grid_spec = pltpu.PrefetchScalarGridSpec(
        num_scalar_prefetch=3,
        grid=(n_steps, n_f),
        in_specs=[
            pl.BlockSpec(memory_space=pl.ANY),
            pl.BlockSpec((None, d, MOE_F_CHUNK), lambda j, f, te, na, rf: (te[j], 0, f_idx(j, f, na))),
            pl.BlockSpec((None, d, MOE_F_CHUNK), lambda j, f, te, na, rf: (te[j], 0, f_idx(j, f, na))),
            pl.BlockSpec((None, MOE_F_CHUNK, d), lambda j, f, te, na, rf: (te[j], f_idx(j, f, na), 0)),
        ],
        out_specs=pl.BlockSpec(memory_space=pl.ANY),
        scratch_shapes=[
            pltpu.VMEM((2, tile_rows, LANES), F32),
            pltpu.VMEM((2, tile_rows, LANES), F32),
            pltpu.VMEM((MOE_ROW_TILE, d), BF16),
            pltpu.VMEM((MOE_ROW_TILE, d), F32),
            pltpu.SemaphoreType.DMA((2,)),
            pltpu.SemaphoreType.DMA((2,)),
            pltpu.SemaphoreType.DMA,
        ],
    )
    kernel = functools.partial(_moe_kernel, cpr=cpr, rows_per_step=MOE_ROW_TILE // n_f,
                               n_flat=n_flat, n_dump_tiles=n_dump_tiles)
    return pl.pallas_call(
        kernel,
        grid_spec=grid_spec,
        out_shape=jax.ShapeDtypeStruct(((n_flat + n_dump_tiles * MOE_ROW_TILE) * cpr, LANES), F32),
        compiler_params=_params(2),
        name="moe",
    )(tile_expert, n_active, row_flat, h_rows, wg, wu, wd)


def _combine_kernel(y_ref, x2_ref, g0_ref, g1_ref, mod_ref, o_ref, *, cpr):
    tm = x2_ref.shape[0]
    gate2 = mod_ref[...][5:6]
    g0 = g0_ref[...]
    g1 = g1_ref[...]
    stride = TOP_K * cpr
    for s in range(cpr):
        lanes = slice(s * LANES, (s + 1) * LANES)
        ya = y_ref[pl.ds(s, tm, stride=stride), :]
        yb = y_ref[pl.ds(cpr + s, tm, stride=stride), :]
        o_ref[:, lanes] = x2_ref[:, lanes] + gate2[:, lanes] * (g0 * ya + g1 * yb)


def _combine(y_rows, x2, g0, g1, mod, l, cond_row):
    n, d = x2.shape
    cpr = d // LANES
    kernel = functools.partial(_combine_kernel, cpr=cpr)
    return pl.pallas_call(
        kernel,
        grid=(n // TOKEN_TILE,),
        in_specs=[
            pl.BlockSpec((TOKEN_TILE * TOP_K * cpr, LANES), lambda i: (i, 0)),
            pl.BlockSpec((TOKEN_TILE, d), lambda i: (i, 0)),
            pl.BlockSpec((TOKEN_TILE, LANES), lambda i: (i, 0)),
            pl.BlockSpec((TOKEN_TILE, LANES), lambda i: (i, 0)),
            pl.BlockSpec((None, None, N_MOD, d), lambda i: (l, cond_row(i), 0, 0)),
        ],
        out_specs=pl.BlockSpec((TOKEN_TILE, d), lambda i: (i, 0)),
        out_shape=jax.ShapeDtypeStruct((n, d), F32),
        compiler_params=_params(),
        name="combine",
    )(y_rows, x2, g0, g1, mod)


def _dispatch_plan(expert_ids, n_experts, n_tiles_max):
    tme = MOE_ROW_TILE
    n_flat = expert_ids.size
    e_flat = expert_ids.reshape(-1)
    onehot = (e_flat[:, None] == jnp.arange(n_experts)[None, :]).astype(jnp.int32)
    csum = jnp.cumsum(onehot, axis=0)
    rank = jnp.take_along_axis(csum, e_flat[:, None], axis=1)[:, 0] - 1
    counts = csum[-1]
    tiles = (counts + tme - 1) // tme
    tile_end = jnp.cumsum(tiles)
    row_off = (tile_end - tiles) * tme
    pos = tme + row_off[e_flat] + rank
    n_active = tile_end[-1]
    n_steps = n_tiles_max + 1
    tile_ids = jnp.minimum(jnp.arange(n_steps), n_active - 1)
    tile_expert = jnp.sum(tile_ids[:, None] >= tile_end[None, :], axis=1).astype(jnp.int32)
    n_dump_tiles = n_experts + 1
    rf = jnp.full(((n_steps + 1) * tme,), -1, jnp.int32).at[pos].set(
        jnp.arange(n_flat, dtype=jnp.int32), unique_indices=True)
    is_pad = rf < 0
    pad_slot = jnp.minimum(jnp.cumsum(is_pad.astype(jnp.int32)) - 1, n_dump_tiles * tme - 1)
    rf = jnp.where(is_pad, n_flat + pad_slot, rf)
    return rf, tile_expert, n_active.astype(jnp.int32).reshape(1), n_dump_tiles


def kernel(x_prompt, x_sample, cache_k, cache_v, c, c_ctx, norm1_g, norm2_g, w_ada, b_ada, w_in,
           q_norm_g, k_norm_g, w_pool, pool_scale, rpb, w_out, ffn_w_gate, ffn_w_up, ffn_w_down,
           moe_router, moe_w_gate, moe_w_up, moe_w_down):
    batch, seq, d = x_prompt.shape
    dec_batch, dec_seq, _ = x_sample.shape
    depth = w_in.shape[0]
    heads = cache_k.shape[3]
    na_w = heads * HEAD_DIM
    pool_w = d - na_w
    n_prompt = batch * seq
    n = n_prompt + dec_batch * dec_seq
    n_experts = moe_router.shape[2]
    assert pool_w == len(POOL_WINDOWS) * LANES and na_w % LANES == 0
    assert TOKEN_TILE % seq == 0 and n_prompt % dec_seq == 0 and dec_seq % TOKEN_TILE == 0
    assert dec_seq // GRID_W >= NA_WIN_R and dec_batch < COND_ROWS
    n_prompt_tiles = n_prompt // TOKEN_TILE
    seq_info = dict(n_prompt_tiles=n_prompt_tiles, seq=seq, dec_seq=dec_seq)

    def cond_row(i):
        start = i * TOKEN_TILE
        return jnp.where(start < n_prompt, dec_batch, (start - n_prompt) // dec_seq)

    cond = jnp.zeros((COND_ROWS, d), F32).at[:dec_batch].set(c).at[dec_batch].set(c_ctx)
    mod = _ada(cond, w_ada, b_ada).reshape(depth, COND_ROWS, N_MOD, d)

    hsum = jnp.asarray(np.kron(np.eye(heads), np.ones((HEAD_DIM, HEAD_DIM))), BF16)
    ctx_k = cache_k.reshape(dec_batch, depth, cache_k.shape[2], na_w).astype(BF16)
    ctx_v = cache_v.reshape(dec_batch, depth, cache_v.shape[2], na_w).astype(BF16)

    x = jnp.concatenate([x_prompt.reshape(n_prompt, d), x_sample.reshape(-1, d)], axis=0)
    new_k, new_v = [], []
    for l in range(depth):
        pu, q, k, v, kf, vf = _premix(
            x, mod, l, norm1_g[l][None], w_in[l].astype(BF16),
            jnp.tile(q_norm_g[l], heads)[None], jnp.tile(k_norm_g[l], heads)[None], hsum,
            n_prompt, cond_row)
        new_k.append(kf.reshape(batch, seq, heads, HEAD_DIM))
        new_v.append(vf.reshape(batch, seq, heads, HEAD_DIM))
        attn = _attention(q, k, v, ctx_k, ctx_v, _relative_bias_table(rpb[l]), l,
                          n_prompt, seq, dec_seq)
        mix_args = (mod, l, w_pool[l].astype(BF16), pool_scale[l][None], w_out[l].astype(BF16),
                    norm2_g[l][None])
        li = l // 2
        if l % 2 == 0:
            d_ff = ffn_w_gate.shape[2]
            n_chunks = d_ff // FF_CHUNK
            wg = ffn_w_gate[li].astype(BF16).reshape(d, n_chunks, FF_CHUNK).transpose(1, 0, 2)
            wu = ffn_w_up[li].astype(BF16).reshape(d, n_chunks, FF_CHUNK).transpose(1, 0, 2)
            wd = ffn_w_down[li].astype(BF16).reshape(n_chunks, FF_CHUNK, d)
            x = _postmix_ffn(x, pu, attn, *mix_args, wg, wu, wd, seq_info, cond_row)
        else:
            router = jnp.zeros((d, ROUTER_LANES), F32).at[:, :n_experts].set(moe_router[li])
            r_hi = router.astype(BF16)
            r_lo = (router - r_hi.astype(F32)).astype(BF16)
            x2, h_rows, rout, g0, g1 = _postmix_router(
                x, pu, attn, *mix_args, r_hi, r_lo, n_experts, seq_info, cond_row)
            expert_ids = rout[:, :TOP_K].astype(jnp.int32)
            n_tiles_max = (TOP_K * n + n_experts * (MOE_ROW_TILE - 1)) // MOE_ROW_TILE
            row_flat, tile_expert, n_active, n_dump_tiles = _dispatch_plan(
                expert_ids, n_experts, n_tiles_max)
            y_rows = _moe(tile_expert, n_active, row_flat, h_rows,
                          moe_w_gate[li].astype(BF16), moe_w_up[li].astype(BF16),
                          moe_w_down[li].astype(BF16), d, TOP_K * n, n_dump_tiles)
            x = _combine(y_rows, x2, g0, g1, mod, l, cond_row)

    y_prompt = x[:n_prompt].reshape(batch, seq, d)
    y_sample = x[n_prompt:].reshape(dec_batch, dec_seq, d)
    return (y_prompt, y_sample, jnp.stack(new_k, axis=1), jnp.stack(new_v, axis=1))
```

```python
import functools

import numpy as np
import jax
import jax.numpy as jnp
from jax import lax
from jax.experimental import pallas as pl
from jax.experimental.pallas import tpu as pltpu

F32 = jnp.float32
BF16 = jnp.bfloat16

GRID_W = 64
POOL_WINDOWS = (2, 4, 8, 16)
HEAD_DIM = 64
NA_WIN_R = 8
NA_WIN_C = 16
N_MOD = 6
TOP_K = 2
NORM_EPS = 1e-6
LOG2_E = 1.4426950408889634

LANES = 128
SUBLANES = 8
VMEM_LIMIT_BYTES = 56 * 1024 * 1024

TOKEN_TILE = 512
POOL_HALO = 8
FF_CHUNK = 256
MOE_ROW_TILE = 1008
MOE_F_CHUNK = 512
ADA_COL_TILE = 1024
COND_ROWS = 16
ROUTER_LANES = 128


def _params(n_axes=1):
    return pltpu.CompilerParams(
        dimension_semantics=("arbitrary",) * n_axes,
        vmem_limit_bytes=VMEM_LIMIT_BYTES,
    )


def _resident(shape, index_map):
    return pl.BlockSpec(shape, index_map, pipeline_mode=pl.Buffered(1))


def _x_specs(x, n_prompt_tiles):
    if not isinstance(x, tuple):
        return [pl.BlockSpec((TOKEN_TILE, x.shape[1]), lambda i: (i, 0))]
    d = x[0].shape[1]
    return [pl.BlockSpec((TOKEN_TILE, d), lambda i: (jnp.minimum(i, n_prompt_tiles - 1), 0)),
            pl.BlockSpec((TOKEN_TILE, d), lambda i: (jnp.maximum(i - n_prompt_tiles, 0), 0))]


def _load_x(i, x_refs, n_prompt_tiles):
    if len(x_refs) == 1:
        return x_refs[0][...]
    return jnp.where(i < n_prompt_tiles, x_refs[0][...], x_refs[1][...])


def _as_tuple(x):
    return x if isinstance(x, tuple) else (x,)


def _ada_kernel(c_ref, w_ref, b_ref, o_ref):
    c = c_ref[...]
    s = c * jax.nn.sigmoid(c)
    o_ref[...] = jnp.dot(s.astype(BF16), w_ref[...].astype(BF16),
                         preferred_element_type=F32) + b_ref[...]


def _ada(cond, w_ada, b_ada):
    depth, d, width = w_ada.shape
    return pl.pallas_call(
        _ada_kernel,
        grid=(depth, width // ADA_COL_TILE),
        in_specs=[
            pl.BlockSpec((COND_ROWS, d), lambda l, j: (0, 0)),
            pl.BlockSpec((None, d, ADA_COL_TILE), lambda l, j: (l, 0, j)),
            pl.BlockSpec((None, 1, ADA_COL_TILE), lambda l, j: (l, 0, j)),
        ],
        out_specs=pl.BlockSpec((None, COND_ROWS, ADA_COL_TILE), lambda l, j: (l, 0, j)),
        out_shape=jax.ShapeDtypeStruct((depth, COND_ROWS, width), F32),
        compiler_params=_params(2),
        name="ada",
    )(cond, w_ada, b_ada.reshape(depth, 1, width))


def _premix_kernel(*refs, n_x, n_prompt_tiles, pool_w, na_w):
    (mod_ref, g_ref, w_ref, qg_ref, kg_ref, hsum_ref,
     pu_ref, q_ref, k_ref, v_ref, kf_ref, vf_ref) = refs[n_x:]
    i = pl.program_id(0)
    x = _load_x(i, refs[:n_x], n_prompt_tiles)
    inv = lax.rsqrt(jnp.mean(x * x, axis=-1, keepdims=True) + NORM_EPS)
    mod = mod_ref[...]
    h = (x * inv * g_ref[...]) * (1.0 + mod[1:2]) + mod[0:1]
    u = jnp.dot(h.astype(BF16), w_ref[...], preferred_element_type=F32)
    pu_ref[...] = u[:, :pool_w]
    q = u[:, pool_w:pool_w + na_w]
    k = u[:, pool_w + na_w:pool_w + 2 * na_w]
    v = u[:, pool_w + 2 * na_w:]

    def head_norm(t, g):
        ms = jnp.dot((t * t).astype(BF16), hsum_ref[...],
                     preferred_element_type=F32) * (1.0 / HEAD_DIM)
        return t * lax.rsqrt(ms + NORM_EPS) * g

    qn = head_norm(q, qg_ref[...])
    kn = head_norm(k, kg_ref[...])
    q_ref[...] = (qn * (HEAD_DIM ** -0.5 * LOG2_E)).astype(BF16)
    k_ref[...] = kn.astype(BF16)
    v_ref[...] = v.astype(BF16)

    @pl.when(i < n_prompt_tiles)
    def _():
        kf_ref[...] = kn
        vf_ref[...] = v


def _premix(x, mod, l, norm_g, w_in, q_g, k_g, hsum, n_prompt, cond_row):
    xs = _as_tuple(x)
    n, d = sum(a.shape[0] for a in xs), xs[0].shape[1]
    pool_w = d // 2
    na_w = d - pool_w
    n_tiles = n // TOKEN_TILE
    n_prompt_tiles = n_prompt // TOKEN_TILE
    last_p = n_prompt_tiles - 1
    tok = lambda w: pl.BlockSpec((TOKEN_TILE, w), lambda i: (i, 0))
    ptok = pl.BlockSpec((TOKEN_TILE, na_w), lambda i: (jnp.minimum(i, last_p), 0))
    kernel = functools.partial(_premix_kernel, n_x=len(xs), n_prompt_tiles=n_prompt_tiles,
                               pool_w=pool_w, na_w=na_w)
    return pl.pallas_call(
        kernel,
        grid=(n_tiles,),
        in_specs=_x_specs(x, n_prompt_tiles) + [
            pl.BlockSpec((None, None, N_MOD, d), lambda i: (l, cond_row(i), 0, 0)),
            pl.BlockSpec((1, d), lambda i: (0, 0)),
            _resident((d, w_in.shape[1]), lambda i: (0, 0)),
            pl.BlockSpec((1, na_w), lambda i: (0, 0)),
            pl.BlockSpec((1, na_w), lambda i: (0, 0)),
            _resident((na_w, na_w), lambda i: (0, 0)),
        ],
        out_specs=[tok(pool_w), tok(na_w), tok(na_w), tok(na_w), ptok, ptok],
        out_shape=[
            jax.ShapeDtypeStruct((n, pool_w), F32),
            jax.ShapeDtypeStruct((n, na_w), BF16),
            jax.ShapeDtypeStruct((n, na_w), BF16),
            jax.ShapeDtypeStruct((n, na_w), BF16),
            jax.ShapeDtypeStruct((n_prompt, na_w), F32),
            jax.ShapeDtypeStruct((n_prompt, na_w), F32),
        ],
        compiler_params=_params(),
        name=f"premix{l}",
    )(*xs, mod, norm_g, w_in, q_g, k_g, hsum)


_NT = (((1,), (1,)), ((), ()))
_TN = (((0,), (0,)), ((), ()))


def _block_diag_queries(q2):
    lo = lax.broadcasted_iota(jnp.int32, q2.shape, 1) < HEAD_DIM
    zero = jnp.zeros_like(q2)
    return jnp.concatenate([jnp.where(lo, q2, zero), jnp.where(lo, zero, q2)], axis=0)


def _pick_head_blocks(o, nq):
    lo = lax.broadcasted_iota(jnp.int32, (nq, LANES), 1) < HEAD_DIM
    return jnp.where(lo, o[:nq], o[nq:])


def _pair_attention(q2, k, v):
    nq = q2.shape[0]
    s = lax.dot_general(k, _block_diag_queries(q2), _NT, preferred_element_type=F32)
    p = jnp.exp2(s - jnp.max(s, axis=0, keepdims=True))
    r = 1.0 / jnp.sum(p, axis=0, keepdims=True)
    o = lax.dot_general((p * r).astype(BF16), v, _TN, preferred_element_type=F32)
    return _pick_head_blocks(o, nq)


def _attn_kernel(q_ref, k_ref, v_ref, ck_ref, cv_ref, bias_ref, o_ref, s_ref, p_ref, *,
                 n_prompt_tiles, seq, dec_seq, n_pairs):
    i = pl.program_id(0)
    tiles_per_seq = dec_seq // TOKEN_TILE
    rows_per_tile = TOKEN_TILE // GRID_W
    rows = dec_seq // GRID_W
    win_keys = NA_WIN_R * GRID_W
    block_off = (i * TOKEN_TILE) % dec_seq

    @pl.when(i < n_prompt_tiles)
    def _():
        for s in range(TOKEN_TILE // seq):
            start = pl.multiple_of(block_off + s * seq, seq)
            for hp in range(n_pairs):
                lanes = slice(hp * LANES, (hp + 1) * LANES)
                out = _pair_attention(q_ref[s * seq:(s + 1) * seq, lanes],
                                      k_ref[pl.ds(start, seq), lanes],
                                      v_ref[pl.ds(start, seq), lanes])
                o_ref[s * seq:(s + 1) * seq, lanes] = out.astype(o_ref.dtype)

    @pl.when(i >= n_prompt_tiles)
    def _():
        row0 = ((i - n_prompt_tiles) % tiles_per_seq) * rows_per_tile

        def indices(rl):
            r = row0 + rl
            r0 = jnp.clip(r - NA_WIN_R // 2, 0, rows - NA_WIN_R)
            return r0 - r + NA_WIN_R - 1, pl.multiple_of(r0 * GRID_W, GRID_W)

        def scores(rl):
            d0, kstart = indices(rl)
            for hp in range(n_pairs):
                lanes = slice(hp * LANES, (hp + 1) * LANES)
                qbd = _block_diag_queries(q_ref[rl * GRID_W:(rl + 1) * GRID_W, lanes])
                bias = bias_ref[hp, pl.ds(d0, NA_WIN_R)].reshape(win_keys, LANES)
                s_ref[rl % 2, hp, 0:win_keys] = lax.dot_general(
                    k_ref[pl.ds(kstart, win_keys), lanes], qbd, _NT,
                    preferred_element_type=F32) + bias
                s_ref[rl % 2, hp, win_keys:] = lax.dot_general(
                    ck_ref[:, lanes], qbd, _NT, preferred_element_type=F32)

        def softmax(rl):
            for hp in range(n_pairs):
                s = s_ref[rl % 2, hp]
                p = jnp.exp2(s - jnp.max(s, axis=0, keepdims=True))
                rr = 1.0 / jnp.sum(p, axis=0, keepdims=True)
                p_ref[rl % 2, hp] = (p * rr).astype(BF16)

        def values(rl):
            _, kstart = indices(rl)
            for hp in range(n_pairs):
                lanes = slice(hp * LANES, (hp + 1) * LANES)
                o = (lax.dot_general(p_ref[rl % 2, hp, 0:win_keys],
                                     v_ref[pl.ds(kstart, win_keys), lanes],
                                     _TN, preferred_element_type=F32)
                     + lax.dot_general(p_ref[rl % 2, hp, win_keys:], cv_ref[:, lanes], _TN,
                                       preferred_element_type=F32))
                o_ref[rl * GRID_W:(rl + 1) * GRID_W, lanes] = (
                    _pick_head_blocks(o, GRID_W).astype(o_ref.dtype))

        for step in range(rows_per_tile + 2):
            if step < rows_per_tile:
                scores(step)
            if 1 <= step <= rows_per_tile:
                softmax(step - 1)
            if step >= 2:
                values(step - 2)


def _attention(q, k, v, ctx_k, ctx_v, bias, l, n_prompt, seq, dec_seq):
    n, na_w = q.shape
    n_tiles = n // TOKEN_TILE
    n_prompt_tiles = n_prompt // TOKEN_TILE
    tiles_per_seq = dec_seq // TOKEN_TILE
    past = ctx_k.shape[2]
    n_pairs = na_w // LANES
    n_keys = NA_WIN_R * GRID_W + past
    kv_spec = pl.BlockSpec((dec_seq, na_w), lambda i: (i * TOKEN_TILE // dec_seq, 0))
    ctx_spec = pl.BlockSpec(
        (None, None, past, na_w),
        lambda i: (jnp.maximum(i - n_prompt_tiles, 0) // tiles_per_seq, l, 0, 0))
    kernel = functools.partial(_attn_kernel, n_prompt_tiles=n_prompt_tiles, seq=seq,
                               dec_seq=dec_seq, n_pairs=n_pairs)
    return pl.pallas_call(
        kernel,
        grid=(n_tiles,),
        in_specs=[
            pl.BlockSpec((TOKEN_TILE, na_w), lambda i: (i, 0)),
            kv_spec, kv_spec, ctx_spec, ctx_spec,
            _resident(bias.shape, lambda i: (0, 0, 0, 0)),
        ],
        out_specs=pl.BlockSpec((TOKEN_TILE, na_w), lambda i: (i, 0)),
        out_shape=jax.ShapeDtypeStruct((n, na_w), BF16),
        scratch_shapes=[pltpu.VMEM((2, n_pairs, n_keys, LANES), F32),
                        pltpu.VMEM((2, n_pairs, n_keys, LANES), BF16)],
        compiler_params=_params(),
        name=f"attn{l}",
    )(q, k, v, ctx_k, ctx_v, bias)


def _relative_bias_table(rpb_l):
    heads = rpb_l.shape[0]
    kc = np.arange(GRID_W)[:, None]
    qc = np.arange(GRID_W)[None, :]
    q_start = np.clip(qc - NA_WIN_C // 2, 0, GRID_W - NA_WIN_C)
    valid = (kc >= q_start) & (kc < q_start + NA_WIN_C)
    dc_idx = np.clip(kc - qc, -(NA_WIN_C - 1), NA_WIN_C - 1) + NA_WIN_C - 1
    t = rpb_l[:, :, dc_idx]
    t = jnp.where(valid[None, None], t.astype(F32) * LOG2_E, -jnp.inf)
    t = t.reshape(heads // 2, 2, 2 * NA_WIN_R - 1, GRID_W, GRID_W)
    return t.transpose(0, 2, 3, 1, 4).reshape(heads // 2, 2 * NA_WIN_R - 1, GRID_W, 2 * GRID_W)


def _mod_static(t, m):
    return t & (m - 1) if m & (m - 1) == 0 else lax.rem(t, m)


def _pool_mix(i, b, pu_ref, pp_ref, pn_ref, ext_ref, wp_ref, ps_ref, *, n_prompt_tiles, seq, dec_seq):
    sub = seq
    h = POOL_HALO
    is_prompt = i < n_prompt_tiles
    seq_len = jnp.where(is_prompt, seq, dec_seq)
    lo, hi = b * sub, (b + 1) * sub
    tok0 = i * TOKEN_TILE + lo
    pos0 = jnp.where(is_prompt, _mod_static(tok0, seq), _mod_static(tok0, dec_seq))
    prev = pp_ref[...] if lo == 0 else pu_ref[lo - h:lo, :]
    nxt = pn_ref[...] if hi == TOKEN_TILE else pu_ref[hi:hi + h, :]
    ext_ref[b, 0:h] = jnp.where(pos0 != 0, prev, 0.0)
    ext_ref[b, h:h + sub] = pu_ref[lo:hi, :]
    ext_ref[b, h + sub:] = jnp.where(pos0 + sub != seq_len, nxt, 0.0)
    left = pos0 + lax.broadcasted_iota(jnp.int32, (sub, LANES), 0)
    right = seq_len - left
    outs = []
    for g, window in enumerate(POOL_WINDOWS):
        half = window // 2
        lanes = slice(g * LANES, (g + 1) * LANES)
        total = ext_ref[b, h - half:h - half + sub, lanes]
        for j in range(1 - half, half):
            total = total + ext_ref[b, h + j:h + j + sub, lanes]
        count = (jnp.minimum(left, half) + jnp.minimum(right, half)).astype(F32)
        pooled = total / count - pu_ref[lo:hi, lanes]
        outs.append(jnp.dot(pooled.astype(BF16), wp_ref[g], preferred_element_type=F32))
    return jnp.concatenate(outs, axis=-1) * ps_ref[...]


def _postmix_chain(i, x_refs, pu_ref, pp_ref, pn_ref, at_ref, mod_ref, wp_ref, ps_ref, wo_ref,
                   g2_ref, ext_ref, tail, **seq_info):
    sub = seq_info["seq"]
    n_sub = TOKEN_TILE // sub
    mod = mod_ref[...]
    is_prompt = i < seq_info["n_prompt_tiles"]

    def rows(b):
        return slice(b * sub, (b + 1) * sub)

    def pool_stage(b, _):
        return _pool_mix(i, b, pu_ref, pp_ref, pn_ref, ext_ref, wp_ref, ps_ref, **seq_info)

    def mix_stage(b, pool_out):
        mixed = jnp.concatenate([pool_out.astype(BF16), at_ref[rows(b), :]], axis=-1)
        mix = jnp.dot(mixed, wo_ref[...], preferred_element_type=F32)
        if len(x_refs) == 1:
            x = x_refs[0][rows(b), :]
        else:
            x = jnp.where(is_prompt, x_refs[0][rows(b), :], x_refs[1][rows(b), :])
        return x + mod[2:3] * mix

    def norm_stage(b, x2):
        inv = lax.rsqrt(jnp.mean(x2 * x2, axis=-1, keepdims=True) + NORM_EPS)
        return x2, (x2 * inv * g2_ref[...]) * (1.0 + mod[4:5]) + mod[3:4]

    def tail_stage(b, x2_h2):
        tail(rows(b), *x2_h2)

    stages = (pool_stage, mix_stage, norm_stage, tail_stage)
    state = [None] * n_sub
    for step in range(n_sub + len(stages) - 1):
        for b in range(n_sub):
            if 0 <= step - b < len(stages):
                state[b] = stages[step - b](b, state[b])
    return mod[5:6]


def _postmix_ffn_kernel(*refs, n_x, **seq_info):
    (pu_ref, pp_ref, pn_ref, at_ref, mod_ref, wp_ref, ps_ref, wo_ref,
     g2_ref, wg_ref, wu_ref, wd_ref, y_ref, ext_ref, h_ref, acc_ref) = refs[n_x:]
    i = pl.program_id(0)

    def tail(rows, x2, h2):
        y_ref[rows, :] = x2
        h_ref[rows, :] = h2.astype(BF16)

    gate2 = _postmix_chain(i, refs[:n_x], pu_ref, pp_ref, pn_ref, at_ref, mod_ref, wp_ref,
                           ps_ref, wo_ref, g2_ref, ext_ref, tail, **seq_info)
    acc_ref[...] = jnp.zeros_like(acc_ref)

    def chunk(c, carry):
        h = h_ref[...]
        a = jnp.dot(h, wg_ref[c], preferred_element_type=F32)
        b = jnp.dot(h, wu_ref[c], preferred_element_type=F32)
        act = (a * jax.nn.sigmoid(a) * b).astype(BF16)
        acc_ref[...] += jnp.dot(act, wd_ref[c], preferred_element_type=F32)
        return carry

    lax.fori_loop(0, wg_ref.shape[0], chunk, 0, unroll=True)
    y_ref[...] = y_ref[...] + gate2 * acc_ref[...]


def _postmix_router_kernel(*refs, n_x, n_experts, **seq_info):
    (pu_ref, pp_ref, pn_ref, at_ref, mod_ref, wp_ref, ps_ref, wo_ref,
     g2_ref, rh_ref, rl_ref, x2_ref, hr_ref, rout_ref, g0_ref, g1_ref, ext_ref) = refs[n_x:]
    i = pl.program_id(0)

    def tail(rows, x2, h2):
        x2_ref[rows, :] = x2
        n_rows = rows.stop - rows.start
        for s in range(h2.shape[1] // LANES):
            hr_ref[pl.ds(rows.start * SUBLANES + s, n_rows, stride=SUBLANES), :] = (
                h2[:, s * LANES:(s + 1) * LANES])
        hi = h2.astype(BF16)
        lo = (h2 - hi.astype(F32)).astype(BF16)
        logits = (jnp.dot(hi, rh_ref[...], preferred_element_type=F32)
                  + jnp.dot(lo, rh_ref[...], preferred_element_type=F32)
                  + jnp.dot(hi, rl_ref[...], preferred_element_type=F32))
        lane = lax.broadcasted_iota(jnp.int32, logits.shape, 1).astype(F32)
        logits = jnp.where(lane < n_experts, logits, -jnp.inf)
        m1 = jnp.max(logits, axis=-1, keepdims=True)
        i1 = jnp.min(jnp.where(logits == m1, lane, float(ROUTER_LANES)), axis=-1, keepdims=True)
        rest = jnp.where(lane == i1, -jnp.inf, logits)
        m2 = jnp.max(rest, axis=-1, keepdims=True)
        i2 = jnp.min(jnp.where(rest == m2, lane, float(ROUTER_LANES)), axis=-1, keepdims=True)
        e = jnp.exp(m2 - m1)
        rout_ref[rows, :] = jnp.where(lane == 0, i1, jnp.where(lane == 1, i2, 0.0))
        g0_ref[rows, :] = jnp.broadcast_to(1.0 / (1.0 + e), (n_rows, LANES))
        g1_ref[rows, :] = jnp.broadcast_to(e / (1.0 + e), (n_rows, LANES))

    _postmix_chain(i, refs[:n_x], pu_ref, pp_ref, pn_ref, at_ref, mod_ref, wp_ref,
                   ps_ref, wo_ref, g2_ref, ext_ref, tail, **seq_info)


def _pool_scratch_shape(seq_info, pool_w):
    sub = seq_info["seq"]
    return (TOKEN_TILE // sub, sub + 2 * POOL_HALO, pool_w)


def _postmix_specs(x, n, d, pool_w, na_w, l, cond_row, n_prompt_tiles):
    n_halo_blocks = n // POOL_HALO
    per_tile = TOKEN_TILE // POOL_HALO
    tok = lambda w: pl.BlockSpec((TOKEN_TILE, w), lambda i: (i, 0))
    return _x_specs(x, n_prompt_tiles) + [
        tok(pool_w),
        pl.BlockSpec((POOL_HALO, pool_w), lambda i: (jnp.maximum(i * per_tile - 1, 0), 0)),
        pl.BlockSpec((POOL_HALO, pool_w),
                     lambda i: (jnp.minimum((i + 1) * per_tile, n_halo_blocks - 1), 0)),
        tok(na_w),
        pl.BlockSpec((None, None, N_MOD, d), lambda i: (l, cond_row(i), 0, 0)),
        _resident((len(POOL_WINDOWS), LANES, LANES), lambda i: (0, 0, 0)),
        pl.BlockSpec((1, pool_w), lambda i: (0, 0)),
        _resident((d, d), lambda i: (0, 0)),
        pl.BlockSpec((1, d), lambda i: (0, 0)),
    ]


def _postmix_ffn(x, pu, attn, mod, l, w_pool, pool_scale, w_out, norm_g, wg, wu, wd,
                 seq_info, cond_row):
    xs = _as_tuple(x)
    n, d = pu.shape[0], xs[0].shape[1]
    pool_w = pu.shape[1]
    n_chunks = wg.shape[0]
    kernel = functools.partial(_postmix_ffn_kernel, n_x=len(xs), **seq_info)
    return pl.pallas_call(
        kernel,
        grid=(n // TOKEN_TILE,),
        in_specs=_postmix_specs(x, n, d, pool_w, attn.shape[1], l, cond_row,
                                seq_info["n_prompt_tiles"]) + [
            _resident((n_chunks, d, FF_CHUNK), lambda i: (0, 0, 0)),
            _resident((n_chunks, d, FF_CHUNK), lambda i: (0, 0, 0)),
            _resident((n_chunks, FF_CHUNK, d), lambda i: (0, 0, 0)),
        ],
        out_specs=pl.BlockSpec((TOKEN_TILE, d), lambda i: (i, 0)),
        out_shape=jax.ShapeDtypeStruct((n, d), F32),
        scratch_shapes=[
            pltpu.VMEM(_pool_scratch_shape(seq_info, pool_w), F32),
            pltpu.VMEM((TOKEN_TILE, d), BF16),
            pltpu.VMEM((TOKEN_TILE, d), F32),
        ],
        compiler_params=_params(),
        name=f"postmix_ffn{l}",
    )(*xs, pu, pu, pu, attn, mod, w_pool, pool_scale, w_out, norm_g, wg, wu, wd)


def _postmix_router(x, pu, attn, mod, l, w_pool, pool_scale, w_out, norm_g, r_hi, r_lo,
                    n_experts, seq_info, cond_row):
    xs = _as_tuple(x)
    n, d = pu.shape[0], xs[0].shape[1]
    pool_w = pu.shape[1]
    kernel = functools.partial(_postmix_router_kernel, n_x=len(xs), n_experts=n_experts,
                               **seq_info)
    row_tile = TOKEN_TILE * d // LANES
    return pl.pallas_call(
        kernel,
        grid=(n // TOKEN_TILE,),
        in_specs=_postmix_specs(x, n, d, pool_w, attn.shape[1], l, cond_row,
                                seq_info["n_prompt_tiles"]) + [
            pl.BlockSpec((d, ROUTER_LANES), lambda i: (0, 0)),
            pl.BlockSpec((d, ROUTER_LANES), lambda i: (0, 0)),
        ],
        out_specs=[
            pl.BlockSpec((TOKEN_TILE, d), lambda i: (i, 0)),
            pl.BlockSpec((row_tile, LANES), lambda i: (i, 0)),
            pl.BlockSpec((TOKEN_TILE, ROUTER_LANES), lambda i: (i, 0)),
            pl.BlockSpec((TOKEN_TILE, LANES), lambda i: (i, 0)),
            pl.BlockSpec((TOKEN_TILE, LANES), lambda i: (i, 0)),
        ],
        out_shape=[
            jax.ShapeDtypeStruct((n, d), F32),
            jax.ShapeDtypeStruct((n * d // LANES, LANES), F32),
            jax.ShapeDtypeStruct((n, ROUTER_LANES), F32),
            jax.ShapeDtypeStruct((n, LANES), F32),
            jax.ShapeDtypeStruct((n, LANES), F32),
        ],
        scratch_shapes=[pltpu.VMEM(_pool_scratch_shape(seq_info, pool_w), F32)],
        compiler_params=_params(),
        name=f"postmix_router{l}",
    )(*xs, pu, pu, pu, attn, mod, w_pool, pool_scale, w_out, norm_g, r_hi, r_lo)


def _moe_kernel(te_ref, na_ref, rf_ref, h_hbm, wg_ref, wu_ref, wd_ref, y_hbm,
                xbuf, ybuf, xb16, acc_ref, gsem, ssem, zsem, *,
                cpr, rows_per_step, n_flat, n_dump_tiles):
    j = pl.program_id(0)
    f = pl.program_id(1)
    n_f = pl.num_programs(1)
    tme = MOE_ROW_TILE
    tile_rows = tme * cpr
    n_active = na_ref[0]
    last_token = n_flat // TOP_K - 1

    def gather_row(tile, slot, row):
        v = rf_ref[(tile + 1) * tme + row]
        tok = jnp.minimum(lax.shift_right_logical(v, 1), last_token)
        return pltpu.make_async_copy(
            h_hbm.at[pl.ds(pl.multiple_of(tok * cpr, cpr), cpr)],
            xbuf.at[slot, pl.ds(pl.multiple_of(row * cpr, cpr), cpr)],
            gsem.at[slot])

    def scatter_row(tile, slot, row):
        v = rf_ref[(tile + 1) * tme + row]
        return pltpu.make_async_copy(
            ybuf.at[slot, pl.ds(pl.multiple_of(row * cpr, cpr), cpr)],
            y_hbm.at[pl.ds(pl.multiple_of(v * cpr, cpr), cpr)],
            ssem.at[slot])

    def wait_gather(slot):
        pltpu.make_async_copy(h_hbm.at[pl.ds(0, tile_rows)], xbuf.at[slot], gsem.at[slot]).wait()

    def wait_scatter(slot):
        pltpu.make_async_copy(ybuf.at[slot], y_hbm.at[pl.ds(0, tile_rows)], ssem.at[slot]).wait()

    def dump_fill(t):
        return pltpu.make_async_copy(
            ybuf.at[1], y_hbm.at[pl.ds((n_flat + t * tme) * cpr, tile_rows)], zsem)

    @pl.when((f == 0) & (j == 0))
    def _():
        ybuf[1] = jnp.zeros(ybuf.shape[1:], ybuf.dtype)
        for t in range(n_dump_tiles):
            dump_fill(t).start()
        for t in range(n_dump_tiles):
            dump_fill(t).wait()

        def body(r, carry):
            gather_row(0, 0, r).start()
            return carry

        lax.fori_loop(0, tme, body, 0)

    for par in range(2):
        @pl.when((j < n_active) & (j % 2 == par))
        def _(par=par):
            @pl.when(f == 0)
            def _():
                wait_gather(par)
                for s in range(cpr):
                    xb16[:, s * LANES:(s + 1) * LANES] = (
                        xbuf[par, pl.ds(s, tme, stride=cpr), :].astype(BF16))
                acc_ref[...] = jnp.zeros_like(acc_ref)

            for t in range(rows_per_step):
                gather_row(j + 1, 1 - par, t * n_f + f).start()
                scatter_row(j - 1, 1 - par, t * n_f + f).start()
            x = xb16[...]
            a = jnp.dot(x, wg_ref[...], preferred_element_type=F32)
            b = jnp.dot(x, wu_ref[...], preferred_element_type=F32)
            act = (a * jax.nn.sigmoid(a) * b).astype(BF16)
            acc_ref[...] += jnp.dot(act, wd_ref[...], preferred_element_type=F32)

            @pl.when(f == n_f - 1)
            def _():
                @pl.when(j >= 1)
                def _():
                    wait_scatter(par)

                for s in range(cpr):
                    ybuf[par, pl.ds(s, tme, stride=cpr), :] = acc_ref[:, s * LANES:(s + 1) * LANES]

        @pl.when((j == n_active) & (f == 0) & (j % 2 == par))
        def _(par=par):
            def body(r, carry):
                scatter_row(j - 1, 1 - par, r).start()
                return carry

            lax.fori_loop(0, tme, body, 0)
            wait_gather(par)
            wait_scatter(par)
            wait_scatter(1 - par)


def _moe(tile_expert, n_active, row_flat, h_rows, wg, wu, wd, d, n_flat, n_dump_tiles):
    n_steps = tile_expert.shape[0]
    d_exp = wg.shape[2]
    n_f = d_exp // MOE_F_CHUNK
    cpr = d // LANES
    tile_rows = MOE_ROW_TILE * cpr
    assert MOE_ROW_TILE % n_f == 0

    def f_idx(j, f, na):
        return jnp.where(j < na[0], f, n_f - 1)

    grid_spec = pltpu.PrefetchScalarGridSpec(
        num_scalar_prefetch=3,
        grid=(n_steps, n_f),
        in_specs=[
            pl.BlockSpec(memory_space=pl.ANY),
            pl.BlockSpec((None, d, MOE_F_CHUNK), lambda j, f, te, na, rf: (te[j], 0, f_idx(j, f, na))),
            pl.BlockSpec((None, d, MOE_F_CHUNK), lambda j, f, te, na, rf: (te[j], 0, f_idx(j, f, na))),
            pl.BlockSpec((None, MOE_F_CHUNK, d), lambda j, f, te, na, rf: (te[j], f_idx(j, f, na), 0)),
        ],
        out_specs=pl.BlockSpec(memory_space=pl.ANY),
        scratch_shapes=[
            pltpu.VMEM((2, tile_rows, LANES), F32),
            pltpu.VMEM((2, tile_rows, LANES), F32),
            pltpu.VMEM((MOE_ROW_TILE, d), BF16),
            pltpu.VMEM((MOE_ROW_TILE, d), F32),
            pltpu.SemaphoreType.DMA((2,)),
            pltpu.SemaphoreType.DMA((2,)),
            pltpu.SemaphoreType.DMA,
        ],
    )
    kernel = functools.partial(_moe_kernel, cpr=cpr, rows_per_step=MOE_ROW_TILE // n_f,
                               n_flat=n_flat, n_dump_tiles=n_dump_tiles)
    return pl.pallas_call(
        kernel,
        grid_spec=grid_spec,
        out_shape=jax.ShapeDtypeStruct(((n_flat + n_dump_tiles * MOE_ROW_TILE) * cpr, LANES), F32),
        compiler_params=_params(2),
        name="moe",
    )(tile_expert, n_active, row_flat, h_rows, wg, wu, wd)


def _combine_kernel(y_ref, x2_ref, g0_ref, g1_ref, mod_ref, *o_refs, cpr, n_prompt_tiles):
    i = pl.program_id(0)
    tm = x2_ref.shape[0]
    gate2 = mod_ref[...][5:6]
    stride = TOP_K * cpr

    def write(o_ref):
        g0 = g0_ref[...]
        g1 = g1_ref[...]
        for s in range(cpr):
            lanes = slice(s * LANES, (s + 1) * LANES)
            ya = y_ref[pl.ds(s, tm, stride=stride), :]
            yb = y_ref[pl.ds(cpr + s, tm, stride=stride), :]
            o_ref[:, lanes] = x2_ref[:, lanes] + gate2[:, lanes] * (g0 * ya + g1 * yb)

    if len(o_refs) == 1:
        write(o_refs[0])
    else:
        pl.when(i < n_prompt_tiles)(lambda: write(o_refs[0]))
        pl.when(i >= n_prompt_tiles)(lambda: write(o_refs[1]))


def _combine(y_rows, x2, g0, g1, mod, l, cond_row, n_prompt, split_output):
    n, d = x2.shape
    cpr = d // LANES
    n_prompt_tiles = n_prompt // TOKEN_TILE
    kernel = functools.partial(_combine_kernel, cpr=cpr, n_prompt_tiles=n_prompt_tiles)
    if split_output:
        out_specs = [
            pl.BlockSpec((TOKEN_TILE, d), lambda i: (jnp.minimum(i, n_prompt_tiles - 1), 0)),
            pl.BlockSpec((TOKEN_TILE, d), lambda i: (jnp.maximum(i - n_prompt_tiles, 0), 0))]
        out_shape = [jax.ShapeDtypeStruct((n_prompt, d), F32),
                     jax.ShapeDtypeStruct((n - n_prompt, d), F32)]
    else:
        out_specs = pl.BlockSpec((TOKEN_TILE, d), lambda i: (i, 0))
        out_shape = jax.ShapeDtypeStruct((n, d), F32)
    return pl.pallas_call(
        kernel,
        grid=(n // TOKEN_TILE,),
        in_specs=[
            pl.BlockSpec((TOKEN_TILE * TOP_K * cpr, LANES), lambda i: (i, 0)),
            pl.BlockSpec((TOKEN_TILE, d), lambda i: (i, 0)),
            pl.BlockSpec((TOKEN_TILE, LANES), lambda i: (i, 0)),
            pl.BlockSpec((TOKEN_TILE, LANES), lambda i: (i, 0)),
            pl.BlockSpec((None, None, N_MOD, d), lambda i: (l, cond_row(i), 0, 0)),
        ],
        out_specs=out_specs,
        out_shape=out_shape,
        compiler_params=_params(),
        name="combine",
    )(y_rows, x2, g0, g1, mod)


def _dispatch_plan(expert_ids, n_experts, n_tiles_max):
    tme = MOE_ROW_TILE
    n_flat = expert_ids.size
    e_flat = expert_ids.reshape(-1)
    onehot = (e_flat[:, None] == jnp.arange(n_experts)[None, :]).astype(jnp.int32)
    csum = jnp.cumsum(onehot, axis=0)
    rank = jnp.take_along_axis(csum, e_flat[:, None], axis=1)[:, 0] - 1
    counts = csum[-1]
    tiles = (counts + tme - 1) // tme
    tile_end = jnp.cumsum(tiles)
    row_off = (tile_end - tiles) * tme
    pos = tme + row_off[e_flat] + rank
    n_active = tile_end[-1]
    n_steps = n_tiles_max + 1
    tile_ids = jnp.minimum(jnp.arange(n_steps), n_active - 1)
    tile_expert = jnp.sum(tile_ids[:, None] >= tile_end[None, :], axis=1).astype(jnp.int32)
    n_dump_tiles = n_experts + 1
    rf = jnp.full(((n_steps + 1) * tme,), -1, jnp.int32).at[pos].set(
        jnp.arange(n_flat, dtype=jnp.int32), unique_indices=True)
    is_pad = rf < 0
    pad_slot = jnp.minimum(jnp.cumsum(is_pad.astype(jnp.int32)) - 1, n_dump_tiles * tme - 1)
    rf = jnp.where(is_pad, n_flat + pad_slot, rf)
    return rf, tile_expert, n_active.astype(jnp.int32).reshape(1), n_dump_tiles


def kernel(x_prompt, x_sample, cache_k, cache_v, c, c_ctx, norm1_g, norm2_g, w_ada, b_ada, w_in,
           q_norm_g, k_norm_g, w_pool, pool_scale, rpb, w_out, ffn_w_gate, ffn_w_up, ffn_w_down,
           moe_router, moe_w_gate, moe_w_up, moe_w_down):
    batch, seq, d = x_prompt.shape
    dec_batch, dec_seq, _ = x_sample.shape
    depth = w_in.shape[0]
    heads = cache_k.shape[3]
    na_w = heads * HEAD_DIM
    pool_w = d - na_w
    n_prompt = batch * seq
    n = n_prompt + dec_batch * dec_seq
    n_experts = moe_router.shape[2]
    assert pool_w == len(POOL_WINDOWS) * LANES and na_w % LANES == 0
    assert TOKEN_TILE % seq == 0 and n_prompt % dec_seq == 0 and dec_seq % TOKEN_TILE == 0
    assert dec_seq // GRID_W >= NA_WIN_R and dec_batch < COND_ROWS
    n_prompt_tiles = n_prompt // TOKEN_TILE
    seq_info = dict(n_prompt_tiles=n_prompt_tiles, seq=seq, dec_seq=dec_seq)

    def cond_row(i):
        start = i * TOKEN_TILE
        return jnp.where(start < n_prompt, dec_batch, (start - n_prompt) // dec_seq)

    cond = jnp.zeros((COND_ROWS, d), F32).at[:dec_batch].set(c).at[dec_batch].set(c_ctx)
    mod = _ada(cond, w_ada, b_ada).reshape(depth, COND_ROWS, N_MOD, d)

    hsum = jnp.asarray(np.kron(np.eye(heads), np.ones((HEAD_DIM, HEAD_DIM))), BF16)
    ctx_k = cache_k.reshape(dec_batch, depth, cache_k.shape[2], na_w).astype(BF16)
    ctx_v = cache_v.reshape(dec_batch, depth, cache_v.shape[2], na_w).astype(BF16)

    x = (x_prompt.reshape(n_prompt, d), x_sample.reshape(-1, d))
    new_k, new_v = [], []
    for l in range(depth):
        pu, q, k, v, kf, vf = _premix(
            x, mod, l, norm1_g[l][None], w_in[l].astype(BF16),
            jnp.tile(q_norm_g[l], heads)[None], jnp.tile(k_norm_g[l], heads)[None], hsum,
            n_prompt, cond_row)
        new_k.append(kf.reshape(batch, seq, heads, HEAD_DIM))
        new_v.append(vf.reshape(batch, seq, heads, HEAD_DIM))
        attn = _attention(q, k, v, ctx_k, ctx_v, _relative_bias_table(rpb[l]), l,
                          n_prompt, seq, dec_seq)
        mix_args = (mod, l, w_pool[l].astype(BF16), pool_scale[l][None], w_out[l].astype(BF16),
                    norm2_g[l][None])
        li = l // 2
        if l % 2 == 0:
            d_ff = ffn_w_gate.shape[2]
            n_chunks = d_ff // FF_CHUNK
            wg = ffn_w_gate[li].astype(BF16).reshape(d, n_chunks, FF_CHUNK).transpose(1, 0, 2)
            wu = ffn_w_up[li].astype(BF16).reshape(d, n_chunks, FF_CHUNK).transpose(1, 0, 2)
            wd = ffn_w_down[li].astype(BF16).reshape(n_chunks, FF_CHUNK, d)
            x = _postmix_ffn(x, pu, attn, *mix_args, wg, wu, wd, seq_info, cond_row)
        else:
            router = jnp.zeros((d, ROUTER_LANES), F32).at[:, :n_experts].set(moe_router[li])
            r_hi = router.astype(BF16)
            r_lo = (router - r_hi.astype(F32)).astype(BF16)
            x2, h_rows, rout, g0, g1 = _postmix_router(
                x, pu, attn, *mix_args, r_hi, r_lo, n_experts, seq_info, cond_row)
            expert_ids = rout[:, :TOP_K].astype(jnp.int32)
            n_tiles_max = (TOP_K * n + n_experts * (MOE_ROW_TILE - 1)) // MOE_ROW_TILE
            row_flat, tile_expert, n_active, n_dump_tiles = _dispatch_plan(
                expert_ids, n_experts, n_tiles_max)
            y_rows = _moe(tile_expert, n_active, row_flat, h_rows,
                          moe_w_gate[li].astype(BF16), moe_w_up[li].astype(BF16),
                          moe_w_down[li].astype(BF16), d, TOP_K * n, n_dump_tiles)
            x = _combine(y_rows, x2, g0, g1, mod, l, cond_row, n_prompt,
                         split_output=(l == depth - 1))

    if not isinstance(x, tuple):
        x = (x[:n_prompt], x[n_prompt:])
    y_prompt = x[0].reshape(batch, seq, d)
    y_sample = x[1].reshape(dec_batch, dec_seq, d)
    return (y_prompt, y_sample, jnp.stack(new_k, axis=1), jnp.stack(new_v, axis=1))
```

```python
import functools

import numpy as np
import jax
import jax.numpy as jnp
from jax import lax
from jax.experimental import pallas as pl
from jax.experimental.pallas import tpu as pltpu

F32 = jnp.float32
BF16 = jnp.bfloat16

GRID_W = 64
POOL_WINDOWS = (2, 4, 8, 16)
HEAD_DIM = 64
NA_WIN_R = 8
NA_WIN_C = 16
N_MOD = 6
TOP_K = 2
NORM_EPS = 1e-6
LOG2_E = 1.4426950408889634

LANES = 128
SUBLANES = 8
VMEM_LIMIT_BYTES = 56 * 1024 * 1024

TOKEN_TILE = 512
POOL_HALO = 8
FF_CHUNK = 256
MOE_ROW_TILE = 1008
MOE_F_CHUNK = 512
ADA_COL_TILE = 1024
COND_ROWS = 16
ROUTER_LANES = 128


def _params(n_axes=1):
    return pltpu.CompilerParams(
        dimension_semantics=("arbitrary",) * n_axes,
        vmem_limit_bytes=VMEM_LIMIT_BYTES,
    )


def _resident(shape, index_map):
    return pl.BlockSpec(shape, index_map, pipeline_mode=pl.Buffered(1))


def _x_specs(x, n_prompt_tiles):
    if not isinstance(x, tuple):
        return [pl.BlockSpec((TOKEN_TILE, x.shape[1]), lambda i: (i, 0))]
    d = x[0].shape[1]
    return [pl.BlockSpec((TOKEN_TILE, d), lambda i: (jnp.minimum(i, n_prompt_tiles - 1), 0)),
            pl.BlockSpec((TOKEN_TILE, d), lambda i: (jnp.maximum(i - n_prompt_tiles, 0), 0))]


def _load_x(i, x_refs, n_prompt_tiles):
    if len(x_refs) == 1:
        return x_refs[0][...]
    return jnp.where(i < n_prompt_tiles, x_refs[0][...], x_refs[1][...])


def _as_tuple(x):
    return x if isinstance(x, tuple) else (x,)


def _ada_kernel(c_ref, w_ref, b_ref, o_ref):
    c = c_ref[...]
    s = c * jax.nn.sigmoid(c)
    o_ref[...] = jnp.dot(s.astype(BF16), w_ref[...].astype(BF16),
                         preferred_element_type=F32) + b_ref[...]


def _ada(cond, w_ada, b_ada):
    depth, d, width = w_ada.shape
    return pl.pallas_call(
        _ada_kernel,
        grid=(depth, width // ADA_COL_TILE),
        in_specs=[
            pl.BlockSpec((COND_ROWS, d), lambda l, j: (0, 0)),
            pl.BlockSpec((None, d, ADA_COL_TILE), lambda l, j: (l, 0, j)),
            pl.BlockSpec((None, 1, ADA_COL_TILE), lambda l, j: (l, 0, j)),
        ],
        out_specs=pl.BlockSpec((None, COND_ROWS, ADA_COL_TILE), lambda l, j: (l, 0, j)),
        out_shape=jax.ShapeDtypeStruct((depth, COND_ROWS, width), F32),
        compiler_params=_params(2),
        name="ada",
    )(cond, w_ada, b_ada.reshape(depth, 1, width))


def _premix_kernel(*refs, n_x, n_prompt_tiles, pool_w, na_w, seq):
    (mod_ref, g_ref, w_ref, qg_ref, kg_ref, hsum_ref, _, _,
     pu_ref, q_ref, k_ref, v_ref, kf_ref, vf_ref) = refs[n_x:]
    i = pl.program_id(0)
    x = _load_x(i, refs[:n_x], n_prompt_tiles)
    inv = lax.rsqrt(jnp.mean(x * x, axis=-1, keepdims=True) + NORM_EPS)
    mod = mod_ref[...]
    h = (x * inv * g_ref[...]) * (1.0 + mod[1:2]) + mod[0:1]
    u = jnp.dot(h.astype(BF16), w_ref[...], preferred_element_type=F32)
    pu_ref[...] = u[:, :pool_w]
    q = u[:, pool_w:pool_w + na_w]
    k = u[:, pool_w + na_w:pool_w + 2 * na_w]
    v = u[:, pool_w + 2 * na_w:]

    def head_norm(t, g):
        ms = jnp.dot((t * t).astype(BF16), hsum_ref[...],
                     preferred_element_type=F32) * (1.0 / HEAD_DIM)
        return t * lax.rsqrt(ms + NORM_EPS) * g

    qn = head_norm(q, qg_ref[...])
    kn = head_norm(k, kg_ref[...])
    q_ref[...] = (qn * (HEAD_DIM ** -0.5 * LOG2_E)).astype(BF16)
    k_ref[...] = kn.astype(BF16)
    v_ref[...] = v.astype(BF16)

    @pl.when(i < n_prompt_tiles)
    def _():
        heads = na_w // HEAD_DIM
        for b in range(TOKEN_TILE // seq):
            for hd in range(heads):
                rows, cols = slice(b * seq, (b + 1) * seq), slice(hd * HEAD_DIM, (hd + 1) * HEAD_DIM)
                kf_ref[b, pl.ds(hd, seq, stride=heads), :] = kn[rows, cols]
                vf_ref[b, pl.ds(hd, seq, stride=heads), :] = v[rows, cols]


def _premix(x, mod, l, norm_g, w_in, q_g, k_g, hsum, n_prompt, cond_row, cache_kv, seq):
    xs = _as_tuple(x)
    n, d = sum(a.shape[0] for a in xs), xs[0].shape[1]
    pool_w = d // 2
    na_w = d - pool_w
    n_tiles = n // TOKEN_TILE
    n_prompt_tiles = n_prompt // TOKEN_TILE
    last_p = n_prompt_tiles - 1
    tok = lambda w: pl.BlockSpec((TOKEN_TILE, w), lambda i: (i, 0))
    cache_spec = pl.BlockSpec((TOKEN_TILE // seq, None) + cache_kv[0].shape[2:],
                              lambda i: (jnp.minimum(i, last_p), l, 0, 0))
    any_spec = pl.BlockSpec(memory_space=pl.ANY)
    n_in = len(xs) + 8
    kernel = functools.partial(_premix_kernel, n_x=len(xs), n_prompt_tiles=n_prompt_tiles,
                               pool_w=pool_w, na_w=na_w, seq=seq)
    return pl.pallas_call(
        kernel,
        grid=(n_tiles,),
        in_specs=_x_specs(x, n_prompt_tiles) + [
            pl.BlockSpec((None, None, N_MOD, d), lambda i: (l, cond_row(i), 0, 0)),
            pl.BlockSpec((1, d), lambda i: (0, 0)),
            _resident((d, w_in.shape[1]), lambda i: (0, 0)),
            pl.BlockSpec((1, na_w), lambda i: (0, 0)),
            pl.BlockSpec((1, na_w), lambda i: (0, 0)),
            _resident((na_w, na_w), lambda i: (0, 0)),
            any_spec, any_spec,
        ],
        out_specs=[tok(pool_w), tok(na_w), tok(na_w), tok(na_w), cache_spec, cache_spec],
        out_shape=[
            jax.ShapeDtypeStruct((n, pool_w), F32),
            jax.ShapeDtypeStruct((n, na_w), BF16),
            jax.ShapeDtypeStruct((n, na_w), BF16),
            jax.ShapeDtypeStruct((n, na_w), BF16),
            jax.ShapeDtypeStruct(cache_kv[0].shape, F32),
            jax.ShapeDtypeStruct(cache_kv[1].shape, F32),
        ],
        input_output_aliases={n_in - 2: 4, n_in - 1: 5},
        compiler_params=_params(),
        name=f"premix{l}",
    )(*xs, mod, norm_g, w_in, q_g, k_g, hsum, *cache_kv)


_NT = (((1,), (1,)), ((), ()))
_TN = (((0,), (0,)), ((), ()))


def _block_diag_queries(q2):
    lo = lax.broadcasted_iota(jnp.int32, q2.shape, 1) < HEAD_DIM
    zero = jnp.zeros_like(q2)
    return jnp.concatenate([jnp.where(lo, q2, zero), jnp.where(lo, zero, q2)], axis=0)


def _pick_head_blocks(o, nq):
    lo = lax.broadcasted_iota(jnp.int32, (nq, LANES), 1) < HEAD_DIM
    return jnp.where(lo, o[:nq], o[nq:])


def _pair_attention(q2, k, v):
    nq = q2.shape[0]
    s = lax.dot_general(k, _block_diag_queries(q2), _NT, preferred_element_type=F32)
    p = jnp.exp2(s - jnp.max(s, axis=0, keepdims=True))
    r = 1.0 / jnp.sum(p, axis=0, keepdims=True)
    o = lax.dot_general((p * r).astype(BF16), v, _TN, preferred_element_type=F32)
    return _pick_head_blocks(o, nq)


def _attn_kernel(q_ref, k_ref, v_ref, ck_ref, cv_ref, bias_ref, o_ref, s_ref, p_ref, *,
                 n_prompt_tiles, seq, dec_seq, n_pairs):
    i = pl.program_id(0)
    tiles_per_seq = dec_seq // TOKEN_TILE
    rows_per_tile = TOKEN_TILE // GRID_W
    rows = dec_seq // GRID_W
    win_keys = NA_WIN_R * GRID_W
    block_off = (i * TOKEN_TILE) % dec_seq

    @pl.when(i < n_prompt_tiles)
    def _():
        for s in range(TOKEN_TILE // seq):
            start = pl.multiple_of(block_off + s * seq, seq)
            for hp in range(n_pairs):
                lanes = slice(hp * LANES, (hp + 1) * LANES)
                out = _pair_attention(q_ref[s * seq:(s + 1) * seq, lanes],
                                      k_ref[pl.ds(start, seq), lanes],
                                      v_ref[pl.ds(start, seq), lanes])
                o_ref[s * seq:(s + 1) * seq, lanes] = out.astype(o_ref.dtype)

    @pl.when(i >= n_prompt_tiles)
    def _():
        row0 = ((i - n_prompt_tiles) % tiles_per_seq) * rows_per_tile

        def indices(rl):
            r = row0 + rl
            r0 = jnp.clip(r - NA_WIN_R // 2, 0, rows - NA_WIN_R)
            return r0 - r + NA_WIN_R - 1, pl.multiple_of(r0 * GRID_W, GRID_W)

        def scores(rl):
            d0, kstart = indices(rl)
            for hp in range(n_pairs):
                lanes = slice(hp * LANES, (hp + 1) * LANES)
                qbd = _block_diag_queries(q_ref[rl * GRID_W:(rl + 1) * GRID_W, lanes])
                bias = bias_ref[hp, pl.ds(d0, NA_WIN_R)].reshape(win_keys, LANES)
                s_ref[rl % 2, hp, 0:win_keys] = lax.dot_general(
                    k_ref[pl.ds(kstart, win_keys), lanes], qbd, _NT,
                    preferred_element_type=F32) + bias
                s_ref[rl % 2, hp, win_keys:] = lax.dot_general(
                    ck_ref[:, lanes], qbd, _NT, preferred_element_type=F32)

        def softmax(rl):
            for hp in range(n_pairs):
                s = s_ref[rl % 2, hp]
                p = jnp.exp2(s - jnp.max(s, axis=0, keepdims=True))
                rr = 1.0 / jnp.sum(p, axis=0, keepdims=True)
                p_ref[rl % 2, hp] = (p * rr).astype(BF16)

        def values(rl):
            _, kstart = indices(rl)
            for hp in range(n_pairs):
                lanes = slice(hp * LANES, (hp + 1) * LANES)
                o = (lax.dot_general(p_ref[rl % 2, hp, 0:win_keys],
                                     v_ref[pl.ds(kstart, win_keys), lanes],
                                     _TN, preferred_element_type=F32)
                     + lax.dot_general(p_ref[rl % 2, hp, win_keys:], cv_ref[:, lanes], _TN,
                                       preferred_element_type=F32))
                o_ref[rl * GRID_W:(rl + 1) * GRID_W, lanes] = (
                    _pick_head_blocks(o, GRID_W).astype(o_ref.dtype))

        for step in range(rows_per_tile + 2):
            if step < rows_per_tile:
                scores(step)
            if 1 <= step <= rows_per_tile:
                softmax(step - 1)
            if step >= 2:
                values(step - 2)


def _attention(q, k, v, ctx_k, ctx_v, bias, l, n_prompt, seq, dec_seq):
    n, na_w = q.shape
    n_tiles = n // TOKEN_TILE
    n_prompt_tiles = n_prompt // TOKEN_TILE
    tiles_per_seq = dec_seq // TOKEN_TILE
    past = ctx_k.shape[2]
    n_pairs = na_w // LANES
    n_keys = NA_WIN_R * GRID_W + past
    kv_spec = pl.BlockSpec((dec_seq, na_w), lambda i: (i * TOKEN_TILE // dec_seq, 0))
    ctx_spec = pl.BlockSpec(
        (None, None, past, na_w),
        lambda i: (jnp.maximum(i - n_prompt_tiles, 0) // tiles_per_seq, l, 0, 0))
    kernel = functools.partial(_attn_kernel, n_prompt_tiles=n_prompt_tiles, seq=seq,
                               dec_seq=dec_seq, n_pairs=n_pairs)
    return pl.pallas_call(
        kernel,
        grid=(n_tiles,),
        in_specs=[
            pl.BlockSpec((TOKEN_TILE, na_w), lambda i: (i, 0)),
            kv_spec, kv_spec, ctx_spec, ctx_spec,
            _resident(bias.shape, lambda i: (0, 0, 0, 0)),
        ],
        out_specs=pl.BlockSpec((TOKEN_TILE, na_w), lambda i: (i, 0)),
        out_shape=jax.ShapeDtypeStruct((n, na_w), BF16),
        scratch_shapes=[pltpu.VMEM((2, n_pairs, n_keys, LANES), F32),
                        pltpu.VMEM((2, n_pairs, n_keys, LANES), BF16)],
        compiler_params=_params(),
        name=f"attn{l}",
    )(q, k, v, ctx_k, ctx_v, bias)


def _relative_bias_table(rpb_l):
    heads = rpb_l.shape[0]
    kc = np.arange(GRID_W)[:, None]
    qc = np.arange(GRID_W)[None, :]
    q_start = np.clip(qc - NA_WIN_C // 2, 0, GRID_W - NA_WIN_C)
    valid = (kc >= q_start) & (kc < q_start + NA_WIN_C)
    dc_idx = np.clip(kc - qc, -(NA_WIN_C - 1), NA_WIN_C - 1) + NA_WIN_C - 1
    t = rpb_l[:, :, dc_idx]
    t = jnp.where(valid[None, None], t.astype(F32) * LOG2_E, -jnp.inf)
    t = t.reshape(heads // 2, 2, 2 * NA_WIN_R - 1, GRID_W, GRID_W)
    return t.transpose(0, 2, 3, 1, 4).reshape(heads // 2, 2 * NA_WIN_R - 1, GRID_W, 2 * GRID_W)


def _mod_static(t, m):
    return t & (m - 1) if m & (m - 1) == 0 else lax.rem(t, m)


def _pool_mix(i, b, pu_ref, pp_ref, pn_ref, ext_ref, wp_ref, ps_ref, *, n_prompt_tiles, seq, dec_seq):
    sub = seq
    h = POOL_HALO
    is_prompt = i < n_prompt_tiles
    seq_len = jnp.where(is_prompt, seq, dec_seq)
    lo, hi = b * sub, (b + 1) * sub
    tok0 = i * TOKEN_TILE + lo
    pos0 = jnp.where(is_prompt, _mod_static(tok0, seq), _mod_static(tok0, dec_seq))
    prev = pp_ref[...] if lo == 0 else pu_ref[lo - h:lo, :]
    nxt = pn_ref[...] if hi == TOKEN_TILE else pu_ref[hi:hi + h, :]
    ext_ref[b, 0:h] = jnp.where(pos0 != 0, prev, 0.0)
    ext_ref[b, h:h + sub] = pu_ref[lo:hi, :]
    ext_ref[b, h + sub:] = jnp.where(pos0 + sub != seq_len, nxt, 0.0)
    left = pos0 + lax.broadcasted_iota(jnp.int32, (sub, LANES), 0)
    right = seq_len - left
    outs = []
    for g, window in enumerate(POOL_WINDOWS):
        half = window // 2
        lanes = slice(g * LANES, (g + 1) * LANES)
        total = ext_ref[b, h - half:h - half + sub, lanes]
        for j in range(1 - half, half):
            total = total + ext_ref[b, h + j:h + j + sub, lanes]
        count = (jnp.minimum(left, half) + jnp.minimum(right, half)).astype(F32)
        pooled = total / count - pu_ref[lo:hi, lanes]
        outs.append(jnp.dot(pooled.astype(BF16), wp_ref[g], preferred_element_type=F32))
    return jnp.concatenate(outs, axis=-1) * ps_ref[...]


def _postmix_chain(i, x_refs, pu_ref, pp_ref, pn_ref, at_ref, mod_ref, wp_ref, ps_ref, wo_ref,
                   g2_ref, ext_ref, tail, **seq_info):
    sub = seq_info["seq"]
    n_sub = TOKEN_TILE // sub
    mod = mod_ref[...]
    is_prompt = i < seq_info["n_prompt_tiles"]

    def rows(b):
        return slice(b * sub, (b + 1) * sub)

    def pool_stage(b, _):
        return _pool_mix(i, b, pu_ref, pp_ref, pn_ref, ext_ref, wp_ref, ps_ref, **seq_info)

    def mix_stage(b, pool_out):
        mixed = jnp.concatenate([pool_out.astype(BF16), at_ref[rows(b), :]], axis=-1)
        mix = jnp.dot(mixed, wo_ref[...], preferred_element_type=F32)
        if len(x_refs) == 1:
            x = x_refs[0][rows(b), :]
        else:
            x = jnp.where(is_prompt, x_refs[0][rows(b), :], x_refs[1][rows(b), :])
        return x + mod[2:3] * mix

    def norm_stage(b, x2):
        inv = lax.rsqrt(jnp.mean(x2 * x2, axis=-1, keepdims=True) + NORM_EPS)
        return x2, (x2 * inv * g2_ref[...]) * (1.0 + mod[4:5]) + mod[3:4]

    def tail_stage(b, x2_h2):
        tail(rows(b), *x2_h2)

    stages = (pool_stage, mix_stage, norm_stage, tail_stage)
    state = [None] * n_sub
    for step in range(n_sub + len(stages) - 1):
        for b in range(n_sub):
            if 0 <= step - b < len(stages):
                state[b] = stages[step - b](b, state[b])
    return mod[5:6]


def _postmix_ffn_kernel(*refs, n_x, **seq_info):
    (pu_ref, pp_ref, pn_ref, at_ref, mod_ref, wp_ref, ps_ref, wo_ref,
     g2_ref, wg_ref, wu_ref, wd_ref, y_ref, ext_ref, h_ref, acc_ref) = refs[n_x:]
    i = pl.program_id(0)

    def tail(rows, x2, h2):
        y_ref[rows, :] = x2
        h_ref[rows, :] = h2.astype(BF16)

    gate2 = _postmix_chain(i, refs[:n_x], pu_ref, pp_ref, pn_ref, at_ref, mod_ref, wp_ref,
                           ps_ref, wo_ref, g2_ref, ext_ref, tail, **seq_info)
    acc_ref[...] = jnp.zeros_like(acc_ref)

    def chunk(c, carry):
        h = h_ref[...]
        a = jnp.dot(h, wg_ref[c], preferred_element_type=F32)
        b = jnp.dot(h, wu_ref[c], preferred_element_type=F32)
        act = (a * jax.nn.sigmoid(a) * b).astype(BF16)
        acc_ref[...] += jnp.dot(act, wd_ref[c], preferred_element_type=F32)
        return carry

    lax.fori_loop(0, wg_ref.shape[0], chunk, 0, unroll=True)
    y_ref[...] = y_ref[...] + gate2 * acc_ref[...]


def _postmix_router_kernel(*refs, n_x, n_experts, **seq_info):
    (pu_ref, pp_ref, pn_ref, at_ref, mod_ref, wp_ref, ps_ref, wo_ref,
     g2_ref, rh_ref, rl_ref, x2_ref, hr_ref, rout_ref, g0_ref, g1_ref, ext_ref) = refs[n_x:]
    i = pl.program_id(0)

    def tail(rows, x2, h2):
        x2_ref[rows, :] = x2
        n_rows = rows.stop - rows.start
        for s in range(h2.shape[1] // LANES):
            hr_ref[pl.ds(rows.start * SUBLANES + s, n_rows, stride=SUBLANES), :] = (
                h2[:, s * LANES:(s + 1) * LANES])
        hi = h2.astype(BF16)
        lo = (h2 - hi.astype(F32)).astype(BF16)
        logits = (jnp.dot(hi, rh_ref[...], preferred_element_type=F32)
                  + jnp.dot(lo, rh_ref[...], preferred_element_type=F32)
                  + jnp.dot(hi, rl_ref[...], preferred_element_type=F32))
        lane = lax.broadcasted_iota(jnp.int32, logits.shape, 1).astype(F32)
        logits = jnp.where(lane < n_experts, logits, -jnp.inf)
        m1 = jnp.max(logits, axis=-1, keepdims=True)
        i1 = jnp.min(jnp.where(logits == m1, lane, float(ROUTER_LANES)), axis=-1, keepdims=True)
        rest = jnp.where(lane == i1, -jnp.inf, logits)
        m2 = jnp.max(rest, axis=-1, keepdims=True)
        i2 = jnp.min(jnp.where(rest == m2, lane, float(ROUTER_LANES)), axis=-1, keepdims=True)
        e = jnp.exp(m2 - m1)
        rout_ref[rows, :] = jnp.where(lane == 0, i1, jnp.where(lane == 1, i2, 0.0))
        g0_ref[rows, :] = jnp.broadcast_to(1.0 / (1.0 + e), (n_rows, LANES))
        g1_ref[rows, :] = jnp.broadcast_to(e / (1.0 + e), (n_rows, LANES))

    _postmix_chain(i, refs[:n_x], pu_ref, pp_ref, pn_ref, at_ref, mod_ref, wp_ref,
                   ps_ref, wo_ref, g2_ref, ext_ref, tail, **seq_info)


def _pool_scratch_shape(seq_info, pool_w):
    sub = seq_info["seq"]
    return (TOKEN_TILE // sub, sub + 2 * POOL_HALO, pool_w)


def _postmix_specs(x, n, d, pool_w, na_w, l, cond_row, n_prompt_tiles):
    n_halo_blocks = n // POOL_HALO
    per_tile = TOKEN_TILE // POOL_HALO
    tok = lambda w: pl.BlockSpec((TOKEN_TILE, w), lambda i: (i, 0))
    return _x_specs(x, n_prompt_tiles) + [
        tok(pool_w),
        pl.BlockSpec((POOL_HALO, pool_w), lambda i: (jnp.maximum(i * per_tile - 1, 0), 0)),
        pl.BlockSpec((POOL_HALO, pool_w),
                     lambda i: (jnp.minimum((i + 1) * per_tile, n_halo_blocks - 1), 0)),
        tok(na_w),
        pl.BlockSpec((None, None, N_MOD, d), lambda i: (l, cond_row(i), 0, 0)),
        _resident((len(POOL_WINDOWS), LANES, LANES), lambda i: (0, 0, 0)),
        pl.BlockSpec((1, pool_w), lambda i: (0, 0)),
        _resident((d, d), lambda i: (0, 0)),
        pl.BlockSpec((1, d), lambda i: (0, 0)),
    ]


def _postmix_ffn(x, pu, attn, mod, l, w_pool, pool_scale, w_out, norm_g, wg, wu, wd,
                 seq_info, cond_row):
    xs = _as_tuple(x)
    n, d = pu.shape[0], xs[0].shape[1]
    pool_w = pu.shape[1]
    n_chunks = wg.shape[0]
    kernel = functools.partial(_postmix_ffn_kernel, n_x=len(xs), **seq_info)
    return pl.pallas_call(
        kernel,
        grid=(n // TOKEN_TILE,),
        in_specs=_postmix_specs(x, n, d, pool_w, attn.shape[1], l, cond_row,
                                seq_info["n_prompt_tiles"]) + [
            _resident((n_chunks, d, FF_CHUNK), lambda i: (0, 0, 0)),
            _resident((n_chunks, d, FF_CHUNK), lambda i: (0, 0, 0)),
            _resident((n_chunks, FF_CHUNK, d), lambda i: (0, 0, 0)),
        ],
        out_specs=pl.BlockSpec((TOKEN_TILE, d), lambda i: (i, 0)),
        out_shape=jax.ShapeDtypeStruct((n, d), F32),
        scratch_shapes=[
            pltpu.VMEM(_pool_scratch_shape(seq_info, pool_w), F32),
            pltpu.VMEM((TOKEN_TILE, d), BF16),
            pltpu.VMEM((TOKEN_TILE, d), F32),
        ],
        compiler_params=_params(),
        name=f"postmix_ffn{l}",
    )(*xs, pu, pu, pu, attn, mod, w_pool, pool_scale, w_out, norm_g, wg, wu, wd)


def _postmix_router(x, pu, attn, mod, l, w_pool, pool_scale, w_out, norm_g, r_hi, r_lo,
                    n_experts, seq_info, cond_row):
    xs = _as_tuple(x)
    n, d = pu.shape[0], xs[0].shape[1]
    pool_w = pu.shape[1]
    kernel = functools.partial(_postmix_router_kernel, n_x=len(xs), n_experts=n_experts,
                               **seq_info)
    row_tile = TOKEN_TILE * d // LANES
    return pl.pallas_call(
        kernel,
        grid=(n // TOKEN_TILE,),
        in_specs=_postmix_specs(x, n, d, pool_w, attn.shape[1], l, cond_row,
                                seq_info["n_prompt_tiles"]) + [
            pl.BlockSpec((d, ROUTER_LANES), lambda i: (0, 0)),
            pl.BlockSpec((d, ROUTER_LANES), lambda i: (0, 0)),
        ],
        out_specs=[
            pl.BlockSpec((TOKEN_TILE, d), lambda i: (i, 0)),
            pl.BlockSpec((row_tile, LANES), lambda i: (i, 0)),
            pl.BlockSpec((TOKEN_TILE, ROUTER_LANES), lambda i: (i, 0)),
            pl.BlockSpec((TOKEN_TILE, LANES), lambda i: (i, 0)),
            pl.BlockSpec((TOKEN_TILE, LANES), lambda i: (i, 0)),
        ],
        out_shape=[
            jax.ShapeDtypeStruct((n, d), F32),
            jax.ShapeDtypeStruct((n * d // LANES, LANES), F32),
            jax.ShapeDtypeStruct((n, ROUTER_LANES), F32),
            jax.ShapeDtypeStruct((n, LANES), F32),
            jax.ShapeDtypeStruct((n, LANES), F32),
        ],
        scratch_shapes=[pltpu.VMEM(_pool_scratch_shape(seq_info, pool_w), F32)],
        compiler_params=_params(),
        name=f"postmix_router{l}",
    )(*xs, pu, pu, pu, attn, mod, w_pool, pool_scale, w_out, norm_g, r_hi, r_lo)


def _moe_kernel(te_ref, na_ref, rf_ref, h_hbm, wg_ref, wu_ref, wd_ref, y_hbm,
                xbuf, ybuf, xb16, acc_ref, gsem, ssem, zsem, *,
                cpr, rows_per_step, n_flat, n_dump_tiles):
    j = pl.program_id(0)
    f = pl.program_id(1)
    n_f = pl.num_programs(1)
    tme = MOE_ROW_TILE
    tile_rows = tme * cpr
    n_active = na_ref[0]
    last_token = n_flat // TOP_K - 1

    def gather_row(tile, slot, row):
        v = rf_ref[(tile + 1) * tme + row]
        tok = jnp.minimum(lax.shift_right_logical(v, 1), last_token)
        return pltpu.make_async_copy(
            h_hbm.at[pl.ds(pl.multiple_of(tok * cpr, cpr), cpr)],
            xbuf.at[slot, pl.ds(pl.multiple_of(row * cpr, cpr), cpr)],
            gsem.at[slot])

    def scatter_row(tile, slot, row):
        v = rf_ref[(tile + 1) * tme + row]
        return pltpu.make_async_copy(
            ybuf.at[slot, pl.ds(pl.multiple_of(row * cpr, cpr), cpr)],
            y_hbm.at[pl.ds(pl.multiple_of(v * cpr, cpr), cpr)],
            ssem.at[slot])

    def wait_gather(slot):
        pltpu.make_async_copy(h_hbm.at[pl.ds(0, tile_rows)], xbuf.at[slot], gsem.at[slot]).wait()

    def wait_scatter(slot):
        pltpu.make_async_copy(ybuf.at[slot], y_hbm.at[pl.ds(0, tile_rows)], ssem.at[slot]).wait()

    def dump_fill(t):
        return pltpu.make_async_copy(
            ybuf.at[1], y_hbm.at[pl.ds((n_flat + t * tme) * cpr, tile_rows)], zsem)

    @pl.when((f == 0) & (j == 0))
    def _():
        ybuf[1] = jnp.zeros(ybuf.shape[1:], ybuf.dtype)
        for t in range(n_dump_tiles):
            dump_fill(t).start()
        for t in range(n_dump_tiles):
            dump_fill(t).wait()

        def body(r, carry):
            gather_row(0, 0, r).start()
            return carry

        lax.fori_loop(0, tme, body, 0)

    for par in range(2):
        @pl.when((j < n_active) & (j % 2 == par))
        def _(par=par):
            @pl.when(f == 0)
            def _():
                wait_gather(par)
                for s in range(cpr):
                    xb16[:, s * LANES:(s + 1) * LANES] = (
                        xbuf[par, pl.ds(s, tme, stride=cpr), :].astype(BF16))
                acc_ref[...] = jnp.zeros_like(acc_ref)

            for t in range(rows_per_step):
                gather_row(j + 1, 1 - par, t * n_f + f).start()
                scatter_row(j - 1, 1 - par, t * n_f + f).start()
            x = xb16[...]
            a = jnp.dot(x, wg_ref[...].astype(BF16), preferred_element_type=F32)
            b = jnp.dot(x, wu_ref[...].astype(BF16), preferred_element_type=F32)
            act = (a * jax.nn.sigmoid(a) * b).astype(BF16)
            acc_ref[...] += jnp.dot(act, wd_ref[...].astype(BF16), preferred_element_type=F32)

            @pl.when(f == n_f - 1)
            def _():
                @pl.when(j >= 1)
                def _():
                    wait_scatter(par)

                for s in range(cpr):
                    ybuf[par, pl.ds(s, tme, stride=cpr), :] = acc_ref[:, s * LANES:(s + 1) * LANES]

        @pl.when((j == n_active) & (f == 0) & (j % 2 == par))
        def _(par=par):
            def body(r, carry):
                scatter_row(j - 1, 1 - par, r).start()
                return carry

            lax.fori_loop(0, tme, body, 0)
            wait_gather(par)
            wait_scatter(par)
            wait_scatter(1 - par)


def _moe(tile_expert, n_active, row_flat, h_rows, wg, wu, wd, d, n_flat, n_dump_tiles):
    n_steps = tile_expert.shape[0]
    d_exp = wg.shape[2]
    n_f = d_exp // MOE_F_CHUNK
    cpr = d // LANES
    tile_rows = MOE_ROW_TILE * cpr
    assert MOE_ROW_TILE % n_f == 0

    def f_idx(j, f, na):
        return jnp.where(j < na[0], f, n_f - 1)

    grid_spec = pltpu.PrefetchScalarGridSpec(
        num_scalar_prefetch=3,
        grid=(n_steps, n_f),
        in_specs=[
            pl.BlockSpec(memory_space=pl.ANY),
            pl.BlockSpec((None, d, MOE_F_CHUNK), lambda j, f, te, na, rf: (te[j], 0, f_idx(j, f, na))),
            pl.BlockSpec((None, d, MOE_F_CHUNK), lambda j, f, te, na, rf: (te[j], 0, f_idx(j, f, na))),
            pl.BlockSpec((None, MOE_F_CHUNK, d), lambda j, f, te, na, rf: (te[j], f_idx(j, f, na), 0)),
        ],
        out_specs=pl.BlockSpec(memory_space=pl.ANY),
        scratch_shapes=[
            pltpu.VMEM((2, tile_rows, LANES), F32),
            pltpu.VMEM((2, tile_rows, LANES), F32),
            pltpu.VMEM((MOE_ROW_TILE, d), BF16),
            pltpu.VMEM((MOE_ROW_TILE, d), F32),
            pltpu.SemaphoreType.DMA((2,)),
            pltpu.SemaphoreType.DMA((2,)),
            pltpu.SemaphoreType.DMA,
        ],
    )
    kernel = functools.partial(_moe_kernel, cpr=cpr, rows_per_step=MOE_ROW_TILE // n_f,
                               n_flat=n_flat, n_dump_tiles=n_dump_tiles)
    return pl.pallas_call(
        kernel,
        grid_spec=grid_spec,
        out_shape=jax.ShapeDtypeStruct(((n_flat + n_dump_tiles * MOE_ROW_TILE) * cpr, LANES), F32),
        compiler_params=_params(2),
        name="moe",
    )(tile_expert, n_active, row_flat, h_rows, wg, wu, wd)


def _combine_kernel(y_ref, x2_ref, g0_ref, g1_ref, mod_ref, *o_refs, cpr, n_prompt_tiles):
    i = pl.program_id(0)
    tm = x2_ref.shape[0]
    gate2 = mod_ref[...][5:6]
    stride = TOP_K * cpr

    def write(o_ref):
        g0 = g0_ref[...]
        g1 = g1_ref[...]
        for s in range(cpr):
            lanes = slice(s * LANES, (s + 1) * LANES)
            ya = y_ref[pl.ds(s, tm, stride=stride), :]
            yb = y_ref[pl.ds(cpr + s, tm, stride=stride), :]
            o_ref[:, lanes] = x2_ref[:, lanes] + gate2[:, lanes] * (g0 * ya + g1 * yb)

    if len(o_refs) == 1:
        write(o_refs[0])
    else:
        pl.when(i < n_prompt_tiles)(lambda: write(o_refs[0]))
        pl.when(i >= n_prompt_tiles)(lambda: write(o_refs[1]))


def _combine(y_rows, x2, g0, g1, mod, l, cond_row, n_prompt, split_output):
    n, d = x2.shape
    cpr = d // LANES
    n_prompt_tiles = n_prompt // TOKEN_TILE
    kernel = functools.partial(_combine_kernel, cpr=cpr, n_prompt_tiles=n_prompt_tiles)
    if split_output:
        out_specs = [
            pl.BlockSpec((TOKEN_TILE, d), lambda i: (jnp.minimum(i, n_prompt_tiles - 1), 0)),
            pl.BlockSpec((TOKEN_TILE, d), lambda i: (jnp.maximum(i - n_prompt_tiles, 0), 0))]
        out_shape = [jax.ShapeDtypeStruct((n_prompt, d), F32),
                     jax.ShapeDtypeStruct((n - n_prompt, d), F32)]
    else:
        out_specs = pl.BlockSpec((TOKEN_TILE, d), lambda i: (i, 0))
        out_shape = jax.ShapeDtypeStruct((n, d), F32)
    return pl.pallas_call(
        kernel,
        grid=(n // TOKEN_TILE,),
        in_specs=[
            pl.BlockSpec((TOKEN_TILE * TOP_K * cpr, LANES), lambda i: (i, 0)),
            pl.BlockSpec((TOKEN_TILE, d), lambda i: (i, 0)),
            pl.BlockSpec((TOKEN_TILE, LANES), lambda i: (i, 0)),
            pl.BlockSpec((TOKEN_TILE, LANES), lambda i: (i, 0)),
            pl.BlockSpec((None, None, N_MOD, d), lambda i: (l, cond_row(i), 0, 0)),
        ],
        out_specs=out_specs,
        out_shape=out_shape,
        compiler_params=_params(),
        name="combine",
    )(y_rows, x2, g0, g1, mod)


def _dispatch_plan(expert_ids, n_experts, n_tiles_max):
    tme = MOE_ROW_TILE
    n_flat = expert_ids.size
    e_flat = expert_ids.reshape(-1)
    onehot = (e_flat[:, None] == jnp.arange(n_experts)[None, :]).astype(jnp.int32)
    csum = jnp.cumsum(onehot, axis=0)
    rank = jnp.take_along_axis(csum, e_flat[:, None], axis=1)[:, 0] - 1
    counts = csum[-1]
    tiles = (counts + tme - 1) // tme
    tile_end = jnp.cumsum(tiles)
    row_off = (tile_end - tiles) * tme
    pos = tme + row_off[e_flat] + rank
    n_active = tile_end[-1]
    n_steps = n_tiles_max + 1
    tile_ids = jnp.minimum(jnp.arange(n_steps), n_active - 1)
    tile_expert = jnp.sum(tile_ids[:, None] >= tile_end[None, :], axis=1).astype(jnp.int32)
    n_dump_tiles = n_experts + 1
    rf = jnp.full(((n_steps + 1) * tme,), -1, jnp.int32).at[pos].set(
        jnp.arange(n_flat, dtype=jnp.int32), unique_indices=True)
    is_pad = rf < 0
    pad_slot = jnp.minimum(jnp.cumsum(is_pad.astype(jnp.int32)) - 1, n_dump_tiles * tme - 1)
    rf = jnp.where(is_pad, n_flat + pad_slot, rf)
    return rf, tile_expert, n_active.astype(jnp.int32).reshape(1), n_dump_tiles


def kernel(x_prompt, x_sample, cache_k, cache_v, c, c_ctx, norm1_g, norm2_g, w_ada, b_ada, w_in,
           q_norm_g, k_norm_g, w_pool, pool_scale, rpb, w_out, ffn_w_gate, ffn_w_up, ffn_w_down,
           moe_router, moe_w_gate, moe_w_up, moe_w_down):
    batch, seq, d = x_prompt.shape
    dec_batch, dec_seq, _ = x_sample.shape
    depth = w_in.shape[0]
    heads = cache_k.shape[3]
    na_w = heads * HEAD_DIM
    pool_w = d - na_w
    n_prompt = batch * seq
    n = n_prompt + dec_batch * dec_seq
    n_experts = moe_router.shape[2]
    assert pool_w == len(POOL_WINDOWS) * LANES and na_w % LANES == 0
    assert TOKEN_TILE % seq == 0 and n_prompt % dec_seq == 0 and dec_seq % TOKEN_TILE == 0
    assert dec_seq // GRID_W >= NA_WIN_R and dec_batch < COND_ROWS
    n_prompt_tiles = n_prompt // TOKEN_TILE
    seq_info = dict(n_prompt_tiles=n_prompt_tiles, seq=seq, dec_seq=dec_seq)

    def cond_row(i):
        start = i * TOKEN_TILE
        return jnp.where(start < n_prompt, dec_batch, (start - n_prompt) // dec_seq)

    cond = jnp.zeros((COND_ROWS, d), F32).at[:dec_batch].set(c).at[dec_batch].set(c_ctx)
    mod = _ada(cond, w_ada, b_ada).reshape(depth, COND_ROWS, N_MOD, d)

    hsum = jnp.asarray(np.kron(np.eye(heads), np.ones((HEAD_DIM, HEAD_DIM))), BF16)
    ctx_k = cache_k.reshape(dec_batch, depth, cache_k.shape[2], na_w).astype(BF16)
    ctx_v = cache_v.reshape(dec_batch, depth, cache_v.shape[2], na_w).astype(BF16)

    x = (x_prompt.reshape(n_prompt, d), x_sample.reshape(-1, d))
    cache_kv = (jnp.zeros((batch, depth, seq * heads, HEAD_DIM), F32),) * 2
    for l in range(depth):
        pu, q, k, v, *cache_kv = _premix(
            x, mod, l, norm1_g[l][None], w_in[l].astype(BF16),
            jnp.tile(q_norm_g[l], heads)[None], jnp.tile(k_norm_g[l], heads)[None], hsum,
            n_prompt, cond_row, cache_kv, seq)
        attn = _attention(q, k, v, ctx_k, ctx_v, _relative_bias_table(rpb[l]), l,
                          n_prompt, seq, dec_seq)
        mix_args = (mod, l, w_pool[l].astype(BF16), pool_scale[l][None], w_out[l].astype(BF16),
                    norm2_g[l][None])
        li = l // 2
        if l % 2 == 0:
            d_ff = ffn_w_gate.shape[2]
            n_chunks = d_ff // FF_CHUNK
            wg = ffn_w_gate[li].astype(BF16).reshape(d, n_chunks, FF_CHUNK).transpose(1, 0, 2)
            wu = ffn_w_up[li].astype(BF16).reshape(d, n_chunks, FF_CHUNK).transpose(1, 0, 2)
            wd = ffn_w_down[li].astype(BF16).reshape(n_chunks, FF_CHUNK, d)
            x = _postmix_ffn(x, pu, attn, *mix_args, wg, wu, wd, seq_info, cond_row)
        else:
            router = jnp.zeros((d, ROUTER_LANES), F32).at[:, :n_experts].set(moe_router[li])
            r_hi = router.astype(BF16)
            r_lo = (router - r_hi.astype(F32)).astype(BF16)
            x2, h_rows, rout, g0, g1 = _postmix_router(
                x, pu, attn, *mix_args, r_hi, r_lo, n_experts, seq_info, cond_row)
            expert_ids = rout[:, :TOP_K].astype(jnp.int32)
            n_tiles_max = (TOP_K * n + n_experts * (MOE_ROW_TILE - 1)) // MOE_ROW_TILE
            row_flat, tile_expert, n_active, n_dump_tiles = _dispatch_plan(
                expert_ids, n_experts, n_tiles_max)
            y_rows = _moe(tile_expert, n_active, row_flat, h_rows,
                          moe_w_gate[li], moe_w_up[li], moe_w_down[li], d, TOP_K * n, n_dump_tiles)
            x = _combine(y_rows, x2, g0, g1, mod, l, cond_row, n_prompt,
                         split_output=(l == depth - 1))

    if not isinstance(x, tuple):
        x = (x[:n_prompt], x[n_prompt:])
    y_prompt = x[0].reshape(batch, seq, d)
    y_sample = x[1].reshape(dec_batch, dec_seq, d)
    new_k, new_v = (a.reshape(batch, depth, seq, heads, HEAD_DIM) for a in cache_kv)
    return (y_prompt, y_sample, new_k, new_v)
```

```python
import functools

import numpy as np
import jax
import jax.numpy as jnp
from jax import lax
from jax.experimental import pallas as pl
from jax.experimental.pallas import tpu as pltpu

F32 = jnp.float32
BF16 = jnp.bfloat16

GRID_W = 64
POOL_WINDOWS = (2, 4, 8, 16)
HEAD_DIM = 64
NA_WIN_R = 8
NA_WIN_C = 16
N_MOD = 6
TOP_K = 2
NORM_EPS = 1e-6
LOG2_E = 1.4426950408889634

LANES = 128
SUBLANES = 8
VMEM_LIMIT_BYTES = 56 * 1024 * 1024

TOKEN_TILE = 512
POOL_HALO = 8
FF_CHUNK = 256
MOE_ROW_TILE = 1008
MOE_F_CHUNK = 512
ADA_COL_TILE = 1024
COND_ROWS = 16
ROUTER_LANES = 128


def _params(n_axes=1):
    return pltpu.CompilerParams(
        dimension_semantics=("arbitrary",) * n_axes,
        vmem_limit_bytes=VMEM_LIMIT_BYTES,
    )


def _resident(shape, index_map):
    return pl.BlockSpec(shape, index_map, pipeline_mode=pl.Buffered(1))


def _x_specs(x, n_prompt_tiles):
    if not isinstance(x, tuple):
        return [pl.BlockSpec((TOKEN_TILE, x.shape[1]), lambda i: (i, 0))]
    d = x[0].shape[1]
    return [pl.BlockSpec((TOKEN_TILE, d), lambda i: (jnp.minimum(i, n_prompt_tiles - 1), 0)),
            pl.BlockSpec((TOKEN_TILE, d), lambda i: (jnp.maximum(i - n_prompt_tiles, 0), 0))]


def _load_x(i, x_refs, n_prompt_tiles):
    if len(x_refs) == 1:
        return x_refs[0][...]
    return jnp.where(i < n_prompt_tiles, x_refs[0][...], x_refs[1][...])


def _as_tuple(x):
    return x if isinstance(x, tuple) else (x,)


def _ada_kernel(c_ref, w_ref, b_ref, o_ref):
    c = c_ref[...]
    s = c * jax.nn.sigmoid(c)
    o_ref[...] = jnp.dot(s.astype(BF16), w_ref[...].astype(BF16),
                         preferred_element_type=F32) + b_ref[...]


def _ada(cond, w_ada, b_ada):
    depth, d, width = w_ada.shape
    return pl.pallas_call(
        _ada_kernel,
        grid=(depth, width // ADA_COL_TILE),
        in_specs=[
            pl.BlockSpec((COND_ROWS, d), lambda l, j: (0, 0)),
            pl.BlockSpec((None, d, ADA_COL_TILE), lambda l, j: (l, 0, j)),
            pl.BlockSpec((None, 1, ADA_COL_TILE), lambda l, j: (l, 0, j)),
        ],
        out_specs=pl.BlockSpec((None, COND_ROWS, ADA_COL_TILE), lambda l, j: (l, 0, j)),
        out_shape=jax.ShapeDtypeStruct((depth, COND_ROWS, width), F32),
        compiler_params=_params(2),
        name="ada",
    )(cond, w_ada, b_ada.reshape(depth, 1, width))


def _premix_kernel(*refs, n_x, n_prompt_tiles, pool_w, na_w, seq):
    (mod_ref, g_ref, w_ref, qg_ref, kg_ref, hsum_ref, _, _,
     pu_ref, q_ref, k_ref, v_ref, kf_ref, vf_ref) = refs[n_x:]
    i = pl.program_id(0)
    x = _load_x(i, refs[:n_x], n_prompt_tiles)
    inv = lax.rsqrt(jnp.mean(x * x, axis=-1, keepdims=True) + NORM_EPS)
    mod = mod_ref[...]
    h = (x * inv * g_ref[...]) * (1.0 + mod[1:2]) + mod[0:1]
    u = jnp.dot(h.astype(BF16), w_ref[...], preferred_element_type=F32)
    pu_ref[...] = u[:, :pool_w]
    q = u[:, pool_w:pool_w + na_w]
    k = u[:, pool_w + na_w:pool_w + 2 * na_w]
    v = u[:, pool_w + 2 * na_w:]

    def head_norm(t, g):
        ms = jnp.dot((t * t).astype(BF16), hsum_ref[...],
                     preferred_element_type=F32) * (1.0 / HEAD_DIM)
        return t * lax.rsqrt(ms + NORM_EPS) * g

    qn = head_norm(q, qg_ref[...])
    kn = head_norm(k, kg_ref[...])
    q_ref[...] = (qn * (HEAD_DIM ** -0.5 * LOG2_E)).astype(BF16)
    k_ref[...] = kn.astype(BF16)
    v_ref[...] = v.astype(BF16)

    @pl.when(i < n_prompt_tiles)
    def _():
        heads = na_w // HEAD_DIM
        for b in range(TOKEN_TILE // seq):
            for hd in range(heads):
                rows, cols = slice(b * seq, (b + 1) * seq), slice(hd * HEAD_DIM, (hd + 1) * HEAD_DIM)
                kf_ref[b, pl.ds(hd, seq, stride=heads), :] = kn[rows, cols]
                vf_ref[b, pl.ds(hd, seq, stride=heads), :] = v[rows, cols]


def _premix(x, mod, l, norm_g, w_in, q_g, k_g, hsum, n_prompt, cond_row, cache_kv, seq):
    xs = _as_tuple(x)
    n, d = sum(a.shape[0] for a in xs), xs[0].shape[1]
    pool_w = d // 2
    na_w = d - pool_w
    n_tiles = n // TOKEN_TILE
    n_prompt_tiles = n_prompt // TOKEN_TILE
    last_p = n_prompt_tiles - 1
    tok = lambda w: pl.BlockSpec((TOKEN_TILE, w), lambda i: (i, 0))
    cache_spec = pl.BlockSpec((TOKEN_TILE // seq, None) + cache_kv[0].shape[2:],
                              lambda i: (jnp.minimum(i, last_p), l, 0, 0))
    any_spec = pl.BlockSpec(memory_space=pl.ANY)
    n_in = len(xs) + 8
    kernel = functools.partial(_premix_kernel, n_x=len(xs), n_prompt_tiles=n_prompt_tiles,
                               pool_w=pool_w, na_w=na_w, seq=seq)
    return pl.pallas_call(
        kernel,
        grid=(n_tiles,),
        in_specs=_x_specs(x, n_prompt_tiles) + [
            pl.BlockSpec((None, None, N_MOD, d), lambda i: (l, cond_row(i), 0, 0)),
            pl.BlockSpec((1, d), lambda i: (0, 0)),
            _resident((d, w_in.shape[1]), lambda i: (0, 0)),
            pl.BlockSpec((1, na_w), lambda i: (0, 0)),
            pl.BlockSpec((1, na_w), lambda i: (0, 0)),
            _resident((na_w, na_w), lambda i: (0, 0)),
            any_spec, any_spec,
        ],
        out_specs=[tok(pool_w), tok(na_w), tok(na_w), tok(na_w), cache_spec, cache_spec],
        out_shape=[
            jax.ShapeDtypeStruct((n, pool_w), F32),
            jax.ShapeDtypeStruct((n, na_w), BF16),
            jax.ShapeDtypeStruct((n, na_w), BF16),
            jax.ShapeDtypeStruct((n, na_w), BF16),
            jax.ShapeDtypeStruct(cache_kv[0].shape, F32),
            jax.ShapeDtypeStruct(cache_kv[1].shape, F32),
        ],
        input_output_aliases={n_in - 2: 4, n_in - 1: 5},
        compiler_params=_params(),
        name=f"premix{l}",
    )(*xs, mod, norm_g, w_in, q_g, k_g, hsum, *cache_kv)


_NT = (((1,), (1,)), ((), ()))
_TN = (((0,), (0,)), ((), ()))


def _block_diag_queries(q2):
    lo = lax.broadcasted_iota(jnp.int32, q2.shape, 1) < HEAD_DIM
    zero = jnp.zeros_like(q2)
    return jnp.concatenate([jnp.where(lo, q2, zero), jnp.where(lo, zero, q2)], axis=0)


def _pick_head_blocks(o, nq):
    lo = lax.broadcasted_iota(jnp.int32, (nq, LANES), 1) < HEAD_DIM
    return jnp.where(lo, o[:nq], o[nq:])


def _pair_attention(q2, k, v):
    nq = q2.shape[0]
    s = lax.dot_general(k, _block_diag_queries(q2), _NT, preferred_element_type=F32)
    p = jnp.exp2(s - jnp.max(s, axis=0, keepdims=True))
    r = 1.0 / jnp.sum(p, axis=0, keepdims=True)
    o = lax.dot_general((p * r).astype(BF16), v, _TN, preferred_element_type=F32)
    return _pick_head_blocks(o, nq)


def _attn_kernel(q_ref, k_ref, v_ref, ckf_ref, cvf_ref, bias_ref, o_ref, s_ref, p_ref,
                 ck_ref, cv_ref, *, n_prompt_tiles, seq, dec_seq, n_pairs):
    i = pl.program_id(0)
    tiles_per_seq = dec_seq // TOKEN_TILE
    rows_per_tile = TOKEN_TILE // GRID_W
    rows = dec_seq // GRID_W
    win_keys = NA_WIN_R * GRID_W
    block_off = (i * TOKEN_TILE) % dec_seq

    @pl.when(i < n_prompt_tiles)
    def _():
        for s in range(TOKEN_TILE // seq):
            start = pl.multiple_of(block_off + s * seq, seq)
            for hp in range(n_pairs):
                lanes = slice(hp * LANES, (hp + 1) * LANES)
                out = _pair_attention(q_ref[s * seq:(s + 1) * seq, lanes],
                                      k_ref[pl.ds(start, seq), lanes],
                                      v_ref[pl.ds(start, seq), lanes])
                o_ref[s * seq:(s + 1) * seq, lanes] = out.astype(o_ref.dtype)

    @pl.when(i >= n_prompt_tiles)
    def _():
        tile_in_seq = (i - n_prompt_tiles) % tiles_per_seq
        row0 = tile_in_seq * rows_per_tile

        @pl.when(tile_in_seq == 0)
        def _():
            past = ck_ref.shape[0]
            heads = 2 * n_pairs
            for src, dst in ((ckf_ref, ck_ref), (cvf_ref, cv_ref)):
                for hp in range(n_pairs):
                    pair = [src[pl.ds(2 * hp + t, past, stride=heads), :] for t in range(2)]
                    dst[:, hp * LANES:(hp + 1) * LANES] = (
                        jnp.concatenate(pair, axis=-1).astype(BF16))

        def indices(rl):
            r = row0 + rl
            r0 = jnp.clip(r - NA_WIN_R // 2, 0, rows - NA_WIN_R)
            return r0 - r + NA_WIN_R - 1, pl.multiple_of(r0 * GRID_W, GRID_W)

        def scores(rl):
            d0, kstart = indices(rl)
            for hp in range(n_pairs):
                lanes = slice(hp * LANES, (hp + 1) * LANES)
                qbd = _block_diag_queries(q_ref[rl * GRID_W:(rl + 1) * GRID_W, lanes])
                bias = bias_ref[hp, pl.ds(d0, NA_WIN_R)].reshape(win_keys, LANES)
                s_ref[rl % 2, hp, 0:win_keys] = lax.dot_general(
                    k_ref[pl.ds(kstart, win_keys), lanes], qbd, _NT,
                    preferred_element_type=F32) + bias
                s_ref[rl % 2, hp, win_keys:] = lax.dot_general(
                    ck_ref[:, lanes], qbd, _NT, preferred_element_type=F32)

        def softmax(rl):
            for hp in range(n_pairs):
                s = s_ref[rl % 2, hp]
                p = jnp.exp2(s - jnp.max(s, axis=0, keepdims=True))
                rr = 1.0 / jnp.sum(p, axis=0, keepdims=True)
                p_ref[rl % 2, hp] = (p * rr).astype(BF16)

        def values(rl):
            _, kstart = indices(rl)
            for hp in range(n_pairs):
                lanes = slice(hp * LANES, (hp + 1) * LANES)
                o = (lax.dot_general(p_ref[rl % 2, hp, 0:win_keys],
                                     v_ref[pl.ds(kstart, win_keys), lanes],
                                     _TN, preferred_element_type=F32)
                     + lax.dot_general(p_ref[rl % 2, hp, win_keys:], cv_ref[:, lanes], _TN,
                                       preferred_element_type=F32))
                o_ref[rl * GRID_W:(rl + 1) * GRID_W, lanes] = (
                    _pick_head_blocks(o, GRID_W).astype(o_ref.dtype))

        for step in range(rows_per_tile + 2):
            if step < rows_per_tile:
                scores(step)
            if 1 <= step <= rows_per_tile:
                softmax(step - 1)
            if step >= 2:
                values(step - 2)


def _attention(q, k, v, ctx_k, ctx_v, bias, l, n_prompt, seq, dec_seq):
    n, na_w = q.shape
    n_tiles = n // TOKEN_TILE
    n_prompt_tiles = n_prompt // TOKEN_TILE
    tiles_per_seq = dec_seq // TOKEN_TILE
    heads = na_w // HEAD_DIM
    past = ctx_k.shape[2] // heads
    n_pairs = na_w // LANES
    n_keys = NA_WIN_R * GRID_W + past
    kv_spec = pl.BlockSpec((dec_seq, na_w), lambda i: (i * TOKEN_TILE // dec_seq, 0))
    ctx_spec = pl.BlockSpec(
        (None, None, past * heads, HEAD_DIM),
        lambda i: (jnp.maximum(i - n_prompt_tiles, 0) // tiles_per_seq, l, 0, 0))
    kernel = functools.partial(_attn_kernel, n_prompt_tiles=n_prompt_tiles, seq=seq,
                               dec_seq=dec_seq, n_pairs=n_pairs)
    return pl.pallas_call(
        kernel,
        grid=(n_tiles,),
        in_specs=[
            pl.BlockSpec((TOKEN_TILE, na_w), lambda i: (i, 0)),
            kv_spec, kv_spec, ctx_spec, ctx_spec,
            _resident(bias.shape, lambda i: (0, 0, 0, 0)),
        ],
        out_specs=pl.BlockSpec((TOKEN_TILE, na_w), lambda i: (i, 0)),
        out_shape=jax.ShapeDtypeStruct((n, na_w), BF16),
        scratch_shapes=[pltpu.VMEM((2, n_pairs, n_keys, LANES), F32),
                        pltpu.VMEM((2, n_pairs, n_keys, LANES), BF16),
                        pltpu.VMEM((past, na_w), BF16),
                        pltpu.VMEM((past, na_w), BF16)],
        compiler_params=_params(),
        name=f"attn{l}",
    )(q, k, v, ctx_k, ctx_v, bias)


def _relative_bias_table(rpb_l):
    heads = rpb_l.shape[0]
    kc = np.arange(GRID_W)[:, None]
    qc = np.arange(GRID_W)[None, :]
    q_start = np.clip(qc - NA_WIN_C // 2, 0, GRID_W - NA_WIN_C)
    valid = (kc >= q_start) & (kc < q_start + NA_WIN_C)
    dc_idx = np.clip(kc - qc, -(NA_WIN_C - 1), NA_WIN_C - 1) + NA_WIN_C - 1
    t = rpb_l[:, :, dc_idx]
    t = jnp.where(valid[None, None], t.astype(F32) * LOG2_E, -jnp.inf)
    t = t.reshape(heads // 2, 2, 2 * NA_WIN_R - 1, GRID_W, GRID_W)
    return t.transpose(0, 2, 3, 1, 4).reshape(heads // 2, 2 * NA_WIN_R - 1, GRID_W, 2 * GRID_W)


def _mod_static(t, m):
    return t & (m - 1) if m & (m - 1) == 0 else lax.rem(t, m)


def _pool_mix(i, b, pu_ref, pp_ref, pn_ref, ext_ref, wp_ref, ps_ref, *, n_prompt_tiles, seq, dec_seq):
    sub = seq
    h = POOL_HALO
    is_prompt = i < n_prompt_tiles
    seq_len = jnp.where(is_prompt, seq, dec_seq)
    lo, hi = b * sub, (b + 1) * sub
    tok0 = i * TOKEN_TILE + lo
    pos0 = jnp.where(is_prompt, _mod_static(tok0, seq), _mod_static(tok0, dec_seq))
    prev = pp_ref[...] if lo == 0 else pu_ref[lo - h:lo, :]
    nxt = pn_ref[...] if hi == TOKEN_TILE else pu_ref[hi:hi + h, :]
    ext_ref[b, 0:h] = jnp.where(pos0 != 0, prev, 0.0)
    ext_ref[b, h:h + sub] = pu_ref[lo:hi, :]
    ext_ref[b, h + sub:] = jnp.where(pos0 + sub != seq_len, nxt, 0.0)
    left = pos0 + lax.broadcasted_iota(jnp.int32, (sub, LANES), 0)
    right = seq_len - left
    outs = []
    for g, window in enumerate(POOL_WINDOWS):
        half = window // 2
        lanes = slice(g * LANES, (g + 1) * LANES)
        total = ext_ref[b, h - half:h - half + sub, lanes]
        for j in range(1 - half, half):
            total = total + ext_ref[b, h + j:h + j + sub, lanes]
        count = (jnp.minimum(left, half) + jnp.minimum(right, half)).astype(F32)
        pooled = total / count - pu_ref[lo:hi, lanes]
        outs.append(jnp.dot(pooled.astype(BF16), wp_ref[g], preferred_element_type=F32))
    return jnp.concatenate(outs, axis=-1) * ps_ref[...]


def _postmix_chain(i, x_refs, pu_ref, pp_ref, pn_ref, at_ref, mod_ref, wp_ref, ps_ref, wo_ref,
                   g2_ref, ext_ref, tail, **seq_info):
    sub = seq_info["seq"]
    n_sub = TOKEN_TILE // sub
    mod = mod_ref[...]
    is_prompt = i < seq_info["n_prompt_tiles"]

    def rows(b):
        return slice(b * sub, (b + 1) * sub)

    def pool_stage(b, _):
        return _pool_mix(i, b, pu_ref, pp_ref, pn_ref, ext_ref, wp_ref, ps_ref, **seq_info)

    def mix_stage(b, pool_out):
        mixed = jnp.concatenate([pool_out.astype(BF16), at_ref[rows(b), :]], axis=-1)
        mix = jnp.dot(mixed, wo_ref[...], preferred_element_type=F32)
        if len(x_refs) == 1:
            x = x_refs[0][rows(b), :]
        else:
            x = jnp.where(is_prompt, x_refs[0][rows(b), :], x_refs[1][rows(b), :])
        return x + mod[2:3] * mix

    def norm_stage(b, x2):
        inv = lax.rsqrt(jnp.mean(x2 * x2, axis=-1, keepdims=True) + NORM_EPS)
        return x2, (x2 * inv * g2_ref[...]) * (1.0 + mod[4:5]) + mod[3:4]

    def tail_stage(b, x2_h2):
        tail(rows(b), *x2_h2)

    stages = (pool_stage, mix_stage, norm_stage, tail_stage)
    state = [None] * n_sub
    for step in range(n_sub + len(stages) - 1):
        for b in range(n_sub):
            if 0 <= step - b < len(stages):
                state[b] = stages[step - b](b, state[b])
    return mod[5:6]


def _postmix_ffn_kernel(*refs, n_x, **seq_info):
    (pu_ref, pp_ref, pn_ref, at_ref, mod_ref, wp_ref, ps_ref, wo_ref,
     g2_ref, wg_ref, wu_ref, wd_ref, y_ref, ext_ref, h_ref, acc_ref) = refs[n_x:]
    i = pl.program_id(0)

    def tail(rows, x2, h2):
        y_ref[rows, :] = x2
        h_ref[rows, :] = h2.astype(BF16)

    gate2 = _postmix_chain(i, refs[:n_x], pu_ref, pp_ref, pn_ref, at_ref, mod_ref, wp_ref,
                           ps_ref, wo_ref, g2_ref, ext_ref, tail, **seq_info)
    acc_ref[...] = jnp.zeros_like(acc_ref)

    h = h_ref[...]
    for c in range(wg_ref.shape[1] // FF_CHUNK):
        cols = slice(c * FF_CHUNK, (c + 1) * FF_CHUNK)
        a = jnp.dot(h, wg_ref[:, cols], preferred_element_type=F32)
        b = jnp.dot(h, wu_ref[:, cols], preferred_element_type=F32)
        act = (a * jax.nn.sigmoid(a) * b).astype(BF16)
        acc_ref[...] += jnp.dot(act, wd_ref[cols, :], preferred_element_type=F32)
    y_ref[...] = y_ref[...] + gate2 * acc_ref[...]


def _postmix_router_kernel(*refs, n_x, n_experts, **seq_info):
    (pu_ref, pp_ref, pn_ref, at_ref, mod_ref, wp_ref, ps_ref, wo_ref,
     g2_ref, rh_ref, rl_ref, tri_ref, x2_ref, hr_ref, rout_ref, g0_ref, g1_ref, cnt_ref,
     ext_ref) = refs[n_x:]
    i = pl.program_id(0)

    @pl.when(i == 0)
    def _():
        cnt_ref[...] = jnp.zeros_like(cnt_ref)

    def tail(rows, x2, h2):
        x2_ref[rows, :] = x2
        n_rows = rows.stop - rows.start
        for s in range(h2.shape[1] // LANES):
            hr_ref[pl.ds(rows.start * SUBLANES + s, n_rows, stride=SUBLANES), :] = (
                h2[:, s * LANES:(s + 1) * LANES])
        hi = h2.astype(BF16)
        lo = (h2 - hi.astype(F32)).astype(BF16)
        logits = (jnp.dot(hi, rh_ref[...], preferred_element_type=F32)
                  + jnp.dot(lo, rh_ref[...], preferred_element_type=F32)
                  + jnp.dot(hi, rl_ref[...], preferred_element_type=F32))
        lane = lax.broadcasted_iota(jnp.int32, logits.shape, 1).astype(F32)
        logits = jnp.where(lane < n_experts, logits, -jnp.inf)
        m1 = jnp.max(logits, axis=-1, keepdims=True)
        i1 = jnp.min(jnp.where(logits == m1, lane, float(ROUTER_LANES)), axis=-1, keepdims=True)
        rest = jnp.where(lane == i1, -jnp.inf, logits)
        m2 = jnp.max(rest, axis=-1, keepdims=True)
        i2 = jnp.min(jnp.where(rest == m2, lane, float(ROUTER_LANES)), axis=-1, keepdims=True)
        e = jnp.exp(m2 - m1)
        pick0 = jnp.where(lane == i1, 1.0, 0.0)
        pick1 = jnp.where(lane == i2, 1.0, 0.0)
        picks = pick0 + pick1
        before = cnt_ref[...] + jnp.dot(tri_ref[...], picks.astype(BF16),
                                        preferred_element_type=F32)
        rank0 = jnp.sum(before * pick0, axis=-1, keepdims=True)
        rank1 = jnp.sum(before * pick1, axis=-1, keepdims=True)
        cnt_ref[...] = before[n_rows - 1:n_rows, :] + picks[n_rows - 1:n_rows, :]
        rout_ref[rows, :] = jnp.where(
            lane == 0, i1, jnp.where(lane == 1, i2, jnp.where(lane == 2, rank0,
                                                              jnp.where(lane == 3, rank1, 0.0))))
        g0_ref[rows, :] = jnp.broadcast_to(1.0 / (1.0 + e), (n_rows, LANES))
        g1_ref[rows, :] = jnp.broadcast_to(e / (1.0 + e), (n_rows, LANES))

    _postmix_chain(i, refs[:n_x], pu_ref, pp_ref, pn_ref, at_ref, mod_ref, wp_ref,
                   ps_ref, wo_ref, g2_ref, ext_ref, tail, **seq_info)


def _pool_scratch_shape(seq_info, pool_w):
    sub = seq_info["seq"]
    return (TOKEN_TILE // sub, sub + 2 * POOL_HALO, pool_w)


def _postmix_specs(x, n, d, pool_w, na_w, l, cond_row, n_prompt_tiles):
    n_halo_blocks = n // POOL_HALO
    per_tile = TOKEN_TILE // POOL_HALO
    tok = lambda w: pl.BlockSpec((TOKEN_TILE, w), lambda i: (i, 0))
    return _x_specs(x, n_prompt_tiles) + [
        tok(pool_w),
        pl.BlockSpec((POOL_HALO, pool_w), lambda i: (jnp.maximum(i * per_tile - 1, 0), 0)),
        pl.BlockSpec((POOL_HALO, pool_w),
                     lambda i: (jnp.minimum((i + 1) * per_tile, n_halo_blocks - 1), 0)),
        tok(na_w),
        pl.BlockSpec((None, None, N_MOD, d), lambda i: (l, cond_row(i), 0, 0)),
        _resident((len(POOL_WINDOWS), LANES, LANES), lambda i: (0, 0, 0)),
        pl.BlockSpec((1, pool_w), lambda i: (0, 0)),
        _resident((d, d), lambda i: (0, 0)),
        pl.BlockSpec((1, d), lambda i: (0, 0)),
    ]


def _postmix_ffn(x, pu, attn, mod, l, w_pool, pool_scale, w_out, norm_g, wg, wu, wd,
                 seq_info, cond_row):
    xs = _as_tuple(x)
    n, d = pu.shape[0], xs[0].shape[1]
    pool_w = pu.shape[1]
    d_ff = wg.shape[1]
    kernel = functools.partial(_postmix_ffn_kernel, n_x=len(xs), **seq_info)
    return pl.pallas_call(
        kernel,
        grid=(n // TOKEN_TILE,),
        in_specs=_postmix_specs(x, n, d, pool_w, attn.shape[1], l, cond_row,
                                seq_info["n_prompt_tiles"]) + [
            _resident((d, d_ff), lambda i: (0, 0)),
            _resident((d, d_ff), lambda i: (0, 0)),
            _resident((d_ff, d), lambda i: (0, 0)),
        ],
        out_specs=pl.BlockSpec((TOKEN_TILE, d), lambda i: (i, 0)),
        out_shape=jax.ShapeDtypeStruct((n, d), F32),
        scratch_shapes=[
            pltpu.VMEM(_pool_scratch_shape(seq_info, pool_w), F32),
            pltpu.VMEM((TOKEN_TILE, d), BF16),
            pltpu.VMEM((TOKEN_TILE, d), F32),
        ],
        compiler_params=_params(),
        name=f"postmix_ffn{l}",
    )(*xs, pu, pu, pu, attn, mod, w_pool, pool_scale, w_out, norm_g, wg, wu, wd)


def _postmix_router(x, pu, attn, mod, l, w_pool, pool_scale, w_out, norm_g, r_hi, r_lo,
                    n_experts, seq_info, cond_row):
    xs = _as_tuple(x)
    sub = seq_info["seq"]
    tri = jnp.asarray(np.tril(np.ones((sub, sub)), -1), BF16)
    n, d = pu.shape[0], xs[0].shape[1]
    pool_w = pu.shape[1]
    kernel = functools.partial(_postmix_router_kernel, n_x=len(xs), n_experts=n_experts,
                               **seq_info)
    row_tile = TOKEN_TILE * d // LANES
    return pl.pallas_call(
        kernel,
        grid=(n // TOKEN_TILE,),
        in_specs=_postmix_specs(x, n, d, pool_w, attn.shape[1], l, cond_row,
                                seq_info["n_prompt_tiles"]) + [
            pl.BlockSpec((d, ROUTER_LANES), lambda i: (0, 0)),
            pl.BlockSpec((d, ROUTER_LANES), lambda i: (0, 0)),
            pl.BlockSpec((sub, sub), lambda i: (0, 0)),
        ],
        out_specs=[
            pl.BlockSpec((TOKEN_TILE, d), lambda i: (i, 0)),
            pl.BlockSpec((row_tile, LANES), lambda i: (i, 0)),
            pl.BlockSpec((TOKEN_TILE, ROUTER_LANES), lambda i: (i, 0)),
            pl.BlockSpec((TOKEN_TILE, LANES), lambda i: (i, 0)),
            pl.BlockSpec((TOKEN_TILE, LANES), lambda i: (i, 0)),
            pl.BlockSpec((1, ROUTER_LANES), lambda i: (0, 0)),
        ],
        out_shape=[
            jax.ShapeDtypeStruct((n, d), F32),
            jax.ShapeDtypeStruct((n * d // LANES, LANES), F32),
            jax.ShapeDtypeStruct((n, ROUTER_LANES), F32),
            jax.ShapeDtypeStruct((n, LANES), F32),
            jax.ShapeDtypeStruct((n, LANES), F32),
            jax.ShapeDtypeStruct((1, ROUTER_LANES), F32),
        ],
        scratch_shapes=[pltpu.VMEM(_pool_scratch_shape(seq_info, pool_w), F32)],
        compiler_params=_params(),
        name=f"postmix_router{l}",
    )(*xs, pu, pu, pu, attn, mod, w_pool, pool_scale, w_out, norm_g, r_hi, r_lo, tri)


def _moe_kernel(te_ref, na_ref, rf_ref, h_hbm, wg_ref, wu_ref, wd_ref, y_hbm,
                xbuf, ybuf, xb16, acc_ref, gsem, ssem, zsem, *,
                cpr, rows_per_step, n_flat, n_dump_tiles):
    j = pl.program_id(0)
    f = pl.program_id(1)
    n_f = pl.num_programs(1)
    tme = MOE_ROW_TILE
    tile_rows = tme * cpr
    n_active = na_ref[0]
    last_token = n_flat // TOP_K - 1

    def gather_row(tile, slot, row):
        v = rf_ref[(tile + 1) * tme + row]
        tok = jnp.minimum(lax.shift_right_logical(v, 1), last_token)
        return pltpu.make_async_copy(
            h_hbm.at[pl.ds(pl.multiple_of(tok * cpr, cpr), cpr)],
            xbuf.at[slot, pl.ds(pl.multiple_of(row * cpr, cpr), cpr)],
            gsem.at[slot])

    def scatter_row(tile, slot, row):
        v = rf_ref[(tile + 1) * tme + row]
        return pltpu.make_async_copy(
            ybuf.at[slot, pl.ds(pl.multiple_of(row * cpr, cpr), cpr)],
            y_hbm.at[pl.ds(pl.multiple_of(v * cpr, cpr), cpr)],
            ssem.at[slot])

    def wait_gather(slot):
        pltpu.make_async_copy(h_hbm.at[pl.ds(0, tile_rows)], xbuf.at[slot], gsem.at[slot]).wait()

    def wait_scatter(slot):
        pltpu.make_async_copy(ybuf.at[slot], y_hbm.at[pl.ds(0, tile_rows)], ssem.at[slot]).wait()

    def dump_fill(t):
        return pltpu.make_async_copy(
            ybuf.at[1], y_hbm.at[pl.ds((n_flat + t * tme) * cpr, tile_rows)], zsem)

    @pl.when((f == 0) & (j == 0))
    def _():
        ybuf[1] = jnp.zeros(ybuf.shape[1:], ybuf.dtype)
        for t in range(n_dump_tiles):
            dump_fill(t).start()
        for t in range(n_dump_tiles):
            dump_fill(t).wait()

        def body(r, carry):
            gather_row(0, 0, r).start()
            return carry

        lax.fori_loop(0, tme, body, 0)

    for par in range(2):
        @pl.when((j < n_active) & (j % 2 == par))
        def _(par=par):
            @pl.when(f == 0)
            def _():
                wait_gather(par)
                for s in range(cpr):
                    xb16[:, s * LANES:(s + 1) * LANES] = (
                        xbuf[par, pl.ds(s, tme, stride=cpr), :].astype(BF16))
                acc_ref[...] = jnp.zeros_like(acc_ref)

            for t in range(rows_per_step):
                gather_row(j + 1, 1 - par, t * n_f + f).start()
                scatter_row(j - 1, 1 - par, t * n_f + f).start()
            x = xb16[...]
            a = jnp.dot(x, wg_ref[...].astype(BF16), preferred_element_type=F32)
            b = jnp.dot(x, wu_ref[...].astype(BF16), preferred_element_type=F32)
            act = (a * jax.nn.sigmoid(a) * b).astype(BF16)
            acc_ref[...] += jnp.dot(act, wd_ref[...].astype(BF16), preferred_element_type=F32)

            @pl.when(f == n_f - 1)
            def _():
                @pl.when(j >= 1)
                def _():
                    wait_scatter(par)

                for s in range(cpr):
                    ybuf[par, pl.ds(s, tme, stride=cpr), :] = acc_ref[:, s * LANES:(s + 1) * LANES]

        @pl.when((j == n_active) & (f == 0) & (j % 2 == par))
        def _(par=par):
            def body(r, carry):
                scatter_row(j - 1, 1 - par, r).start()
                return carry

            lax.fori_loop(0, tme, body, 0)
            wait_gather(par)
            wait_scatter(par)
            wait_scatter(1 - par)


def _moe(tile_expert, n_active, row_flat, h_rows, wg, wu, wd, d, n_flat, n_dump_tiles):
    n_steps = tile_expert.shape[0]
    d_exp = wg.shape[2]
    n_f = d_exp // MOE_F_CHUNK
    cpr = d // LANES
    tile_rows = MOE_ROW_TILE * cpr
    assert MOE_ROW_TILE % n_f == 0

    def f_idx(j, f, na):
        return jnp.where(j < na[0], f, n_f - 1)

    grid_spec = pltpu.PrefetchScalarGridSpec(
        num_scalar_prefetch=3,
        grid=(n_steps, n_f),
        in_specs=[
            pl.BlockSpec(memory_space=pl.ANY),
            pl.BlockSpec((None, d, MOE_F_CHUNK), lambda j, f, te, na, rf: (te[j], 0, f_idx(j, f, na))),
            pl.BlockSpec((None, d, MOE_F_CHUNK), lambda j, f, te, na, rf: (te[j], 0, f_idx(j, f, na))),
            pl.BlockSpec((None, MOE_F_CHUNK, d), lambda j, f, te, na, rf: (te[j], f_idx(j, f, na), 0)),
        ],
        out_specs=pl.BlockSpec(memory_space=pl.ANY),
        scratch_shapes=[
            pltpu.VMEM((2, tile_rows, LANES), F32),
            pltpu.VMEM((2, tile_rows, LANES), F32),
            pltpu.VMEM((MOE_ROW_TILE, d), BF16),
            pltpu.VMEM((MOE_ROW_TILE, d), F32),
            pltpu.SemaphoreType.DMA((2,)),
            pltpu.SemaphoreType.DMA((2,)),
            pltpu.SemaphoreType.DMA,
        ],
    )
    kernel = functools.partial(_moe_kernel, cpr=cpr, rows_per_step=MOE_ROW_TILE // n_f,
                               n_flat=n_flat, n_dump_tiles=n_dump_tiles)
    return pl.pallas_call(
        kernel,
        grid_spec=grid_spec,
        out_shape=jax.ShapeDtypeStruct(((n_flat + n_dump_tiles * MOE_ROW_TILE) * cpr, LANES), F32),
        compiler_params=_params(2),
        name="moe",
    )(tile_expert, n_active, row_flat, h_rows, wg, wu, wd)


def _combine_kernel(y_ref, x2_ref, g0_ref, g1_ref, mod_ref, *o_refs, cpr, n_prompt_tiles):
    i = pl.program_id(0)
    tm = x2_ref.shape[0]
    gate2 = mod_ref[...][5:6]
    stride = TOP_K * cpr

    def write(o_ref):
        g0 = g0_ref[...]
        g1 = g1_ref[...]
        for s in range(cpr):
            lanes = slice(s * LANES, (s + 1) * LANES)
            ya = y_ref[pl.ds(s, tm, stride=stride), :]
            yb = y_ref[pl.ds(cpr + s, tm, stride=stride), :]
            o_ref[:, lanes] = x2_ref[:, lanes] + gate2[:, lanes] * (g0 * ya + g1 * yb)

    if len(o_refs) == 1:
        write(o_refs[0])
    else:
        pl.when(i < n_prompt_tiles)(lambda: write(o_refs[0]))
        pl.when(i >= n_prompt_tiles)(lambda: write(o_refs[1]))


def _combine(y_rows, x2, g0, g1, mod, l, cond_row, n_prompt, split_output):
    n, d = x2.shape
    cpr = d // LANES
    n_prompt_tiles = n_prompt // TOKEN_TILE
    kernel = functools.partial(_combine_kernel, cpr=cpr, n_prompt_tiles=n_prompt_tiles)
    if split_output:
        out_specs = [
            pl.BlockSpec((TOKEN_TILE, d), lambda i: (jnp.minimum(i, n_prompt_tiles - 1), 0)),
            pl.BlockSpec((TOKEN_TILE, d), lambda i: (jnp.maximum(i - n_prompt_tiles, 0), 0))]
        out_shape = [jax.ShapeDtypeStruct((n_prompt, d), F32),
                     jax.ShapeDtypeStruct((n - n_prompt, d), F32)]
    else:
        out_specs = pl.BlockSpec((TOKEN_TILE, d), lambda i: (i, 0))
        out_shape = jax.ShapeDtypeStruct((n, d), F32)
    return pl.pallas_call(
        kernel,
        grid=(n // TOKEN_TILE,),
        in_specs=[
            pl.BlockSpec((TOKEN_TILE * TOP_K * cpr, LANES), lambda i: (i, 0)),
            pl.BlockSpec((TOKEN_TILE, d), lambda i: (i, 0)),
            pl.BlockSpec((TOKEN_TILE, LANES), lambda i: (i, 0)),
            pl.BlockSpec((TOKEN_TILE, LANES), lambda i: (i, 0)),
            pl.BlockSpec((None, None, N_MOD, d), lambda i: (l, cond_row(i), 0, 0)),
        ],
        out_specs=out_specs,
        out_shape=out_shape,
        compiler_params=_params(),
        name="combine",
    )(y_rows, x2, g0, g1, mod)


def _dispatch_plan(expert_ids, ranks, counts, n_tiles_max):
    tme = MOE_ROW_TILE
    n_experts = counts.shape[0]
    n_flat = expert_ids.size
    e_flat = expert_ids.reshape(-1)
    tiles = (counts + tme - 1) // tme
    tile_end = jnp.cumsum(tiles)
    row_off = (tile_end - tiles) * tme
    off_flat = jnp.sum(jnp.where(e_flat[:, None] == jnp.arange(n_experts)[None, :],
                                 row_off[None, :], 0), axis=1)
    pos = tme + off_flat + ranks.reshape(-1)
    n_active = tile_end[-1]
    n_steps = n_tiles_max + 1
    tile_ids = jnp.minimum(jnp.arange(n_steps), n_active - 1)
    tile_expert = jnp.sum(tile_ids[:, None] >= tile_end[None, :], axis=1).astype(jnp.int32)
    n_dump_tiles = n_experts + 1
    rf = jnp.full(((n_steps + 1) * tme,), -1, jnp.int32).at[pos].set(
        jnp.arange(n_flat, dtype=jnp.int32), unique_indices=True)
    is_pad = rf < 0
    pad_slot = jnp.minimum(jnp.cumsum(is_pad.astype(jnp.int32)) - 1, n_dump_tiles * tme - 1)
    rf = jnp.where(is_pad, n_flat + pad_slot, rf)
    return rf, tile_expert, n_active.astype(jnp.int32).reshape(1), n_dump_tiles


def kernel(x_prompt, x_sample, cache_k, cache_v, c, c_ctx, norm1_g, norm2_g, w_ada, b_ada, w_in,
           q_norm_g, k_norm_g, w_pool, pool_scale, rpb, w_out, ffn_w_gate, ffn_w_up, ffn_w_down,
           moe_router, moe_w_gate, moe_w_up, moe_w_down):
    batch, seq, d = x_prompt.shape
    dec_batch, dec_seq, _ = x_sample.shape
    depth = w_in.shape[0]
    heads = cache_k.shape[3]
    na_w = heads * HEAD_DIM
    pool_w = d - na_w
    n_prompt = batch * seq
    n = n_prompt + dec_batch * dec_seq
    n_experts = moe_router.shape[2]
    assert pool_w == len(POOL_WINDOWS) * LANES and na_w % LANES == 0
    assert TOKEN_TILE % seq == 0 and n_prompt % dec_seq == 0 and dec_seq % TOKEN_TILE == 0
    assert dec_seq // GRID_W >= NA_WIN_R and dec_batch < COND_ROWS
    n_prompt_tiles = n_prompt // TOKEN_TILE
    seq_info = dict(n_prompt_tiles=n_prompt_tiles, seq=seq, dec_seq=dec_seq)

    def cond_row(i):
        start = i * TOKEN_TILE
        return jnp.where(start < n_prompt, dec_batch, (start - n_prompt) // dec_seq)

    cond = jnp.zeros((COND_ROWS, d), F32).at[:dec_batch].set(c).at[dec_batch].set(c_ctx)
    mod = _ada(cond, w_ada, b_ada).reshape(depth, COND_ROWS, N_MOD, d)

    hsum = jnp.asarray(np.kron(np.eye(heads), np.ones((HEAD_DIM, HEAD_DIM))), BF16)
    ctx_k = cache_k.reshape(dec_batch, depth, cache_k.shape[2] * heads, HEAD_DIM)
    ctx_v = cache_v.reshape(dec_batch, depth, cache_v.shape[2] * heads, HEAD_DIM)

    x = (x_prompt.reshape(n_prompt, d), x_sample.reshape(-1, d))
    cache_kv = (jnp.zeros((batch, depth, seq * heads, HEAD_DIM), F32),) * 2
    for l in range(depth):
        pu, q, k, v, *cache_kv = _premix(
            x, mod, l, norm1_g[l][None], w_in[l].astype(BF16),
            jnp.tile(q_norm_g[l], heads)[None], jnp.tile(k_norm_g[l], heads)[None], hsum,
            n_prompt, cond_row, cache_kv, seq)
        attn = _attention(q, k, v, ctx_k, ctx_v, _relative_bias_table(rpb[l]), l,
                          n_prompt, seq, dec_seq)
        mix_args = (mod, l, w_pool[l].astype(BF16), pool_scale[l][None], w_out[l].astype(BF16),
                    norm2_g[l][None])
        li = l // 2
        if l % 2 == 0:
            assert ffn_w_gate.shape[2] % FF_CHUNK == 0
            x = _postmix_ffn(x, pu, attn, *mix_args, ffn_w_gate[li].astype(BF16),
                             ffn_w_up[li].astype(BF16), ffn_w_down[li].astype(BF16),
                             seq_info, cond_row)
        else:
            router = jnp.zeros((d, ROUTER_LANES), F32).at[:, :n_experts].set(moe_router[li])
            r_hi = router.astype(BF16)
            r_lo = (router - r_hi.astype(F32)).astype(BF16)
            x2, h_rows, rout, g0, g1, counts = _postmix_router(
                x, pu, attn, *mix_args, r_hi, r_lo, n_experts, seq_info, cond_row)
            routing = rout[:, :2 * TOP_K].astype(jnp.int32)
            n_tiles_max = (TOP_K * n + n_experts * (MOE_ROW_TILE - 1)) // MOE_ROW_TILE
            row_flat, tile_expert, n_active, n_dump_tiles = _dispatch_plan(
                routing[:, :TOP_K], routing[:, TOP_K:], counts[0, :n_experts].astype(jnp.int32),
                n_tiles_max)
            y_rows = _moe(tile_expert, n_active, row_flat, h_rows,
                          moe_w_gate[li], moe_w_up[li], moe_w_down[li], d, TOP_K * n, n_dump_tiles)
            x = _combine(y_rows, x2, g0, g1, mod, l, cond_row, n_prompt,
                         split_output=(l == depth - 1))

    if not isinstance(x, tuple):
        x = (x[:n_prompt], x[n_prompt:])
    y_prompt = x[0].reshape(batch, seq, d)
    y_sample = x[1].reshape(dec_batch, dec_seq, d)
    new_k, new_v = (a.reshape(batch, depth, seq, heads, HEAD_DIM) for a in cache_kv)
    return (y_prompt, y_sample, new_k, new_v)
```

```python
import functools

import numpy as np
import jax
import jax.numpy as jnp
from jax import lax
from jax.experimental import pallas as pl
from jax.experimental.pallas import tpu as pltpu

F32 = jnp.float32
BF16 = jnp.bfloat16

GRID_W = 64
POOL_WINDOWS = (2, 4, 8, 16)
HEAD_DIM = 64
NA_WIN_R = 8
NA_WIN_C = 16
N_MOD = 6
TOP_K = 2
NORM_EPS = 1e-6
LOG2_E = 1.4426950408889634

LANES = 128
SUBLANES = 8
VMEM_LIMIT_BYTES = 56 * 1024 * 1024

TOKEN_TILE = 512
POOL_HALO = 8
FF_CHUNK = 256
MOE_ROW_TILE = 1008
MOE_F_CHUNK = 512
ADA_COL_TILE = 1024
COND_ROWS = 16
ROUTER_LANES = 128


def _params(n_axes=1):
    return pltpu.CompilerParams(
        dimension_semantics=("arbitrary",) * n_axes,
        vmem_limit_bytes=VMEM_LIMIT_BYTES,
    )


def _resident(shape, index_map):
    return pl.BlockSpec(shape, index_map, pipeline_mode=pl.Buffered(1))


def _x_specs(x, n_prompt_tiles):
    if not isinstance(x, tuple):
        return [pl.BlockSpec((TOKEN_TILE, x.shape[1]), lambda i: (i, 0))]
    d = x[0].shape[1]
    return [pl.BlockSpec((TOKEN_TILE, d), lambda i: (jnp.minimum(i, n_prompt_tiles - 1), 0)),
            pl.BlockSpec((TOKEN_TILE, d), lambda i: (jnp.maximum(i - n_prompt_tiles, 0), 0))]


def _load_x(i, x_refs, n_prompt_tiles):
    if len(x_refs) == 1:
        return x_refs[0][...]
    return jnp.where(i < n_prompt_tiles, x_refs[0][...], x_refs[1][...])


def _as_tuple(x):
    return x if isinstance(x, tuple) else (x,)


def _ada_kernel(c_ref, w_ref, b_ref, o_ref):
    c = c_ref[...]
    s = c * jax.nn.sigmoid(c)
    o_ref[...] = jnp.dot(s.astype(BF16), w_ref[...].astype(BF16),
                         preferred_element_type=F32) + b_ref[...]


def _ada(cond, w_ada, b_ada):
    depth, d, width = w_ada.shape
    return pl.pallas_call(
        _ada_kernel,
        grid=(depth, width // ADA_COL_TILE),
        in_specs=[
            pl.BlockSpec((COND_ROWS, d), lambda l, j: (0, 0)),
            pl.BlockSpec((None, d, ADA_COL_TILE), lambda l, j: (l, 0, j)),
            pl.BlockSpec((None, 1, ADA_COL_TILE), lambda l, j: (l, 0, j)),
        ],
        out_specs=pl.BlockSpec((None, COND_ROWS, ADA_COL_TILE), lambda l, j: (l, 0, j)),
        out_shape=jax.ShapeDtypeStruct((depth, COND_ROWS, width), F32),
        compiler_params=_params(2),
        name="ada",
    )(cond, w_ada, b_ada.reshape(depth, 1, width))


def _premix_kernel(*refs, n_x, n_prompt_tiles, pool_w, na_w, seq, layer, creates_cache):
    n_in = n_x + (6 if creates_cache else 8)
    mod_ref, g_ref, w_ref, qg_ref, kg_ref, hsum_ref = refs[n_x:n_x + 6]
    pu_ref, q_ref, k_ref, v_ref, kf_ref, vf_ref = refs[n_in:]
    i = pl.program_id(0)
    x = _load_x(i, refs[:n_x], n_prompt_tiles)
    inv = lax.rsqrt(jnp.mean(x * x, axis=-1, keepdims=True) + NORM_EPS)
    mod = mod_ref[...]
    h = (x * inv * g_ref[...]) * (1.0 + mod[1:2]) + mod[0:1]
    u = jnp.dot(h.astype(BF16), w_ref[...], preferred_element_type=F32)
    pu_ref[...] = u[:, :pool_w]
    q = u[:, pool_w:pool_w + na_w]
    k = u[:, pool_w + na_w:pool_w + 2 * na_w]
    v = u[:, pool_w + 2 * na_w:]

    def head_norm(t, g):
        ms = jnp.dot((t * t).astype(BF16), hsum_ref[...],
                     preferred_element_type=F32) * (1.0 / HEAD_DIM)
        return t * lax.rsqrt(ms + NORM_EPS) * g

    qn = head_norm(q, qg_ref[...])
    kn = head_norm(k, kg_ref[...])
    q_ref[...] = (qn * (HEAD_DIM ** -0.5 * LOG2_E)).astype(BF16)
    k_ref[...] = kn.astype(BF16)
    v_ref[...] = v.astype(BF16)

    @pl.when(i < n_prompt_tiles)
    def _():
        heads = na_w // HEAD_DIM
        if creates_cache:
            for lz in range(kf_ref.shape[1]):
                if lz != layer:
                    kf_ref[:, lz] = jnp.zeros((kf_ref.shape[0],) + kf_ref.shape[2:], F32)
                    vf_ref[:, lz] = jnp.zeros((vf_ref.shape[0],) + vf_ref.shape[2:], F32)
            kf_l, vf_l = kf_ref.at[:, layer], vf_ref.at[:, layer]
        else:
            kf_l, vf_l = kf_ref, vf_ref
        for b in range(TOKEN_TILE // seq):
            for hd in range(heads):
                rows, cols = slice(b * seq, (b + 1) * seq), slice(hd * HEAD_DIM, (hd + 1) * HEAD_DIM)
                kf_l[b, pl.ds(hd, seq, stride=heads), :] = kn[rows, cols]
                vf_l[b, pl.ds(hd, seq, stride=heads), :] = v[rows, cols]


def _premix(x, mod, l, norm_g, w_in, q_g, k_g, hsum, n_prompt, cond_row, cache_kv, cache_shape,
            seq):
    xs = _as_tuple(x)
    creates_cache = cache_kv is None
    n, d = sum(a.shape[0] for a in xs), xs[0].shape[1]
    pool_w = d // 2
    na_w = d - pool_w
    n_tiles = n // TOKEN_TILE
    n_prompt_tiles = n_prompt // TOKEN_TILE
    last_p = n_prompt_tiles - 1
    tok = lambda w: pl.BlockSpec((TOKEN_TILE, w), lambda i: (i, 0))
    if creates_cache:
        cache_spec = pl.BlockSpec((TOKEN_TILE // seq,) + cache_shape[1:],
                                  lambda i: (jnp.minimum(i, last_p), 0, 0, 0))
        cache_in, cache_in_specs, aliases = (), [], {}
    else:
        cache_spec = pl.BlockSpec((TOKEN_TILE // seq, None) + cache_shape[2:],
                                  lambda i: (jnp.minimum(i, last_p), l, 0, 0))
        cache_in, cache_in_specs = tuple(cache_kv), [pl.BlockSpec(memory_space=pl.ANY)] * 2
        aliases = {len(xs) + 6: 4, len(xs) + 7: 5}
    kernel = functools.partial(_premix_kernel, n_x=len(xs), n_prompt_tiles=n_prompt_tiles,
                               pool_w=pool_w, na_w=na_w, seq=seq, layer=l,
                               creates_cache=creates_cache)
    return pl.pallas_call(
        kernel,
        grid=(n_tiles,),
        in_specs=_x_specs(x, n_prompt_tiles) + [
            pl.BlockSpec((None, None, N_MOD, d), lambda i: (l, cond_row(i), 0, 0)),
            pl.BlockSpec((1, d), lambda i: (0, 0)),
            _resident((d, w_in.shape[1]), lambda i: (0, 0)),
            pl.BlockSpec((1, na_w), lambda i: (0, 0)),
            pl.BlockSpec((1, na_w), lambda i: (0, 0)),
            _resident((na_w, na_w), lambda i: (0, 0)),
        ] + cache_in_specs,
        out_specs=[tok(pool_w), tok(na_w), tok(na_w), tok(na_w), cache_spec, cache_spec],
        out_shape=[
            jax.ShapeDtypeStruct((n, pool_w), F32),
            jax.ShapeDtypeStruct((n, na_w), BF16),
            jax.ShapeDtypeStruct((n, na_w), BF16),
            jax.ShapeDtypeStruct((n, na_w), BF16),
            jax.ShapeDtypeStruct(cache_shape, F32),
            jax.ShapeDtypeStruct(cache_shape, F32),
        ],
        input_output_aliases=aliases,
        compiler_params=_params(),
        name=f"premix{l}",
    )(*xs, mod, norm_g, w_in, q_g, k_g, hsum, *cache_in)


_NT = (((1,), (1,)), ((), ()))
_TN = (((0,), (0,)), ((), ()))


def _block_diag_queries(q2):
    lo = lax.broadcasted_iota(jnp.int32, q2.shape, 1) < HEAD_DIM
    zero = jnp.zeros_like(q2)
    return jnp.concatenate([jnp.where(lo, q2, zero), jnp.where(lo, zero, q2)], axis=0)


def _pick_head_blocks(o, nq):
    lo = lax.broadcasted_iota(jnp.int32, (nq, LANES), 1) < HEAD_DIM
    return jnp.where(lo, o[:nq], o[nq:])


def _pair_attention(q2, k, v):
    nq = q2.shape[0]
    s = lax.dot_general(k, _block_diag_queries(q2), _NT, preferred_element_type=F32)
    p = jnp.exp2(s - jnp.max(s, axis=0, keepdims=True))
    r = 1.0 / jnp.sum(p, axis=0, keepdims=True)
    o = lax.dot_general((p * r).astype(BF16), v, _TN, preferred_element_type=F32)
    return _pick_head_blocks(o, nq)


def _attn_kernel(q_ref, k_ref, v_ref, ckf_ref, cvf_ref, bias_ref, o_ref, s_ref, p_ref,
                 ck_ref, cv_ref, *, n_prompt_tiles, seq, dec_seq, n_pairs):
    i = pl.program_id(0)
    tiles_per_seq = dec_seq // TOKEN_TILE
    rows_per_tile = TOKEN_TILE // GRID_W
    rows = dec_seq // GRID_W
    win_keys = NA_WIN_R * GRID_W
    block_off = (i * TOKEN_TILE) % dec_seq

    @pl.when(i < n_prompt_tiles)
    def _():
        for s in range(TOKEN_TILE // seq):
            start = pl.multiple_of(block_off + s * seq, seq)
            for hp in range(n_pairs):
                lanes = slice(hp * LANES, (hp + 1) * LANES)
                out = _pair_attention(q_ref[s * seq:(s + 1) * seq, lanes],
                                      k_ref[pl.ds(start, seq), lanes],
                                      v_ref[pl.ds(start, seq), lanes])
                o_ref[s * seq:(s + 1) * seq, lanes] = out.astype(o_ref.dtype)

    @pl.when(i >= n_prompt_tiles)
    def _():
        tile_in_seq = (i - n_prompt_tiles) % tiles_per_seq
        row0 = tile_in_seq * rows_per_tile

        @pl.when(tile_in_seq == 0)
        def _():
            past = ck_ref.shape[0]
            heads = 2 * n_pairs
            for src, dst in ((ckf_ref, ck_ref), (cvf_ref, cv_ref)):
                for hp in range(n_pairs):
                    pair = [src[pl.ds(2 * hp + t, past, stride=heads), :] for t in range(2)]
                    dst[:, hp * LANES:(hp + 1) * LANES] = (
                        jnp.concatenate(pair, axis=-1).astype(BF16))

        def indices(rl):
            r = row0 + rl
            r0 = jnp.clip(r - NA_WIN_R // 2, 0, rows - NA_WIN_R)
            return r0 - r + NA_WIN_R - 1, pl.multiple_of(r0 * GRID_W, GRID_W)

        def scores(rl):
            d0, kstart = indices(rl)
            for hp in range(n_pairs):
                lanes = slice(hp * LANES, (hp + 1) * LANES)
                qbd = _block_diag_queries(q_ref[rl * GRID_W:(rl + 1) * GRID_W, lanes])
                bias = bias_ref[hp, pl.ds(d0, NA_WIN_R)].reshape(win_keys, LANES)
                s_ref[rl % 2, hp, 0:win_keys] = lax.dot_general(
                    k_ref[pl.ds(kstart, win_keys), lanes], qbd, _NT,
                    preferred_element_type=F32) + bias
                s_ref[rl % 2, hp, win_keys:] = lax.dot_general(
                    ck_ref[:, lanes], qbd, _NT, preferred_element_type=F32)

        def softmax(rl):
            for hp in range(n_pairs):
                s = s_ref[rl % 2, hp]
                p = jnp.exp2(s - jnp.max(s, axis=0, keepdims=True))
                rr = 1.0 / jnp.sum(p, axis=0, keepdims=True)
                p_ref[rl % 2, hp] = (p * rr).astype(BF16)

        def values(rl):
            _, kstart = indices(rl)
            for hp in range(n_pairs):
                lanes = slice(hp * LANES, (hp + 1) * LANES)
                o = (lax.dot_general(p_ref[rl % 2, hp, 0:win_keys],
                                     v_ref[pl.ds(kstart, win_keys), lanes],
                                     _TN, preferred_element_type=F32)
                     + lax.dot_general(p_ref[rl % 2, hp, win_keys:], cv_ref[:, lanes], _TN,
                                       preferred_element_type=F32))
                o_ref[rl * GRID_W:(rl + 1) * GRID_W, lanes] = (
                    _pick_head_blocks(o, GRID_W).astype(o_ref.dtype))

        for step in range(rows_per_tile + 2):
            if step < rows_per_tile:
                scores(step)
            if 1 <= step <= rows_per_tile:
                softmax(step - 1)
            if step >= 2:
                values(step - 2)


def _attention(q, k, v, ctx_k, ctx_v, bias, l, n_prompt, seq, dec_seq):
    n, na_w = q.shape
    n_tiles = n // TOKEN_TILE
    n_prompt_tiles = n_prompt // TOKEN_TILE
    tiles_per_seq = dec_seq // TOKEN_TILE
    heads = na_w // HEAD_DIM
    past = ctx_k.shape[2] // heads
    n_pairs = na_w // LANES
    n_keys = NA_WIN_R * GRID_W + past
    kv_spec = pl.BlockSpec((dec_seq, na_w), lambda i: (i * TOKEN_TILE // dec_seq, 0))
    ctx_spec = pl.BlockSpec(
        (None, None, past * heads, HEAD_DIM),
        lambda i: (jnp.maximum(i - n_prompt_tiles, 0) // tiles_per_seq, l, 0, 0))
    kernel = functools.partial(_attn_kernel, n_prompt_tiles=n_prompt_tiles, seq=seq,
                               dec_seq=dec_seq, n_pairs=n_pairs)
    return pl.pallas_call(
        kernel,
        grid=(n_tiles,),
        in_specs=[
            pl.BlockSpec((TOKEN_TILE, na_w), lambda i: (i, 0)),
            kv_spec, kv_spec, ctx_spec, ctx_spec,
            _resident(bias.shape, lambda i: (0, 0, 0, 0)),
        ],
        out_specs=pl.BlockSpec((TOKEN_TILE, na_w), lambda i: (i, 0)),
        out_shape=jax.ShapeDtypeStruct((n, na_w), BF16),
        scratch_shapes=[pltpu.VMEM((2, n_pairs, n_keys, LANES), F32),
                        pltpu.VMEM((2, n_pairs, n_keys, LANES), BF16),
                        pltpu.VMEM((past, na_w), BF16),
                        pltpu.VMEM((past, na_w), BF16)],
        compiler_params=_params(),
        name=f"attn{l}",
    )(q, k, v, ctx_k, ctx_v, bias)


def _relative_bias_table(rpb_l):
    heads = rpb_l.shape[0]
    kc = np.arange(GRID_W)[:, None]
    qc = np.arange(GRID_W)[None, :]
    q_start = np.clip(qc - NA_WIN_C // 2, 0, GRID_W - NA_WIN_C)
    valid = (kc >= q_start) & (kc < q_start + NA_WIN_C)
    dc_idx = np.clip(kc - qc, -(NA_WIN_C - 1), NA_WIN_C - 1) + NA_WIN_C - 1
    t = jnp.zeros(rpb_l.shape[:2] + dc_idx.shape, F32)
    for c in range(rpb_l.shape[2]):
        t = jnp.where(dc_idx[None, None] == c, rpb_l[:, :, c, None, None].astype(F32), t)
    t = jnp.where(valid[None, None], t * LOG2_E, -jnp.inf)
    t = t.reshape(heads // 2, 2, 2 * NA_WIN_R - 1, GRID_W, GRID_W)
    return t.transpose(0, 2, 3, 1, 4).reshape(heads // 2, 2 * NA_WIN_R - 1, GRID_W, 2 * GRID_W)


def _mod_static(t, m):
    return t & (m - 1) if m & (m - 1) == 0 else lax.rem(t, m)


def _pool_mix(i, b, pu_ref, pp_ref, pn_ref, ext_ref, wp_ref, ps_ref, *, n_prompt_tiles, seq, dec_seq):
    sub = seq
    h = POOL_HALO
    is_prompt = i < n_prompt_tiles
    seq_len = jnp.where(is_prompt, seq, dec_seq)
    lo, hi = b * sub, (b + 1) * sub
    tok0 = i * TOKEN_TILE + lo
    pos0 = jnp.where(is_prompt, _mod_static(tok0, seq), _mod_static(tok0, dec_seq))
    prev = pp_ref[...] if lo == 0 else pu_ref[lo - h:lo, :]
    nxt = pn_ref[...] if hi == TOKEN_TILE else pu_ref[hi:hi + h, :]
    ext_ref[b, 0:h] = jnp.where(pos0 != 0, prev, 0.0)
    ext_ref[b, h:h + sub] = pu_ref[lo:hi, :]
    ext_ref[b, h + sub:] = jnp.where(pos0 + sub != seq_len, nxt, 0.0)
    left = pos0 + lax.broadcasted_iota(jnp.int32, (sub, LANES), 0)
    right = seq_len - left
    outs = []
    for g, window in enumerate(POOL_WINDOWS):
        half = window // 2
        lanes = slice(g * LANES, (g + 1) * LANES)
        total = ext_ref[b, h - half:h - half + sub, lanes]
        for j in range(1 - half, half):
            total = total + ext_ref[b, h + j:h + j + sub, lanes]
        count = (jnp.minimum(left, half) + jnp.minimum(right, half)).astype(F32)
        pooled = total / count - pu_ref[lo:hi, lanes]
        outs.append(jnp.dot(pooled.astype(BF16), wp_ref[g], preferred_element_type=F32))
    return jnp.concatenate(outs, axis=-1) * ps_ref[...]


def _postmix_chain(i, x_refs, pu_ref, pp_ref, pn_ref, at_ref, mod_ref, wp_ref, ps_ref, wo_ref,
                   g2_ref, ext_ref, tail, **seq_info):
    sub = seq_info["seq"]
    n_sub = TOKEN_TILE // sub
    mod = mod_ref[...]
    is_prompt = i < seq_info["n_prompt_tiles"]

    def rows(b):
        return slice(b * sub, (b + 1) * sub)

    def pool_stage(b, _):
        return _pool_mix(i, b, pu_ref, pp_ref, pn_ref, ext_ref, wp_ref, ps_ref, **seq_info)

    def mix_stage(b, pool_out):
        mixed = jnp.concatenate([pool_out.astype(BF16), at_ref[rows(b), :]], axis=-1)
        mix = jnp.dot(mixed, wo_ref[...], preferred_element_type=F32)
        if len(x_refs) == 1:
            x = x_refs[0][rows(b), :]
        else:
            x = jnp.where(is_prompt, x_refs[0][rows(b), :], x_refs[1][rows(b), :])
        return x + mod[2:3] * mix

    def norm_stage(b, x2):
        inv = lax.rsqrt(jnp.mean(x2 * x2, axis=-1, keepdims=True) + NORM_EPS)
        return x2, (x2 * inv * g2_ref[...]) * (1.0 + mod[4:5]) + mod[3:4]

    def tail_stage(b, x2_h2):
        tail(rows(b), *x2_h2)

    stages = (pool_stage, mix_stage, norm_stage, tail_stage)
    state = [None] * n_sub
    for step in range(n_sub + len(stages) - 1):
        for b in range(n_sub):
            if 0 <= step - b < len(stages):
                state[b] = stages[step - b](b, state[b])
    return mod[5:6]


def _postmix_ffn_kernel(*refs, n_x, **seq_info):
    (pu_ref, pp_ref, pn_ref, at_ref, mod_ref, wp_ref, ps_ref, wo_ref,
     g2_ref, wg_ref, wu_ref, wd_ref, y_ref, ext_ref, h_ref, acc_ref) = refs[n_x:]
    i = pl.program_id(0)

    def tail(rows, x2, h2):
        y_ref[rows, :] = x2
        h_ref[rows, :] = h2.astype(BF16)

    gate2 = _postmix_chain(i, refs[:n_x], pu_ref, pp_ref, pn_ref, at_ref, mod_ref, wp_ref,
                           ps_ref, wo_ref, g2_ref, ext_ref, tail, **seq_info)
    acc_ref[...] = jnp.zeros_like(acc_ref)

    h = h_ref[...]
    for c in range(wg_ref.shape[1] // FF_CHUNK):
        cols = slice(c * FF_CHUNK, (c + 1) * FF_CHUNK)
        a = jnp.dot(h, wg_ref[:, cols], preferred_element_type=F32)
        b = jnp.dot(h, wu_ref[:, cols], preferred_element_type=F32)
        act = (a * jax.nn.sigmoid(a) * b).astype(BF16)
        acc_ref[...] += jnp.dot(act, wd_ref[cols, :], preferred_element_type=F32)
    y_ref[...] = y_ref[...] + gate2 * acc_ref[...]


def _postmix_router_kernel(*refs, n_x, n_experts, **seq_info):
    (pu_ref, pp_ref, pn_ref, at_ref, mod_ref, wp_ref, ps_ref, wo_ref,
     g2_ref, rh_ref, rl_ref, tri_ref, x2_ref, hr_ref, rout_ref, g0_ref, g1_ref, cnt_ref,
     ext_ref) = refs[n_x:]
    i = pl.program_id(0)

    @pl.when(i == 0)
    def _():
        cnt_ref[...] = jnp.zeros_like(cnt_ref)

    def tail(rows, x2, h2):
        x2_ref[rows, :] = x2
        n_rows = rows.stop - rows.start
        for s in range(h2.shape[1] // LANES):
            hr_ref[pl.ds(rows.start * SUBLANES + s, n_rows, stride=SUBLANES), :] = (
                h2[:, s * LANES:(s + 1) * LANES])
        hi = h2.astype(BF16)
        lo = (h2 - hi.astype(F32)).astype(BF16)
        logits = (jnp.dot(hi, rh_ref[...], preferred_element_type=F32)
                  + jnp.dot(lo, rh_ref[...], preferred_element_type=F32)
                  + jnp.dot(hi, rl_ref[...], preferred_element_type=F32))
        lane = lax.broadcasted_iota(jnp.int32, logits.shape, 1).astype(F32)
        logits = jnp.where(lane < n_experts, logits, -jnp.inf)
        m1 = jnp.max(logits, axis=-1, keepdims=True)
        i1 = jnp.min(jnp.where(logits == m1, lane, float(ROUTER_LANES)), axis=-1, keepdims=True)
        rest = jnp.where(lane == i1, -jnp.inf, logits)
        m2 = jnp.max(rest, axis=-1, keepdims=True)
        i2 = jnp.min(jnp.where(rest == m2, lane, float(ROUTER_LANES)), axis=-1, keepdims=True)
        e = jnp.exp(m2 - m1)
        pick0 = jnp.where(lane == i1, 1.0, 0.0)
        pick1 = jnp.where(lane == i2, 1.0, 0.0)
        picks = pick0 + pick1
        before = cnt_ref[...] + jnp.dot(tri_ref[...], picks.astype(BF16),
                                        preferred_element_type=F32)
        rank0 = jnp.sum(before * pick0, axis=-1, keepdims=True)
        rank1 = jnp.sum(before * pick1, axis=-1, keepdims=True)
        cnt_ref[...] = before[n_rows - 1:n_rows, :] + picks[n_rows - 1:n_rows, :]
        rout_ref[rows, :] = jnp.where(
            lane == 0, i1, jnp.where(lane == 1, i2, jnp.where(lane == 2, rank0,
                                                              jnp.where(lane == 3, rank1, 0.0))))
        g0_ref[rows, :] = jnp.broadcast_to(1.0 / (1.0 + e), (n_rows, LANES))
        g1_ref[rows, :] = jnp.broadcast_to(e / (1.0 + e), (n_rows, LANES))

    _postmix_chain(i, refs[:n_x], pu_ref, pp_ref, pn_ref, at_ref, mod_ref, wp_ref,
                   ps_ref, wo_ref, g2_ref, ext_ref, tail, **seq_info)


def _pool_scratch_shape(seq_info, pool_w):
    sub = seq_info["seq"]
    return (TOKEN_TILE // sub, sub + 2 * POOL_HALO, pool_w)


def _postmix_specs(x, n, d, pool_w, na_w, l, cond_row, n_prompt_tiles):
    n_halo_blocks = n // POOL_HALO
    per_tile = TOKEN_TILE // POOL_HALO
    tok = lambda w: pl.BlockSpec((TOKEN_TILE, w), lambda i: (i, 0))
    return _x_specs(x, n_prompt_tiles) + [
        tok(pool_w),
        pl.BlockSpec((POOL_HALO, pool_w), lambda i: (jnp.maximum(i * per_tile - 1, 0), 0)),
        pl.BlockSpec((POOL_HALO, pool_w),
                     lambda i: (jnp.minimum((i + 1) * per_tile, n_halo_blocks - 1), 0)),
        tok(na_w),
        pl.BlockSpec((None, None, N_MOD, d), lambda i: (l, cond_row(i), 0, 0)),
        _resident((len(POOL_WINDOWS), LANES, LANES), lambda i: (0, 0, 0)),
        pl.BlockSpec((1, pool_w), lambda i: (0, 0)),
        _resident((d, d), lambda i: (0, 0)),
        pl.BlockSpec((1, d), lambda i: (0, 0)),
    ]


def _postmix_ffn(x, pu, attn, mod, l, w_pool, pool_scale, w_out, norm_g, wg, wu, wd,
                 seq_info, cond_row):
    xs = _as_tuple(x)
    n, d = pu.shape[0], xs[0].shape[1]
    pool_w = pu.shape[1]
    d_ff = wg.shape[1]
    kernel = functools.partial(_postmix_ffn_kernel, n_x=len(xs), **seq_info)
    return pl.pallas_call(
        kernel,
        grid=(n // TOKEN_TILE,),
        in_specs=_postmix_specs(x, n, d, pool_w, attn.shape[1], l, cond_row,
                                seq_info["n_prompt_tiles"]) + [
            _resident((d, d_ff), lambda i: (0, 0)),
            _resident((d, d_ff), lambda i: (0, 0)),
            _resident((d_ff, d), lambda i: (0, 0)),
        ],
        out_specs=pl.BlockSpec((TOKEN_TILE, d), lambda i: (i, 0)),
        out_shape=jax.ShapeDtypeStruct((n, d), F32),
        scratch_shapes=[
            pltpu.VMEM(_pool_scratch_shape(seq_info, pool_w), F32),
            pltpu.VMEM((TOKEN_TILE, d), BF16),
            pltpu.VMEM((TOKEN_TILE, d), F32),
        ],
        compiler_params=_params(),
        name=f"postmix_ffn{l}",
    )(*xs, pu, pu, pu, attn, mod, w_pool, pool_scale, w_out, norm_g, wg, wu, wd)


def _postmix_router(x, pu, attn, mod, l, w_pool, pool_scale, w_out, norm_g, r_hi, r_lo,
                    n_experts, seq_info, cond_row):
    xs = _as_tuple(x)
    sub = seq_info["seq"]
    tri = jnp.asarray(np.tril(np.ones((sub, sub)), -1), BF16)
    n, d = pu.shape[0], xs[0].shape[1]
    pool_w = pu.shape[1]
    kernel = functools.partial(_postmix_router_kernel, n_x=len(xs), n_experts=n_experts,
                               **seq_info)
    row_tile = TOKEN_TILE * d // LANES
    return pl.pallas_call(
        kernel,
        grid=(n // TOKEN_TILE,),
        in_specs=_postmix_specs(x, n, d, pool_w, attn.shape[1], l, cond_row,
                                seq_info["n_prompt_tiles"]) + [
            pl.BlockSpec((d, ROUTER_LANES), lambda i: (0, 0)),
            pl.BlockSpec((d, ROUTER_LANES), lambda i: (0, 0)),
            pl.BlockSpec((sub, sub), lambda i: (0, 0)),
        ],
        out_specs=[
            pl.BlockSpec((TOKEN_TILE, d), lambda i: (i, 0)),
            pl.BlockSpec((row_tile, LANES), lambda i: (i, 0)),
            pl.BlockSpec((TOKEN_TILE, ROUTER_LANES), lambda i: (i, 0)),
            pl.BlockSpec((TOKEN_TILE, LANES), lambda i: (i, 0)),
            pl.BlockSpec((TOKEN_TILE, LANES), lambda i: (i, 0)),
            pl.BlockSpec((1, ROUTER_LANES), lambda i: (0, 0)),
        ],
        out_shape=[
            jax.ShapeDtypeStruct((n, d), F32),
            jax.ShapeDtypeStruct((n * d // LANES, LANES), F32),
            jax.ShapeDtypeStruct((n, ROUTER_LANES), F32),
            jax.ShapeDtypeStruct((n, LANES), F32),
            jax.ShapeDtypeStruct((n, LANES), F32),
            jax.ShapeDtypeStruct((1, ROUTER_LANES), F32),
        ],
        scratch_shapes=[pltpu.VMEM(_pool_scratch_shape(seq_info, pool_w), F32)],
        compiler_params=_params(),
        name=f"postmix_router{l}",
    )(*xs, pu, pu, pu, attn, mod, w_pool, pool_scale, w_out, norm_g, r_hi, r_lo, tri)


def _moe_kernel(te_ref, na_ref, rf_ref, h_hbm, wg_ref, wu_ref, wd_ref, y_hbm,
                xbuf, ybuf, xb16, acc_ref, gsem, ssem, zsem, *,
                cpr, rows_per_step, n_flat, n_dump_tiles):
    j = pl.program_id(0)
    f = pl.program_id(1)
    n_f = pl.num_programs(1)
    tme = MOE_ROW_TILE
    tile_rows = tme * cpr
    n_active = na_ref[0]
    last_token = n_flat // TOP_K - 1

    def gather_row(tile, slot, row):
        v = rf_ref[(tile + 1) * tme + row]
        tok = jnp.minimum(lax.shift_right_logical(v, 1), last_token)
        return pltpu.make_async_copy(
            h_hbm.at[pl.ds(pl.multiple_of(tok * cpr, cpr), cpr)],
            xbuf.at[slot, pl.ds(pl.multiple_of(row * cpr, cpr), cpr)],
            gsem.at[slot])

    def scatter_row(tile, slot, row):
        v = rf_ref[(tile + 1) * tme + row]
        return pltpu.make_async_copy(
            ybuf.at[slot, pl.ds(pl.multiple_of(row * cpr, cpr), cpr)],
            y_hbm.at[pl.ds(pl.multiple_of(v * cpr, cpr), cpr)],
            ssem.at[slot])

    def wait_gather(slot):
        pltpu.make_async_copy(h_hbm.at[pl.ds(0, tile_rows)], xbuf.at[slot], gsem.at[slot]).wait()

    def wait_scatter(slot):
        pltpu.make_async_copy(ybuf.at[slot], y_hbm.at[pl.ds(0, tile_rows)], ssem.at[slot]).wait()

    def dump_fill(t):
        return pltpu.make_async_copy(
            ybuf.at[1], y_hbm.at[pl.ds((n_flat + t * tme) * cpr, tile_rows)], zsem)

    @pl.when((f == 0) & (j == 0))
    def _():
        ybuf[1] = jnp.zeros(ybuf.shape[1:], ybuf.dtype)
        for t in range(n_dump_tiles):
            dump_fill(t).start()
        for t in range(n_dump_tiles):
            dump_fill(t).wait()

        def body(r, carry):
            gather_row(0, 0, r).start()
            return carry

        lax.fori_loop(0, tme, body, 0)

    for par in range(2):
        @pl.when((j < n_active) & (j % 2 == par))
        def _(par=par):
            @pl.when(f == 0)
            def _():
                wait_gather(par)
                for s in range(cpr):
                    xb16[:, s * LANES:(s + 1) * LANES] = (
                        xbuf[par, pl.ds(s, tme, stride=cpr), :].astype(BF16))
                acc_ref[...] = jnp.zeros_like(acc_ref)

            for t in range(rows_per_step):
                gather_row(j + 1, 1 - par, t * n_f + f).start()
                scatter_row(j - 1, 1 - par, t * n_f + f).start()
            x = xb16[...]
            a = jnp.dot(x, wg_ref[...].astype(BF16), preferred_element_type=F32)
            b = jnp.dot(x, wu_ref[...].astype(BF16), preferred_element_type=F32)
            act = (a * jax.nn.sigmoid(a) * b).astype(BF16)
            acc_ref[...] += jnp.dot(act, wd_ref[...].astype(BF16), preferred_element_type=F32)

            @pl.when(f == n_f - 1)
            def _():
                @pl.when(j >= 1)
                def _():
                    wait_scatter(par)

                for s in range(cpr):
                    ybuf[par, pl.ds(s, tme, stride=cpr), :] = acc_ref[:, s * LANES:(s + 1) * LANES]

        @pl.when((j == n_active) & (f == 0) & (j % 2 == par))
        def _(par=par):
            def body(r, carry):
                scatter_row(j - 1, 1 - par, r).start()
                return carry

            lax.fori_loop(0, tme, body, 0)
            wait_gather(par)
            wait_scatter(par)
            wait_scatter(1 - par)


def _moe(tile_expert, n_active, row_flat, h_rows, wg, wu, wd, d, n_flat, n_dump_tiles):
    n_steps = tile_expert.shape[0]
    d_exp = wg.shape[2]
    n_f = d_exp // MOE_F_CHUNK
    cpr = d // LANES
    tile_rows = MOE_ROW_TILE * cpr
    assert MOE_ROW_TILE % n_f == 0

    def f_idx(j, f, na):
        return jnp.where(j < na[0], f, n_f - 1)

    grid_spec = pltpu.PrefetchScalarGridSpec(
        num_scalar_prefetch=3,
        grid=(n_steps, n_f),
        in_specs=[
            pl.BlockSpec(memory_space=pl.ANY),
            pl.BlockSpec((None, d, MOE_F_CHUNK), lambda j, f, te, na, rf: (te[j], 0, f_idx(j, f, na))),
            pl.BlockSpec((None, d, MOE_F_CHUNK), lambda j, f, te, na, rf: (te[j], 0, f_idx(j, f, na))),
            pl.BlockSpec((None, MOE_F_CHUNK, d), lambda j, f, te, na, rf: (te[j], f_idx(j, f, na), 0)),
        ],
        out_specs=pl.BlockSpec(memory_space=pl.ANY),
        scratch_shapes=[
            pltpu.VMEM((2, tile_rows, LANES), F32),
            pltpu.VMEM((2, tile_rows, LANES), F32),
            pltpu.VMEM((MOE_ROW_TILE, d), BF16),
            pltpu.VMEM((MOE_ROW_TILE, d), F32),
            pltpu.SemaphoreType.DMA((2,)),
            pltpu.SemaphoreType.DMA((2,)),
            pltpu.SemaphoreType.DMA,
        ],
    )
    kernel = functools.partial(_moe_kernel, cpr=cpr, rows_per_step=MOE_ROW_TILE // n_f,
                               n_flat=n_flat, n_dump_tiles=n_dump_tiles)
    return pl.pallas_call(
        kernel,
        grid_spec=grid_spec,
        out_shape=jax.ShapeDtypeStruct(((n_flat + n_dump_tiles * MOE_ROW_TILE) * cpr, LANES), F32),
        compiler_params=_params(2),
        name="moe",
    )(tile_expert, n_active, row_flat, h_rows, wg, wu, wd)


def _combine_kernel(y_ref, x2_ref, g0_ref, g1_ref, mod_ref, *o_refs, cpr, n_prompt_tiles):
    i = pl.program_id(0)
    tm = x2_ref.shape[0]
    gate2 = mod_ref[...][5:6]
    stride = TOP_K * cpr

    def write(o_ref):
        g0 = g0_ref[...]
        g1 = g1_ref[...]
        for s in range(cpr):
            lanes = slice(s * LANES, (s + 1) * LANES)
            ya = y_ref[pl.ds(s, tm, stride=stride), :]
            yb = y_ref[pl.ds(cpr + s, tm, stride=stride), :]
            o_ref[:, lanes] = x2_ref[:, lanes] + gate2[:, lanes] * (g0 * ya + g1 * yb)

    if len(o_refs) == 1:
        write(o_refs[0])
    else:
        pl.when(i < n_prompt_tiles)(lambda: write(o_refs[0]))
        pl.when(i >= n_prompt_tiles)(lambda: write(o_refs[1]))


def _combine(y_rows, x2, g0, g1, mod, l, cond_row, n_prompt, split_output):
    n, d = x2.shape
    cpr = d // LANES
    n_prompt_tiles = n_prompt // TOKEN_TILE
    kernel = functools.partial(_combine_kernel, cpr=cpr, n_prompt_tiles=n_prompt_tiles)
    if split_output:
        out_specs = [
            pl.BlockSpec((TOKEN_TILE, d), lambda i: (jnp.minimum(i, n_prompt_tiles - 1), 0)),
            pl.BlockSpec((TOKEN_TILE, d), lambda i: (jnp.maximum(i - n_prompt_tiles, 0), 0))]
        out_shape = [jax.ShapeDtypeStruct((n_prompt, d), F32),
                     jax.ShapeDtypeStruct((n - n_prompt, d), F32)]
    else:
        out_specs = pl.BlockSpec((TOKEN_TILE, d), lambda i: (i, 0))
        out_shape = jax.ShapeDtypeStruct((n, d), F32)
    return pl.pallas_call(
        kernel,
        grid=(n // TOKEN_TILE,),
        in_specs=[
            pl.BlockSpec((TOKEN_TILE * TOP_K * cpr, LANES), lambda i: (i, 0)),
            pl.BlockSpec((TOKEN_TILE, d), lambda i: (i, 0)),
            pl.BlockSpec((TOKEN_TILE, LANES), lambda i: (i, 0)),
            pl.BlockSpec((TOKEN_TILE, LANES), lambda i: (i, 0)),
            pl.BlockSpec((None, None, N_MOD, d), lambda i: (l, cond_row(i), 0, 0)),
        ],
        out_specs=out_specs,
        out_shape=out_shape,
        compiler_params=_params(),
        name="combine",
    )(y_rows, x2, g0, g1, mod)


def _dispatch_plan(expert_ids, ranks, counts, n_tiles_max):
    tme = MOE_ROW_TILE
    n_experts = counts.shape[0]
    n_flat = expert_ids.size
    e_flat = expert_ids.reshape(-1)
    tiles = (counts + tme - 1) // tme
    tile_end = jnp.cumsum(tiles)
    row_off = (tile_end - tiles) * tme
    off_flat = jnp.sum(jnp.where(e_flat[:, None] == jnp.arange(n_experts)[None, :],
                                 row_off[None, :], 0), axis=1)
    pos = tme + off_flat + ranks.reshape(-1)
    n_active = tile_end[-1]
    n_steps = n_tiles_max + 1
    tile_ids = jnp.minimum(jnp.arange(n_steps), n_active - 1)
    tile_expert = jnp.sum(tile_ids[:, None] >= tile_end[None, :], axis=1).astype(jnp.int32)
    n_dump_tiles = n_experts + 1
    rf = jnp.full(((n_steps + 1) * tme,), -1, jnp.int32).at[pos].set(
        jnp.arange(n_flat, dtype=jnp.int32), unique_indices=True)
    is_pad = rf < 0
    pad_slot = jnp.minimum(jnp.cumsum(is_pad.astype(jnp.int32)) - 1, n_dump_tiles * tme - 1)
    rf = jnp.where(is_pad, n_flat + pad_slot, rf)
    return rf, tile_expert, n_active.astype(jnp.int32).reshape(1), n_dump_tiles


def kernel(x_prompt, x_sample, cache_k, cache_v, c, c_ctx, norm1_g, norm2_g, w_ada, b_ada, w_in,
           q_norm_g, k_norm_g, w_pool, pool_scale, rpb, w_out, ffn_w_gate, ffn_w_up, ffn_w_down,
           moe_router, moe_w_gate, moe_w_up, moe_w_down):
    batch, seq, d = x_prompt.shape
    dec_batch, dec_seq, _ = x_sample.shape
    depth = w_in.shape[0]
    heads = cache_k.shape[3]
    na_w = heads * HEAD_DIM
    pool_w = d - na_w
    n_prompt = batch * seq
    n = n_prompt + dec_batch * dec_seq
    n_experts = moe_router.shape[2]
    assert pool_w == len(POOL_WINDOWS) * LANES and na_w % LANES == 0
    assert TOKEN_TILE % seq == 0 and n_prompt % dec_seq == 0 and dec_seq % TOKEN_TILE == 0
    assert dec_seq // GRID_W >= NA_WIN_R and dec_batch < COND_ROWS
    n_prompt_tiles = n_prompt // TOKEN_TILE
    seq_info = dict(n_prompt_tiles=n_prompt_tiles, seq=seq, dec_seq=dec_seq)

    def cond_row(i):
        start = i * TOKEN_TILE
        return jnp.where(start < n_prompt, dec_batch, (start - n_prompt) // dec_seq)

    cond = jnp.zeros((COND_ROWS, d), F32).at[:dec_batch].set(c).at[dec_batch].set(c_ctx)
    mod = _ada(cond, w_ada, b_ada).reshape(depth, COND_ROWS, N_MOD, d)

    hsum = jnp.asarray(np.kron(np.eye(heads), np.ones((HEAD_DIM, HEAD_DIM))), BF16)
    ctx_k = cache_k.reshape(dec_batch, depth, cache_k.shape[2] * heads, HEAD_DIM)
    ctx_v = cache_v.reshape(dec_batch, depth, cache_v.shape[2] * heads, HEAD_DIM)

    x = (x_prompt.reshape(n_prompt, d), x_sample.reshape(-1, d))
    cache_kv = None
    cache_shape = (batch, depth, seq * heads, HEAD_DIM)
    for l in range(depth):
        pu, q, k, v, *cache_kv = _premix(
            x, mod, l, norm1_g[l][None], w_in[l].astype(BF16),
            jnp.tile(q_norm_g[l], heads)[None], jnp.tile(k_norm_g[l], heads)[None], hsum,
            n_prompt, cond_row, cache_kv, cache_shape, seq)
        attn = _attention(q, k, v, ctx_k, ctx_v, _relative_bias_table(rpb[l]), l,
                          n_prompt, seq, dec_seq)
        mix_args = (mod, l, w_pool[l].astype(BF16), pool_scale[l][None], w_out[l].astype(BF16),
                    norm2_g[l][None])
        li = l // 2
        if l % 2 == 0:
            assert ffn_w_gate.shape[2] % FF_CHUNK == 0
            x = _postmix_ffn(x, pu, attn, *mix_args, ffn_w_gate[li].astype(BF16),
                             ffn_w_up[li].astype(BF16), ffn_w_down[li].astype(BF16),
                             seq_info, cond_row)
        else:
            router = jnp.zeros((d, ROUTER_LANES), F32).at[:, :n_experts].set(moe_router[li])
            r_hi = router.astype(BF16)
            r_lo = (router - r_hi.astype(F32)).astype(BF16)
            x2, h_rows, rout, g0, g1, counts = _postmix_router(
                x, pu, attn, *mix_args, r_hi, r_lo, n_experts, seq_info, cond_row)
            routing = rout[:, :2 * TOP_K].astype(jnp.int32)
            n_tiles_max = (TOP_K * n + n_experts * (MOE_ROW_TILE - 1)) // MOE_ROW_TILE
            row_flat, tile_expert, n_active, n_dump_tiles = _dispatch_plan(
                routing[:, :TOP_K], routing[:, TOP_K:], counts[0, :n_experts].astype(jnp.int32),
                n_tiles_max)
            y_rows = _moe(tile_expert, n_active, row_flat, h_rows,
                          moe_w_gate[li], moe_w_up[li], moe_w_down[li], d, TOP_K * n, n_dump_tiles)
            x = _combine(y_rows, x2, g0, g1, mod, l, cond_row, n_prompt,
                         split_output=(l == depth - 1))

    if not isinstance(x, tuple):
        x = (x[:n_prompt], x[n_prompt:])
    y_prompt = x[0].reshape(batch, seq, d)
    y_sample = x[1].reshape(dec_batch, dec_seq, d)
    new_k, new_v = (a.reshape(batch, depth, seq, heads, HEAD_DIM) for a in cache_kv)
    return (y_prompt, y_sample, new_k, new_v)
```

```python
import functools

import numpy as np
import jax
import jax.numpy as jnp
from jax import lax
from jax.experimental import pallas as pl
from jax.experimental.pallas import tpu as pltpu

F32 = jnp.float32
BF16 = jnp.bfloat16

GRID_W = 64
POOL_WINDOWS = (2, 4, 8, 16)
HEAD_DIM = 64
NA_WIN_R = 8
NA_WIN_C = 16
N_MOD = 6
TOP_K = 2
NORM_EPS = 1e-6
LOG2_E = 1.4426950408889634

LANES = 128
SUBLANES = 8
VMEM_LIMIT_BYTES = 56 * 1024 * 1024

TOKEN_TILE = 512
POOL_HALO = 8
FF_CHUNK = 256
MOE_ROW_TILE = 1008
MOE_F_CHUNK = 512
ADA_COL_TILE = 1024
COND_ROWS = 16
ROUTER_LANES = 128


def _params(n_axes=1):
    return pltpu.CompilerParams(
        dimension_semantics=("arbitrary",) * n_axes,
        vmem_limit_bytes=VMEM_LIMIT_BYTES,
    )


def _resident(shape, index_map):
    return pl.BlockSpec(shape, index_map, pipeline_mode=pl.Buffered(1))


def _x_specs(x, n_prompt_tiles):
    if not isinstance(x, tuple):
        return [pl.BlockSpec((TOKEN_TILE, x.shape[1]), lambda i: (i, 0))]
    d = x[0].shape[1]
    return [pl.BlockSpec((TOKEN_TILE, d), lambda i: (jnp.minimum(i, n_prompt_tiles - 1), 0)),
            pl.BlockSpec((TOKEN_TILE, d), lambda i: (jnp.maximum(i - n_prompt_tiles, 0), 0))]


def _load_x(i, x_refs, n_prompt_tiles):
    if len(x_refs) == 1:
        return x_refs[0][...]
    return jnp.where(i < n_prompt_tiles, x_refs[0][...], x_refs[1][...])


def _as_tuple(x):
    return x if isinstance(x, tuple) else (x,)


def _ada_kernel(c_ref, w_ref, b_ref, o_ref):
    c = c_ref[...]
    s = c * jax.nn.sigmoid(c)
    o_ref[...] = jnp.dot(s.astype(BF16), w_ref[...].astype(BF16),
                         preferred_element_type=F32) + b_ref[...]


def _ada(cond, w_ada, b_ada):
    depth, d, width = w_ada.shape
    return pl.pallas_call(
        _ada_kernel,
        grid=(depth, width // ADA_COL_TILE),
        in_specs=[
            pl.BlockSpec((COND_ROWS, d), lambda l, j: (0, 0)),
            pl.BlockSpec((None, d, ADA_COL_TILE), lambda l, j: (l, 0, j)),
            pl.BlockSpec((None, 1, ADA_COL_TILE), lambda l, j: (l, 0, j)),
        ],
        out_specs=pl.BlockSpec((None, COND_ROWS, ADA_COL_TILE), lambda l, j: (l, 0, j)),
        out_shape=jax.ShapeDtypeStruct((depth, COND_ROWS, width), F32),
        compiler_params=_params(2),
        name="ada",
    )(cond, w_ada, b_ada.reshape(depth, 1, width))


def _premix_kernel(*refs, n_x, n_prompt_tiles, pool_w, na_w, seq, layer, creates_cache):
    n_in = n_x + (6 if creates_cache else 8)
    mod_ref, g_ref, w_ref, qg_ref, kg_ref, hsum_ref = refs[n_x:n_x + 6]
    pu_ref, q_ref, k_ref, v_ref, kf_ref, vf_ref = refs[n_in:]
    i = pl.program_id(0)
    x = _load_x(i, refs[:n_x], n_prompt_tiles)
    inv = lax.rsqrt(jnp.mean(x * x, axis=-1, keepdims=True) + NORM_EPS)
    mod = mod_ref[...]
    h = (x * inv * g_ref[...]) * (1.0 + mod[1:2]) + mod[0:1]
    u = jnp.dot(h.astype(BF16), w_ref[...], preferred_element_type=F32)
    pu_ref[...] = u[:, :pool_w]
    q = u[:, pool_w:pool_w + na_w]
    k = u[:, pool_w + na_w:pool_w + 2 * na_w]
    v = u[:, pool_w + 2 * na_w:]

    def head_norm(t, g):
        ms = jnp.dot((t * t).astype(BF16), hsum_ref[...],
                     preferred_element_type=F32) * (1.0 / HEAD_DIM)
        return t * lax.rsqrt(ms + NORM_EPS) * g

    qn = head_norm(q, qg_ref[...])
    kn = head_norm(k, kg_ref[...])
    q_ref[...] = (qn * (HEAD_DIM ** -0.5 * LOG2_E)).astype(BF16)
    k_ref[...] = kn.astype(BF16)
    v_ref[...] = v.astype(BF16)

    @pl.when(i < n_prompt_tiles)
    def _():
        heads = na_w // HEAD_DIM
        if creates_cache:
            for lz in range(kf_ref.shape[1]):
                if lz != layer:
                    kf_ref[:, lz] = jnp.zeros((kf_ref.shape[0],) + kf_ref.shape[2:], F32)
                    vf_ref[:, lz] = jnp.zeros((vf_ref.shape[0],) + vf_ref.shape[2:], F32)
            kf_l, vf_l = kf_ref.at[:, layer], vf_ref.at[:, layer]
        else:
            kf_l, vf_l = kf_ref, vf_ref
        for b in range(TOKEN_TILE // seq):
            for hd in range(heads):
                rows, cols = slice(b * seq, (b + 1) * seq), slice(hd * HEAD_DIM, (hd + 1) * HEAD_DIM)
                kf_l[b, pl.ds(hd, seq, stride=heads), :] = kn[rows, cols]
                vf_l[b, pl.ds(hd, seq, stride=heads), :] = v[rows, cols]


def _premix(x, mod, l, norm_g, w_in, q_g, k_g, hsum, n_prompt, cond_row, cache_kv, cache_shape,
            seq):
    xs = _as_tuple(x)
    creates_cache = cache_kv is None
    n, d = sum(a.shape[0] for a in xs), xs[0].shape[1]
    pool_w = d // 2
    na_w = d - pool_w
    n_tiles = n // TOKEN_TILE
    n_prompt_tiles = n_prompt // TOKEN_TILE
    last_p = n_prompt_tiles - 1
    tok = lambda w: pl.BlockSpec((TOKEN_TILE, w), lambda i: (i, 0))
    if creates_cache:
        cache_spec = pl.BlockSpec((TOKEN_TILE // seq,) + cache_shape[1:],
                                  lambda i: (jnp.minimum(i, last_p), 0, 0, 0))
        cache_in, cache_in_specs, aliases = (), [], {}
    else:
        cache_spec = pl.BlockSpec((TOKEN_TILE // seq, None) + cache_shape[2:],
                                  lambda i: (jnp.minimum(i, last_p), l, 0, 0))
        cache_in, cache_in_specs = tuple(cache_kv), [pl.BlockSpec(memory_space=pl.ANY)] * 2
        aliases = {len(xs) + 6: 4, len(xs) + 7: 5}
    kernel = functools.partial(_premix_kernel, n_x=len(xs), n_prompt_tiles=n_prompt_tiles,
                               pool_w=pool_w, na_w=na_w, seq=seq, layer=l,
                               creates_cache=creates_cache)
    return pl.pallas_call(
        kernel,
        grid=(n_tiles,),
        in_specs=_x_specs(x, n_prompt_tiles) + [
            pl.BlockSpec((None, None, N_MOD, d), lambda i: (l, cond_row(i), 0, 0)),
            pl.BlockSpec((1, d), lambda i: (0, 0)),
            _resident((d, w_in.shape[1]), lambda i: (0, 0)),
            pl.BlockSpec((1, na_w), lambda i: (0, 0)),
            pl.BlockSpec((1, na_w), lambda i: (0, 0)),
            _resident((na_w, na_w), lambda i: (0, 0)),
        ] + cache_in_specs,
        out_specs=[tok(pool_w), tok(na_w), tok(na_w), tok(na_w), cache_spec, cache_spec],
        out_shape=[
            jax.ShapeDtypeStruct((n, pool_w), F32),
            jax.ShapeDtypeStruct((n, na_w), BF16),
            jax.ShapeDtypeStruct((n, na_w), BF16),
            jax.ShapeDtypeStruct((n, na_w), BF16),
            jax.ShapeDtypeStruct(cache_shape, F32),
            jax.ShapeDtypeStruct(cache_shape, F32),
        ],
        input_output_aliases=aliases,
        compiler_params=_params(),
        name=f"premix{l}",
    )(*xs, mod, norm_g, w_in, q_g, k_g, hsum, *cache_in)


_NT = (((1,), (1,)), ((), ()))
_TN = (((0,), (0,)), ((), ()))


def _block_diag_queries(q2):
    lo = lax.broadcasted_iota(jnp.int32, q2.shape, 1) < HEAD_DIM
    zero = jnp.zeros_like(q2)
    return jnp.concatenate([jnp.where(lo, q2, zero), jnp.where(lo, zero, q2)], axis=0)


def _pick_head_blocks(o, nq):
    lo = lax.broadcasted_iota(jnp.int32, (nq, LANES), 1) < HEAD_DIM
    return jnp.where(lo, o[:nq], o[nq:])


def _pair_attention(q2, k, v):
    nq = q2.shape[0]
    s = lax.dot_general(k, _block_diag_queries(q2), _NT, preferred_element_type=F32)
    p = jnp.exp2(s - jnp.max(s, axis=0, keepdims=True))
    r = 1.0 / jnp.sum(p, axis=0, keepdims=True)
    o = lax.dot_general((p * r).astype(BF16), v, _TN, preferred_element_type=F32)
    return _pick_head_blocks(o, nq)


def _attn_kernel(q_ref, k_ref, v_ref, ckf_ref, cvf_ref, bias_ref, o_ref, s_ref, p_ref,
                 ck_ref, cv_ref, *, n_prompt_tiles, seq, dec_seq, n_pairs):
    i = pl.program_id(0)
    tiles_per_seq = dec_seq // TOKEN_TILE
    rows_per_tile = TOKEN_TILE // GRID_W
    rows = dec_seq // GRID_W
    win_keys = NA_WIN_R * GRID_W
    block_off = (i * TOKEN_TILE) % dec_seq

    @pl.when(i < n_prompt_tiles)
    def _():
        for s in range(TOKEN_TILE // seq):
            start = pl.multiple_of(block_off + s * seq, seq)
            for hp in range(n_pairs):
                lanes = slice(hp * LANES, (hp + 1) * LANES)
                out = _pair_attention(q_ref[s * seq:(s + 1) * seq, lanes],
                                      k_ref[pl.ds(start, seq), lanes],
                                      v_ref[pl.ds(start, seq), lanes])
                o_ref[s * seq:(s + 1) * seq, lanes] = out.astype(o_ref.dtype)

    @pl.when(i >= n_prompt_tiles)
    def _():
        tile_in_seq = (i - n_prompt_tiles) % tiles_per_seq
        row0 = tile_in_seq * rows_per_tile

        @pl.when(tile_in_seq == 0)
        def _():
            past = ck_ref.shape[0]
            heads = 2 * n_pairs
            for src, dst in ((ckf_ref, ck_ref), (cvf_ref, cv_ref)):
                for hp in range(n_pairs):
                    pair = [src[pl.ds(2 * hp + t, past, stride=heads), :] for t in range(2)]
                    dst[:, hp * LANES:(hp + 1) * LANES] = (
                        jnp.concatenate(pair, axis=-1).astype(BF16))

        def indices(rl):
            r = row0 + rl
            r0 = jnp.clip(r - NA_WIN_R // 2, 0, rows - NA_WIN_R)
            return r0 - r + NA_WIN_R - 1, pl.multiple_of(r0 * GRID_W, GRID_W)

        def scores(rl):
            d0, kstart = indices(rl)
            for hp in range(n_pairs):
                lanes = slice(hp * LANES, (hp + 1) * LANES)
                qbd = _block_diag_queries(q_ref[rl * GRID_W:(rl + 1) * GRID_W, lanes])
                bias = bias_ref[hp, pl.ds(d0, NA_WIN_R)].reshape(win_keys, LANES)
                s_ref[rl % 2, hp, 0:win_keys] = lax.dot_general(
                    k_ref[pl.ds(kstart, win_keys), lanes], qbd, _NT,
                    preferred_element_type=F32) + bias
                s_ref[rl % 2, hp, win_keys:] = lax.dot_general(
                    ck_ref[:, lanes], qbd, _NT, preferred_element_type=F32)

        def softmax(rl):
            for hp in range(n_pairs):
                s = s_ref[rl % 2, hp]
                p = jnp.exp2(s - jnp.max(s, axis=0, keepdims=True))
                rr = 1.0 / jnp.sum(p, axis=0, keepdims=True)
                p_ref[rl % 2, hp] = (p * rr).astype(BF16)

        def values(rl):
            _, kstart = indices(rl)
            for hp in range(n_pairs):
                lanes = slice(hp * LANES, (hp + 1) * LANES)
                o = (lax.dot_general(p_ref[rl % 2, hp, 0:win_keys],
                                     v_ref[pl.ds(kstart, win_keys), lanes],
                                     _TN, preferred_element_type=F32)
                     + lax.dot_general(p_ref[rl % 2, hp, win_keys:], cv_ref[:, lanes], _TN,
                                       preferred_element_type=F32))
                o_ref[rl * GRID_W:(rl + 1) * GRID_W, lanes] = (
                    _pick_head_blocks(o, GRID_W).astype(o_ref.dtype))

        for step in range(rows_per_tile + 2):
            if step < rows_per_tile:
                scores(step)
            if 1 <= step <= rows_per_tile:
                softmax(step - 1)
            if step >= 2:
                values(step - 2)


def _attention(q, k, v, ctx_k, ctx_v, bias, l, n_prompt, seq, dec_seq):
    n, na_w = q.shape
    n_tiles = n // TOKEN_TILE
    n_prompt_tiles = n_prompt // TOKEN_TILE
    tiles_per_seq = dec_seq // TOKEN_TILE
    heads = na_w // HEAD_DIM
    past = ctx_k.shape[2] // heads
    n_pairs = na_w // LANES
    n_keys = NA_WIN_R * GRID_W + past
    kv_spec = pl.BlockSpec((dec_seq, na_w), lambda i: (i * TOKEN_TILE // dec_seq, 0))
    ctx_spec = pl.BlockSpec(
        (None, None, past * heads, HEAD_DIM),
        lambda i: (jnp.maximum(i - n_prompt_tiles, 0) // tiles_per_seq, l, 0, 0))
    kernel = functools.partial(_attn_kernel, n_prompt_tiles=n_prompt_tiles, seq=seq,
                               dec_seq=dec_seq, n_pairs=n_pairs)
    return pl.pallas_call(
        kernel,
        grid=(n_tiles,),
        in_specs=[
            pl.BlockSpec((TOKEN_TILE, na_w), lambda i: (i, 0)),
            kv_spec, kv_spec, ctx_spec, ctx_spec,
            _resident(bias.shape, lambda i: (0, 0, 0, 0)),
        ],
        out_specs=pl.BlockSpec((TOKEN_TILE, na_w), lambda i: (i, 0)),
        out_shape=jax.ShapeDtypeStruct((n, na_w), BF16),
        scratch_shapes=[pltpu.VMEM((2, n_pairs, n_keys, LANES), F32),
                        pltpu.VMEM((2, n_pairs, n_keys, LANES), BF16),
                        pltpu.VMEM((past, na_w), BF16),
                        pltpu.VMEM((past, na_w), BF16)],
        compiler_params=_params(),
        name=f"attn{l}",
    )(q, k, v, ctx_k, ctx_v, bias)


def _relative_bias_table(rpb_l):
    heads = rpb_l.shape[0]
    kc = np.arange(GRID_W)[:, None]
    qc = np.arange(GRID_W)[None, :]
    q_start = np.clip(qc - NA_WIN_C // 2, 0, GRID_W - NA_WIN_C)
    valid = (kc >= q_start) & (kc < q_start + NA_WIN_C)
    dc_idx = np.clip(kc - qc, -(NA_WIN_C - 1), NA_WIN_C - 1) + NA_WIN_C - 1
    t = jnp.zeros(rpb_l.shape[:2] + dc_idx.shape, F32)
    for c in range(rpb_l.shape[2]):
        t = jnp.where(dc_idx[None, None] == c, rpb_l[:, :, c, None, None].astype(F32), t)
    t = jnp.where(valid[None, None], t * LOG2_E, -jnp.inf)
    t = t.reshape(heads // 2, 2, 2 * NA_WIN_R - 1, GRID_W, GRID_W)
    return t.transpose(0, 2, 3, 1, 4).reshape(heads // 2, 2 * NA_WIN_R - 1, GRID_W, 2 * GRID_W)


def _mod_static(t, m):
    return t & (m - 1) if m & (m - 1) == 0 else lax.rem(t, m)


def _pool_mix(i, b, pu_ref, pp_ref, pn_ref, ext_ref, wp_ref, ps_ref, *, n_prompt_tiles, seq, dec_seq):
    sub = seq
    h = POOL_HALO
    is_prompt = i < n_prompt_tiles
    seq_len = jnp.where(is_prompt, seq, dec_seq)
    lo, hi = b * sub, (b + 1) * sub
    tok0 = i * TOKEN_TILE + lo
    pos0 = jnp.where(is_prompt, _mod_static(tok0, seq), _mod_static(tok0, dec_seq))
    prev = pp_ref[...] if lo == 0 else pu_ref[lo - h:lo, :]
    nxt = pn_ref[...] if hi == TOKEN_TILE else pu_ref[hi:hi + h, :]
    ext_ref[b, 0:h] = jnp.where(pos0 != 0, prev, 0.0)
    ext_ref[b, h:h + sub] = pu_ref[lo:hi, :]
    ext_ref[b, h + sub:] = jnp.where(pos0 + sub != seq_len, nxt, 0.0)
    left = pos0 + lax.broadcasted_iota(jnp.int32, (sub, LANES), 0)
    right = seq_len - left
    outs = []
    for g, window in enumerate(POOL_WINDOWS):
        half = window // 2
        lanes = slice(g * LANES, (g + 1) * LANES)
        total = ext_ref[b, h - half:h - half + sub, lanes]
        for j in range(1 - half, half):
            total = total + ext_ref[b, h + j:h + j + sub, lanes]
        count = (jnp.minimum(left, half) + jnp.minimum(right, half)).astype(F32)
        pooled = total / count - pu_ref[lo:hi, lanes]
        outs.append(jnp.dot(pooled.astype(BF16), wp_ref[g], preferred_element_type=F32))
    return jnp.concatenate(outs, axis=-1) * ps_ref[...]


def _postmix_chain(i, x_refs, pu_ref, pp_ref, pn_ref, at_ref, mod_ref, wp_ref, ps_ref, wo_ref,
                   g2_ref, ext_ref, tail, **seq_info):
    sub = seq_info["seq"]
    n_sub = TOKEN_TILE // sub
    mod = mod_ref[...]
    is_prompt = i < seq_info["n_prompt_tiles"]

    def rows(b):
        return slice(b * sub, (b + 1) * sub)

    def pool_stage(b, _):
        return _pool_mix(i, b, pu_ref, pp_ref, pn_ref, ext_ref, wp_ref, ps_ref, **seq_info)

    def mix_stage(b, pool_out):
        mixed = jnp.concatenate([pool_out.astype(BF16), at_ref[rows(b), :]], axis=-1)
        mix = jnp.dot(mixed, wo_ref[...], preferred_element_type=F32)
        if len(x_refs) == 1:
            x = x_refs[0][rows(b), :]
        else:
            x = jnp.where(is_prompt, x_refs[0][rows(b), :], x_refs[1][rows(b), :])
        return x + mod[2:3] * mix

    def norm_stage(b, x2):
        inv = lax.rsqrt(jnp.mean(x2 * x2, axis=-1, keepdims=True) + NORM_EPS)
        return x2, (x2 * inv * g2_ref[...]) * (1.0 + mod[4:5]) + mod[3:4]

    def tail_stage(b, x2_h2):
        tail(rows(b), *x2_h2)

    stages = (pool_stage, mix_stage, norm_stage, tail_stage)
    state = [None] * n_sub
    for step in range(n_sub + len(stages) - 1):
        for b in range(n_sub):
            if 0 <= step - b < len(stages):
                state[b] = stages[step - b](b, state[b])
    return mod[5:6]


def _postmix_ffn_kernel(*refs, n_x, **seq_info):
    (pu_ref, pp_ref, pn_ref, at_ref, mod_ref, wp_ref, ps_ref, wo_ref,
     g2_ref, wg_ref, wu_ref, wd_ref, y_ref, ext_ref, h_ref, acc_ref) = refs[n_x:]
    i = pl.program_id(0)

    def tail(rows, x2, h2):
        y_ref[rows, :] = x2
        h_ref[rows, :] = h2.astype(BF16)

    gate2 = _postmix_chain(i, refs[:n_x], pu_ref, pp_ref, pn_ref, at_ref, mod_ref, wp_ref,
                           ps_ref, wo_ref, g2_ref, ext_ref, tail, **seq_info)
    acc_ref[...] = jnp.zeros_like(acc_ref)

    h = h_ref[...]
    for c in range(wg_ref.shape[1] // FF_CHUNK):
        cols = slice(c * FF_CHUNK, (c + 1) * FF_CHUNK)
        a = jnp.dot(h, wg_ref[:, cols], preferred_element_type=F32)
        b = jnp.dot(h, wu_ref[:, cols], preferred_element_type=F32)
        act = (a * jax.nn.sigmoid(a) * b).astype(BF16)
        acc_ref[...] += jnp.dot(act, wd_ref[cols, :], preferred_element_type=F32)
    y_ref[...] = y_ref[...] + gate2 * acc_ref[...]


def _postmix_router_kernel(*refs, n_x, n_experts, **seq_info):
    (pu_ref, pp_ref, pn_ref, at_ref, mod_ref, wp_ref, ps_ref, wo_ref,
     g2_ref, rh_ref, rl_ref, tri_ref, x2_ref, hr_ref, rout_ref, g0_ref, g1_ref, cnt_ref,
     ext_ref) = refs[n_x:]
    i = pl.program_id(0)

    @pl.when(i == 0)
    def _():
        cnt_ref[...] = jnp.zeros_like(cnt_ref)

    def tail(rows, x2, h2):
        x2_ref[rows, :] = x2
        n_rows = rows.stop - rows.start
        for s in range(h2.shape[1] // LANES):
            hr_ref[pl.ds(rows.start * SUBLANES + s, n_rows, stride=SUBLANES), :] = (
                h2[:, s * LANES:(s + 1) * LANES])
        hi = h2.astype(BF16)
        lo = (h2 - hi.astype(F32)).astype(BF16)
        logits = (jnp.dot(hi, rh_ref[...], preferred_element_type=F32)
                  + jnp.dot(lo, rh_ref[...], preferred_element_type=F32)
                  + jnp.dot(hi, rl_ref[...], preferred_element_type=F32))
        lane = lax.broadcasted_iota(jnp.int32, logits.shape, 1).astype(F32)
        logits = jnp.where(lane < n_experts, logits, -jnp.inf)
        m1 = jnp.max(logits, axis=-1, keepdims=True)
        i1 = jnp.min(jnp.where(logits == m1, lane, float(ROUTER_LANES)), axis=-1, keepdims=True)
        rest = jnp.where(lane == i1, -jnp.inf, logits)
        m2 = jnp.max(rest, axis=-1, keepdims=True)
        i2 = jnp.min(jnp.where(rest == m2, lane, float(ROUTER_LANES)), axis=-1, keepdims=True)
        e = jnp.exp(m2 - m1)
        pick0 = jnp.where(lane == i1, 1.0, 0.0)
        pick1 = jnp.where(lane == i2, 1.0, 0.0)
        picks = pick0 + pick1
        before = cnt_ref[...] + jnp.dot(tri_ref[...], picks.astype(BF16),
                                        preferred_element_type=F32)
        rank0 = jnp.sum(before * pick0, axis=-1, keepdims=True)
        rank1 = jnp.sum(before * pick1, axis=-1, keepdims=True)
        cnt_ref[...] = before[n_rows - 1:n_rows, :] + picks[n_rows - 1:n_rows, :]
        rout_ref[rows, :] = jnp.where(
            lane == 0, i1, jnp.where(lane == 1, i2, jnp.where(lane == 2, rank0,
                                                              jnp.where(lane == 3, rank1, 0.0))))
        g0_ref[rows, :] = jnp.broadcast_to(1.0 / (1.0 + e), (n_rows, LANES))
        g1_ref[rows, :] = jnp.broadcast_to(e / (1.0 + e), (n_rows, LANES))

    _postmix_chain(i, refs[:n_x], pu_ref, pp_ref, pn_ref, at_ref, mod_ref, wp_ref,
                   ps_ref, wo_ref, g2_ref, ext_ref, tail, **seq_info)


def _pool_scratch_shape(seq_info, pool_w):
    sub = seq_info["seq"]
    return (TOKEN_TILE // sub, sub + 2 * POOL_HALO, pool_w)


def _postmix_specs(x, n, d, pool_w, na_w, l, cond_row, n_prompt_tiles):
    n_halo_blocks = n // POOL_HALO
    per_tile = TOKEN_TILE // POOL_HALO
    tok = lambda w: pl.BlockSpec((TOKEN_TILE, w), lambda i: (i, 0))
    return _x_specs(x, n_prompt_tiles) + [
        tok(pool_w),
        pl.BlockSpec((POOL_HALO, pool_w), lambda i: (jnp.maximum(i * per_tile - 1, 0), 0)),
        pl.BlockSpec((POOL_HALO, pool_w),
                     lambda i: (jnp.minimum((i + 1) * per_tile, n_halo_blocks - 1), 0)),
        tok(na_w),
        pl.BlockSpec((None, None, N_MOD, d), lambda i: (l, cond_row(i), 0, 0)),
        _resident((len(POOL_WINDOWS), LANES, LANES), lambda i: (0, 0, 0)),
        pl.BlockSpec((1, pool_w), lambda i: (0, 0)),
        _resident((d, d), lambda i: (0, 0)),
        pl.BlockSpec((1, d), lambda i: (0, 0)),
    ]


def _postmix_ffn(x, pu, attn, mod, l, w_pool, pool_scale, w_out, norm_g, wg, wu, wd,
                 seq_info, cond_row):
    xs = _as_tuple(x)
    n, d = pu.shape[0], xs[0].shape[1]
    pool_w = pu.shape[1]
    d_ff = wg.shape[1]
    kernel = functools.partial(_postmix_ffn_kernel, n_x=len(xs), **seq_info)
    return pl.pallas_call(
        kernel,
        grid=(n // TOKEN_TILE,),
        in_specs=_postmix_specs(x, n, d, pool_w, attn.shape[1], l, cond_row,
                                seq_info["n_prompt_tiles"]) + [
            _resident((d, d_ff), lambda i: (0, 0)),
            _resident((d, d_ff), lambda i: (0, 0)),
            _resident((d_ff, d), lambda i: (0, 0)),
        ],
        out_specs=pl.BlockSpec((TOKEN_TILE, d), lambda i: (i, 0)),
        out_shape=jax.ShapeDtypeStruct((n, d), F32),
        scratch_shapes=[
            pltpu.VMEM(_pool_scratch_shape(seq_info, pool_w), F32),
            pltpu.VMEM((TOKEN_TILE, d), BF16),
            pltpu.VMEM((TOKEN_TILE, d), F32),
        ],
        compiler_params=_params(),
        name=f"postmix_ffn{l}",
    )(*xs, pu, pu, pu, attn, mod, w_pool, pool_scale, w_out, norm_g, wg, wu, wd)


def _postmix_router(x, pu, attn, mod, l, w_pool, pool_scale, w_out, norm_g, r_hi, r_lo,
                    n_experts, seq_info, cond_row):
    xs = _as_tuple(x)
    sub = seq_info["seq"]
    tri = jnp.asarray(np.tril(np.ones((sub, sub)), -1), BF16)
    n, d = pu.shape[0], xs[0].shape[1]
    pool_w = pu.shape[1]
    kernel = functools.partial(_postmix_router_kernel, n_x=len(xs), n_experts=n_experts,
                               **seq_info)
    row_tile = TOKEN_TILE * d // LANES
    return pl.pallas_call(
        kernel,
        grid=(n // TOKEN_TILE,),
        in_specs=_postmix_specs(x, n, d, pool_w, attn.shape[1], l, cond_row,
                                seq_info["n_prompt_tiles"]) + [
            pl.BlockSpec((d, ROUTER_LANES), lambda i: (0, 0)),
            pl.BlockSpec((d, ROUTER_LANES), lambda i: (0, 0)),
            pl.BlockSpec((sub, sub), lambda i: (0, 0)),
        ],
        out_specs=[
            pl.BlockSpec((TOKEN_TILE, d), lambda i: (i, 0)),
            pl.BlockSpec((row_tile, LANES), lambda i: (i, 0)),
            pl.BlockSpec((TOKEN_TILE, ROUTER_LANES), lambda i: (i, 0)),
            pl.BlockSpec((TOKEN_TILE, LANES), lambda i: (i, 0)),
            pl.BlockSpec((TOKEN_TILE, LANES), lambda i: (i, 0)),
            pl.BlockSpec((1, ROUTER_LANES), lambda i: (0, 0)),
        ],
        out_shape=[
            jax.ShapeDtypeStruct((n, d), F32),
            jax.ShapeDtypeStruct((n * d // LANES, LANES), F32),
            jax.ShapeDtypeStruct((n, ROUTER_LANES), F32),
            jax.ShapeDtypeStruct((n, LANES), F32),
            jax.ShapeDtypeStruct((n, LANES), F32),
            jax.ShapeDtypeStruct((1, ROUTER_LANES), F32),
        ],
        scratch_shapes=[pltpu.VMEM(_pool_scratch_shape(seq_info, pool_w), F32)],
        compiler_params=_params(),
        name=f"postmix_router{l}",
    )(*xs, pu, pu, pu, attn, mod, w_pool, pool_scale, w_out, norm_g, r_hi, r_lo, tri)


def _moe_kernel(te_ref, na_ref, rf_ref, h_hbm, wg_ref, wu_ref, wd_ref, y_hbm,
                xbuf, ybuf, xb16, acc_ref, gsem, ssem, zsem, *,
                cpr, rows_per_step, n_flat, n_dump_tiles):
    j = pl.program_id(0)
    f = pl.program_id(1)
    n_f = pl.num_programs(1)
    tme = MOE_ROW_TILE
    tile_rows = tme * cpr
    n_active = na_ref[0]
    last_token = n_flat // TOP_K - 1

    def gather_row(tile, slot, row):
        v = rf_ref[(tile + 1) * tme + row]
        tok = jnp.minimum(lax.shift_right_logical(v, 1), last_token)
        return pltpu.make_async_copy(
            h_hbm.at[pl.ds(pl.multiple_of(tok * cpr, cpr), cpr)],
            xbuf.at[slot, pl.ds(pl.multiple_of(row * cpr, cpr), cpr)],
            gsem.at[slot])

    def scatter_row(tile, slot, row):
        v = rf_ref[(tile + 1) * tme + row]
        return pltpu.make_async_copy(
            ybuf.at[slot, pl.ds(pl.multiple_of(row * cpr, cpr), cpr)],
            y_hbm.at[pl.ds(pl.multiple_of(v * cpr, cpr), cpr)],
            ssem.at[slot])

    def wait_gather(slot):
        pltpu.make_async_copy(h_hbm.at[pl.ds(0, tile_rows)], xbuf.at[slot], gsem.at[slot]).wait()

    def wait_scatter(slot):
        pltpu.make_async_copy(ybuf.at[slot], y_hbm.at[pl.ds(0, tile_rows)], ssem.at[slot]).wait()

    def dump_fill(t):
        return pltpu.make_async_copy(
            ybuf.at[1], y_hbm.at[pl.ds((n_flat + t * tme) * cpr, tile_rows)], zsem)

    @pl.when((f == 0) & (j == 0))
    def _():
        ybuf[1] = jnp.zeros(ybuf.shape[1:], ybuf.dtype)
        for t in range(n_dump_tiles):
            dump_fill(t).start()
        for t in range(n_dump_tiles):
            dump_fill(t).wait()

        def body(r, carry):
            gather_row(0, 0, r).start()
            return carry

        lax.fori_loop(0, tme, body, 0)

    def step(par, first, last):
        if first:
            wait_gather(par)
            for s in range(cpr):
                xb16[:, s * LANES:(s + 1) * LANES] = (
                    xbuf[par, pl.ds(s, tme, stride=cpr), :].astype(BF16))
        for t in range(rows_per_step):
            gather_row(j + 1, 1 - par, t * n_f + f).start()
            scatter_row(j - 1, 1 - par, t * n_f + f).start()
        x = xb16[...]
        a = jnp.dot(x, wg_ref[...].astype(BF16), preferred_element_type=F32)
        b = jnp.dot(x, wu_ref[...].astype(BF16), preferred_element_type=F32)
        act = (a * jax.nn.sigmoid(a) * b).astype(BF16)
        part = jnp.dot(act, wd_ref[...].astype(BF16), preferred_element_type=F32)
        total = part if first else acc_ref[...] + part
        if not last:
            acc_ref[...] = total
        else:
            pl.when(j >= 1)(lambda: wait_scatter(par))
            for s in range(cpr):
                ybuf[par, pl.ds(s, tme, stride=cpr), :] = total[:, s * LANES:(s + 1) * LANES]

    for par in range(2):
        on = (j < n_active) & (j % 2 == par)
        pl.when(on & (f == 0))(functools.partial(step, par, True, False))
        pl.when(on & (f > 0) & (f < n_f - 1))(functools.partial(step, par, False, False))
        pl.when(on & (f == n_f - 1))(functools.partial(step, par, False, True))

        @pl.when((j == n_active) & (f == 0) & (j % 2 == par))
        def _(par=par):
            def body(r, carry):
                scatter_row(j - 1, 1 - par, r).start()
                return carry

            lax.fori_loop(0, tme, body, 0)
            wait_gather(par)
            wait_scatter(par)
            wait_scatter(1 - par)


def _moe(tile_expert, n_active, row_flat, h_rows, wg, wu, wd, d, n_flat, n_dump_tiles):
    n_steps = tile_expert.shape[0]
    d_exp = wg.shape[2]
    n_f = d_exp // MOE_F_CHUNK
    cpr = d // LANES
    tile_rows = MOE_ROW_TILE * cpr
    assert MOE_ROW_TILE % n_f == 0 and n_f >= 2

    def f_idx(j, f, na):
        return jnp.where(j < na[0], f, n_f - 1)

    grid_spec = pltpu.PrefetchScalarGridSpec(
        num_scalar_prefetch=3,
        grid=(n_steps, n_f),
        in_specs=[
            pl.BlockSpec(memory_space=pl.ANY),
            pl.BlockSpec((None, d, MOE_F_CHUNK), lambda j, f, te, na, rf: (te[j], 0, f_idx(j, f, na))),
            pl.BlockSpec((None, d, MOE_F_CHUNK), lambda j, f, te, na, rf: (te[j], 0, f_idx(j, f, na))),
            pl.BlockSpec((None, MOE_F_CHUNK, d), lambda j, f, te, na, rf: (te[j], f_idx(j, f, na), 0)),
        ],
        out_specs=pl.BlockSpec(memory_space=pl.ANY),
        scratch_shapes=[
            pltpu.VMEM((2, tile_rows, LANES), F32),
            pltpu.VMEM((2, tile_rows, LANES), F32),
            pltpu.VMEM((MOE_ROW_TILE, d), BF16),
            pltpu.VMEM((MOE_ROW_TILE, d), F32),
            pltpu.SemaphoreType.DMA((2,)),
            pltpu.SemaphoreType.DMA((2,)),
            pltpu.SemaphoreType.DMA,
        ],
    )
    kernel = functools.partial(_moe_kernel, cpr=cpr, rows_per_step=MOE_ROW_TILE // n_f,
                               n_flat=n_flat, n_dump_tiles=n_dump_tiles)
    return pl.pallas_call(
        kernel,
        grid_spec=grid_spec,
        out_shape=jax.ShapeDtypeStruct(((n_flat + n_dump_tiles * MOE_ROW_TILE) * cpr, LANES), F32),
        compiler_params=_params(2),
        name="moe",
    )(tile_expert, n_active, row_flat, h_rows, wg, wu, wd)


def _combine_kernel(y_ref, x2_ref, g0_ref, g1_ref, mod_ref, *o_refs, cpr, n_prompt_tiles):
    i = pl.program_id(0)
    tm = x2_ref.shape[0]
    gate2 = mod_ref[...][5:6]
    stride = TOP_K * cpr

    def write(o_ref):
        g0 = g0_ref[...]
        g1 = g1_ref[...]
        for s in range(cpr):
            lanes = slice(s * LANES, (s + 1) * LANES)
            ya = y_ref[pl.ds(s, tm, stride=stride), :]
            yb = y_ref[pl.ds(cpr + s, tm, stride=stride), :]
            o_ref[:, lanes] = x2_ref[:, lanes] + gate2[:, lanes] * (g0 * ya + g1 * yb)

    if len(o_refs) == 1:
        write(o_refs[0])
    else:
        pl.when(i < n_prompt_tiles)(lambda: write(o_refs[0]))
        pl.when(i >= n_prompt_tiles)(lambda: write(o_refs[1]))


def _combine(y_rows, x2, g0, g1, mod, l, cond_row, n_prompt, split_output):
    n, d = x2.shape
    cpr = d // LANES
    n_prompt_tiles = n_prompt // TOKEN_TILE
    kernel = functools.partial(_combine_kernel, cpr=cpr, n_prompt_tiles=n_prompt_tiles)
    if split_output:
        out_specs = [
            pl.BlockSpec((TOKEN_TILE, d), lambda i: (jnp.minimum(i, n_prompt_tiles - 1), 0)),
            pl.BlockSpec((TOKEN_TILE, d), lambda i: (jnp.maximum(i - n_prompt_tiles, 0), 0))]
        out_shape = [jax.ShapeDtypeStruct((n_prompt, d), F32),
                     jax.ShapeDtypeStruct((n - n_prompt, d), F32)]
    else:
        out_specs = pl.BlockSpec((TOKEN_TILE, d), lambda i: (i, 0))
        out_shape = jax.ShapeDtypeStruct((n, d), F32)
    return pl.pallas_call(
        kernel,
        grid=(n // TOKEN_TILE,),
        in_specs=[
            pl.BlockSpec((TOKEN_TILE * TOP_K * cpr, LANES), lambda i: (i, 0)),
            pl.BlockSpec((TOKEN_TILE, d), lambda i: (i, 0)),
            pl.BlockSpec((TOKEN_TILE, LANES), lambda i: (i, 0)),
            pl.BlockSpec((TOKEN_TILE, LANES), lambda i: (i, 0)),
            pl.BlockSpec((None, None, N_MOD, d), lambda i: (l, cond_row(i), 0, 0)),
        ],
        out_specs=out_specs,
        out_shape=out_shape,
        compiler_params=_params(),
        name="combine",
    )(y_rows, x2, g0, g1, mod)


def _dispatch_plan(expert_ids, ranks, counts, n_tiles_max):
    tme = MOE_ROW_TILE
    n_experts = counts.shape[0]
    n_flat = expert_ids.size
    e_flat = expert_ids.reshape(-1)
    tiles = (counts + tme - 1) // tme
    tile_end = jnp.cumsum(tiles)
    row_off = (tile_end - tiles) * tme
    off_flat = jnp.sum(jnp.where(e_flat[:, None] == jnp.arange(n_experts)[None, :],
                                 row_off[None, :], 0), axis=1)
    pos = tme + off_flat + ranks.reshape(-1)
    n_active = tile_end[-1]
    n_steps = n_tiles_max + 1
    tile_ids = jnp.minimum(jnp.arange(n_steps), n_active - 1)
    tile_expert = jnp.sum(tile_ids[:, None] >= tile_end[None, :], axis=1).astype(jnp.int32)
    n_dump_tiles = n_experts + 1
    rf = jnp.full(((n_steps + 1) * tme,), -1, jnp.int32).at[pos].set(
        jnp.arange(n_flat, dtype=jnp.int32), unique_indices=True)
    is_pad = rf < 0
    pad_slot = jnp.minimum(jnp.cumsum(is_pad.astype(jnp.int32)) - 1, n_dump_tiles * tme - 1)
    rf = jnp.where(is_pad, n_flat + pad_slot, rf)
    return rf, tile_expert, n_active.astype(jnp.int32).reshape(1), n_dump_tiles


def kernel(x_prompt, x_sample, cache_k, cache_v, c, c_ctx, norm1_g, norm2_g, w_ada, b_ada, w_in,
           q_norm_g, k_norm_g, w_pool, pool_scale, rpb, w_out, ffn_w_gate, ffn_w_up, ffn_w_down,
           moe_router, moe_w_gate, moe_w_up, moe_w_down):
    batch, seq, d = x_prompt.shape
    dec_batch, dec_seq, _ = x_sample.shape
    depth = w_in.shape[0]
    heads = cache_k.shape[3]
    na_w = heads * HEAD_DIM
    pool_w = d - na_w
    n_prompt = batch * seq
    n = n_prompt + dec_batch * dec_seq
    n_experts = moe_router.shape[2]
    assert pool_w == len(POOL_WINDOWS) * LANES and na_w % LANES == 0
    assert TOKEN_TILE % seq == 0 and n_prompt % dec_seq == 0 and dec_seq % TOKEN_TILE == 0
    assert dec_seq // GRID_W >= NA_WIN_R and dec_batch < COND_ROWS
    n_prompt_tiles = n_prompt // TOKEN_TILE
    seq_info = dict(n_prompt_tiles=n_prompt_tiles, seq=seq, dec_seq=dec_seq)

    def cond_row(i):
        start = i * TOKEN_TILE
        return jnp.where(start < n_prompt, dec_batch, (start - n_prompt) // dec_seq)

    cond = jnp.zeros((COND_ROWS, d), F32).at[:dec_batch].set(c).at[dec_batch].set(c_ctx)
    mod = _ada(cond, w_ada, b_ada).reshape(depth, COND_ROWS, N_MOD, d)

    hsum = jnp.asarray(np.kron(np.eye(heads), np.ones((HEAD_DIM, HEAD_DIM))), BF16)
    ctx_k = cache_k.reshape(dec_batch, depth, cache_k.shape[2] * heads, HEAD_DIM)
    ctx_v = cache_v.reshape(dec_batch, depth, cache_v.shape[2] * heads, HEAD_DIM)

    x = (x_prompt.reshape(n_prompt, d), x_sample.reshape(-1, d))
    cache_kv = None
    cache_shape = (batch, depth, seq * heads, HEAD_DIM)
    for l in range(depth):
        pu, q, k, v, *cache_kv = _premix(
            x, mod, l, norm1_g[l][None], w_in[l].astype(BF16),
            jnp.tile(q_norm_g[l], heads)[None], jnp.tile(k_norm_g[l], heads)[None], hsum,
            n_prompt, cond_row, cache_kv, cache_shape, seq)
        attn = _attention(q, k, v, ctx_k, ctx_v, _relative_bias_table(rpb[l]), l,
                          n_prompt, seq, dec_seq)
        mix_args = (mod, l, w_pool[l].astype(BF16), pool_scale[l][None], w_out[l].astype(BF16),
                    norm2_g[l][None])
        li = l // 2
        if l % 2 == 0:
            assert ffn_w_gate.shape[2] % FF_CHUNK == 0
            x = _postmix_ffn(x, pu, attn, *mix_args, ffn_w_gate[li].astype(BF16),
                             ffn_w_up[li].astype(BF16), ffn_w_down[li].astype(BF16),
                             seq_info, cond_row)
        else:
            router = jnp.zeros((d, ROUTER_LANES), F32).at[:, :n_experts].set(moe_router[li])
            r_hi = router.astype(BF16)
            r_lo = (router - r_hi.astype(F32)).astype(BF16)
            x2, h_rows, rout, g0, g1, counts = _postmix_router(
                x, pu, attn, *mix_args, r_hi, r_lo, n_experts, seq_info, cond_row)
            routing = rout[:, :2 * TOP_K].astype(jnp.int32)
            n_tiles_max = (TOP_K * n + n_experts * (MOE_ROW_TILE - 1)) // MOE_ROW_TILE
            row_flat, tile_expert, n_active, n_dump_tiles = _dispatch_plan(
                routing[:, :TOP_K], routing[:, TOP_K:], counts[0, :n_experts].astype(jnp.int32),
                n_tiles_max)
            y_rows = _moe(tile_expert, n_active, row_flat, h_rows,
                          moe_w_gate[li], moe_w_up[li], moe_w_down[li], d, TOP_K * n, n_dump_tiles)
            x = _combine(y_rows, x2, g0, g1, mod, l, cond_row, n_prompt,
                         split_output=(l == depth - 1))

    if not isinstance(x, tuple):
        x = (x[:n_prompt], x[n_prompt:])
    y_prompt = x[0].reshape(batch, seq, d)
    y_sample = x[1].reshape(dec_batch, dec_seq, d)
    new_k, new_v = (a.reshape(batch, depth, seq, heads, HEAD_DIM) for a in cache_kv)
    return (y_prompt, y_sample, new_k, new_v)
```

```python
import functools

import numpy as np
import jax
import jax.numpy as jnp
from jax import lax
from jax.experimental import pallas as pl
from jax.experimental.pallas import tpu as pltpu

F32 = jnp.float32
BF16 = jnp.bfloat16

GRID_W = 64
POOL_WINDOWS = (2, 4, 8, 16)
HEAD_DIM = 64
NA_WIN_R = 8
NA_WIN_C = 16
N_MOD = 6
TOP_K = 2
NORM_EPS = 1e-6
LOG2_E = 1.4426950408889634

LANES = 128
SUBLANES = 8
VMEM_LIMIT_BYTES = 56 * 1024 * 1024

TOKEN_TILE = 512
POOL_HALO = 8
FF_CHUNK = 256
MOE_ROW_TILE = 1008
MOE_F_CHUNK = 512
ADA_COL_TILE = 1024
COND_ROWS = 16
ROUTER_LANES = 128


def _params(n_axes=1):
    return pltpu.CompilerParams(
        dimension_semantics=("arbitrary",) * n_axes,
        vmem_limit_bytes=VMEM_LIMIT_BYTES,
    )


def _resident(shape, index_map):
    return pl.BlockSpec(shape, index_map, pipeline_mode=pl.Buffered(1))


def _x_specs(x, n_prompt_tiles):
    if not isinstance(x, tuple):
        return [pl.BlockSpec((TOKEN_TILE, x.shape[1]), lambda i: (i, 0))]
    d = x[0].shape[1]
    return [pl.BlockSpec((TOKEN_TILE, d), lambda i: (jnp.minimum(i, n_prompt_tiles - 1), 0)),
            pl.BlockSpec((TOKEN_TILE, d), lambda i: (jnp.maximum(i - n_prompt_tiles, 0), 0))]


def _load_x(i, x_refs, n_prompt_tiles):
    if len(x_refs) == 1:
        return x_refs[0][...]
    return jnp.where(i < n_prompt_tiles, x_refs[0][...], x_refs[1][...])


def _as_tuple(x):
    return x if isinstance(x, tuple) else (x,)


def _ada_kernel(c_ref, w_ref, b_ref, o_ref):
    c = c_ref[...]
    s = c * jax.nn.sigmoid(c)
    o_ref[...] = jnp.dot(s.astype(BF16), w_ref[...].astype(BF16),
                         preferred_element_type=F32) + b_ref[...]


def _ada(cond, w_ada, b_ada):
    depth, d, width = w_ada.shape
    return pl.pallas_call(
        _ada_kernel,
        grid=(depth, width // ADA_COL_TILE),
        in_specs=[
            pl.BlockSpec((COND_ROWS, d), lambda l, j: (0, 0)),
            pl.BlockSpec((None, d, ADA_COL_TILE), lambda l, j: (l, 0, j)),
            pl.BlockSpec((None, 1, ADA_COL_TILE), lambda l, j: (l, 0, j)),
        ],
        out_specs=pl.BlockSpec((None, COND_ROWS, ADA_COL_TILE), lambda l, j: (l, 0, j)),
        out_shape=jax.ShapeDtypeStruct((depth, COND_ROWS, width), F32),
        compiler_params=_params(2),
        name="ada",
    )(cond, w_ada, b_ada.reshape(depth, 1, width))


def _premix_kernel(*refs, n_x, n_prompt_tiles, pool_w, na_w, seq, layer, creates_cache):
    n_in = n_x + (6 if creates_cache else 8)
    mod_ref, g_ref, w_ref, qg_ref, kg_ref, hsum_ref = refs[n_x:n_x + 6]
    pu_ref, q_ref, k_ref, v_ref, kf_ref, vf_ref = refs[n_in:]
    i = pl.program_id(0)
    x = _load_x(i, refs[:n_x], n_prompt_tiles)
    inv = lax.rsqrt(jnp.mean(x * x, axis=-1, keepdims=True) + NORM_EPS)
    mod = mod_ref[...]
    h = (x * inv * g_ref[...]) * (1.0 + mod[1:2]) + mod[0:1]
    u = jnp.dot(h.astype(BF16), w_ref[...], preferred_element_type=F32)
    pu_ref[...] = u[:, :pool_w]
    q = u[:, pool_w:pool_w + na_w]
    k = u[:, pool_w + na_w:pool_w + 2 * na_w]
    v = u[:, pool_w + 2 * na_w:]

    def head_norm(t, g):
        ms = jnp.dot((t * t).astype(BF16), hsum_ref[...],
                     preferred_element_type=F32) * (1.0 / HEAD_DIM)
        return t * lax.rsqrt(ms + NORM_EPS) * g

    qn = head_norm(q, qg_ref[...])
    kn = head_norm(k, kg_ref[...])
    q_ref[...] = (qn * (HEAD_DIM ** -0.5 * LOG2_E)).astype(BF16)
    k_ref[...] = kn.astype(BF16)
    v_ref[...] = v.astype(BF16)

    @pl.when(i < n_prompt_tiles)
    def _():
        heads = na_w // HEAD_DIM
        if creates_cache:
            for lz in range(kf_ref.shape[1]):
                if lz != layer:
                    kf_ref[:, lz] = jnp.zeros((kf_ref.shape[0],) + kf_ref.shape[2:], F32)
                    vf_ref[:, lz] = jnp.zeros((vf_ref.shape[0],) + vf_ref.shape[2:], F32)
            kf_l, vf_l = kf_ref.at[:, layer], vf_ref.at[:, layer]
        else:
            kf_l, vf_l = kf_ref, vf_ref
        for b in range(TOKEN_TILE // seq):
            for hd in range(heads):
                rows, cols = slice(b * seq, (b + 1) * seq), slice(hd * HEAD_DIM, (hd + 1) * HEAD_DIM)
                kf_l[b, pl.ds(hd, seq, stride=heads), :] = kn[rows, cols]
                vf_l[b, pl.ds(hd, seq, stride=heads), :] = v[rows, cols]


def _premix(x, mod, l, norm_g, w_in, q_g, k_g, hsum, n_prompt, cond_row, cache_kv, cache_shape,
            seq):
    xs = _as_tuple(x)
    creates_cache = cache_kv is None
    n, d = sum(a.shape[0] for a in xs), xs[0].shape[1]
    pool_w = d // 2
    na_w = d - pool_w
    n_tiles = n // TOKEN_TILE
    n_prompt_tiles = n_prompt // TOKEN_TILE
    last_p = n_prompt_tiles - 1
    tok = lambda w: pl.BlockSpec((TOKEN_TILE, w), lambda i: (i, 0))
    if creates_cache:
        cache_spec = pl.BlockSpec((TOKEN_TILE // seq,) + cache_shape[1:],
                                  lambda i: (jnp.minimum(i, last_p), 0, 0, 0))
        cache_in, cache_in_specs, aliases = (), [], {}
    else:
        cache_spec = pl.BlockSpec((TOKEN_TILE // seq, None) + cache_shape[2:],
                                  lambda i: (jnp.minimum(i, last_p), l, 0, 0))
        cache_in, cache_in_specs = tuple(cache_kv), [pl.BlockSpec(memory_space=pl.ANY)] * 2
        aliases = {len(xs) + 6: 4, len(xs) + 7: 5}
    kernel = functools.partial(_premix_kernel, n_x=len(xs), n_prompt_tiles=n_prompt_tiles,
                               pool_w=pool_w, na_w=na_w, seq=seq, layer=l,
                               creates_cache=creates_cache)
    return pl.pallas_call(
        kernel,
        grid=(n_tiles,),
        in_specs=_x_specs(x, n_prompt_tiles) + [
            pl.BlockSpec((None, None, N_MOD, d), lambda i: (l, cond_row(i), 0, 0)),
            pl.BlockSpec((1, d), lambda i: (0, 0)),
            _resident((d, w_in.shape[1]), lambda i: (0, 0)),
            pl.BlockSpec((1, na_w), lambda i: (0, 0)),
            pl.BlockSpec((1, na_w), lambda i: (0, 0)),
            _resident((na_w, na_w), lambda i: (0, 0)),
        ] + cache_in_specs,
        out_specs=[tok(pool_w), tok(na_w), tok(na_w), tok(na_w), cache_spec, cache_spec],
        out_shape=[
            jax.ShapeDtypeStruct((n, pool_w), F32),
            jax.ShapeDtypeStruct((n, na_w), BF16),
            jax.ShapeDtypeStruct((n, na_w), BF16),
            jax.ShapeDtypeStruct((n, na_w), BF16),
            jax.ShapeDtypeStruct(cache_shape, F32),
            jax.ShapeDtypeStruct(cache_shape, F32),
        ],
        input_output_aliases=aliases,
        compiler_params=_params(),
        name=f"premix{l}",
    )(*xs, mod, norm_g, w_in, q_g, k_g, hsum, *cache_in)


_NT = (((1,), (1,)), ((), ()))
_TN = (((0,), (0,)), ((), ()))


def _block_diag_queries(q2):
    lo = lax.broadcasted_iota(jnp.int32, q2.shape, 1) < HEAD_DIM
    zero = jnp.zeros_like(q2)
    return jnp.concatenate([jnp.where(lo, q2, zero), jnp.where(lo, zero, q2)], axis=0)


def _pick_head_blocks(o, nq):
    lo = lax.broadcasted_iota(jnp.int32, (nq, LANES), 1) < HEAD_DIM
    return jnp.where(lo, o[:nq], o[nq:])


def _pair_attention(q2, k, v):
    nq = q2.shape[0]
    s = lax.dot_general(k, _block_diag_queries(q2), _NT, preferred_element_type=F32)
    p = jnp.exp2(s - jnp.max(s, axis=0, keepdims=True))
    r = 1.0 / jnp.sum(p, axis=0, keepdims=True)
    o = lax.dot_general((p * r).astype(BF16), v, _TN, preferred_element_type=F32)
    return _pick_head_blocks(o, nq)


def _attn_kernel(q_ref, k_ref, v_ref, ckf_ref, cvf_ref, bias_ref, o_ref, s_ref, p_ref,
                 ck_ref, cv_ref, *, n_prompt_tiles, seq, dec_seq, n_pairs):
    i = pl.program_id(0)
    tiles_per_seq = dec_seq // TOKEN_TILE
    rows_per_tile = TOKEN_TILE // GRID_W
    rows = dec_seq // GRID_W
    win_keys = NA_WIN_R * GRID_W
    block_off = (i * TOKEN_TILE) % dec_seq

    @pl.when(i < n_prompt_tiles)
    def _():
        for s in range(TOKEN_TILE // seq):
            start = pl.multiple_of(block_off + s * seq, seq)
            for hp in range(n_pairs):
                lanes = slice(hp * LANES, (hp + 1) * LANES)
                out = _pair_attention(q_ref[s * seq:(s + 1) * seq, lanes],
                                      k_ref[pl.ds(start, seq), lanes],
                                      v_ref[pl.ds(start, seq), lanes])
                o_ref[s * seq:(s + 1) * seq, lanes] = out.astype(o_ref.dtype)

    @pl.when(i >= n_prompt_tiles)
    def _():
        tile_in_seq = (i - n_prompt_tiles) % tiles_per_seq
        row0 = tile_in_seq * rows_per_tile

        @pl.when(tile_in_seq == 0)
        def _():
            past = ck_ref.shape[0]
            heads = 2 * n_pairs
            for src, dst in ((ckf_ref, ck_ref), (cvf_ref, cv_ref)):
                for hp in range(n_pairs):
                    pair = [src[pl.ds(2 * hp + t, past, stride=heads), :] for t in range(2)]
                    dst[:, hp * LANES:(hp + 1) * LANES] = (
                        jnp.concatenate(pair, axis=-1).astype(BF16))

        def indices(rl):
            r = row0 + rl
            r0 = jnp.clip(r - NA_WIN_R // 2, 0, rows - NA_WIN_R)
            return r0 - r + NA_WIN_R - 1, pl.multiple_of(r0 * GRID_W, GRID_W)

        def scores(rl):
            d0, kstart = indices(rl)
            for hp in range(n_pairs):
                lanes = slice(hp * LANES, (hp + 1) * LANES)
                qbd = _block_diag_queries(q_ref[rl * GRID_W:(rl + 1) * GRID_W, lanes])
                bias = bias_ref[hp, pl.ds(d0, NA_WIN_R)].reshape(win_keys, LANES)
                s_ref[rl % 2, hp, 0:win_keys] = lax.dot_general(
                    k_ref[pl.ds(kstart, win_keys), lanes], qbd, _NT,
                    preferred_element_type=F32) + bias
                s_ref[rl % 2, hp, win_keys:] = lax.dot_general(
                    ck_ref[:, lanes], qbd, _NT, preferred_element_type=F32)

        def softmax(rl):
            for hp in range(n_pairs):
                s = s_ref[rl % 2, hp]
                p = jnp.exp2(s - jnp.max(s, axis=0, keepdims=True))
                rr = 1.0 / jnp.sum(p, axis=0, keepdims=True)
                p_ref[rl % 2, hp] = (p * rr).astype(BF16)

        def values(rl):
            _, kstart = indices(rl)
            for hp in range(n_pairs):
                lanes = slice(hp * LANES, (hp + 1) * LANES)
                o = (lax.dot_general(p_ref[rl % 2, hp, 0:win_keys],
                                     v_ref[pl.ds(kstart, win_keys), lanes],
                                     _TN, preferred_element_type=F32)
                     + lax.dot_general(p_ref[rl % 2, hp, win_keys:], cv_ref[:, lanes], _TN,
                                       preferred_element_type=F32))
                o_ref[rl * GRID_W:(rl + 1) * GRID_W, lanes] = (
                    _pick_head_blocks(o, GRID_W).astype(o_ref.dtype))

        for step in range(rows_per_tile + 2):
            if step < rows_per_tile:
                scores(step)
            if 1 <= step <= rows_per_tile:
                softmax(step - 1)
            if step >= 2:
                values(step - 2)


def _attention(q, k, v, ctx_k, ctx_v, bias, l, n_prompt, seq, dec_seq):
    n, na_w = q.shape
    n_tiles = n // TOKEN_TILE
    n_prompt_tiles = n_prompt // TOKEN_TILE
    tiles_per_seq = dec_seq // TOKEN_TILE
    heads = na_w // HEAD_DIM
    past = ctx_k.shape[2] // heads
    n_pairs = na_w // LANES
    n_keys = NA_WIN_R * GRID_W + past
    kv_spec = pl.BlockSpec((dec_seq, na_w), lambda i: (i * TOKEN_TILE // dec_seq, 0))
    ctx_spec = pl.BlockSpec(
        (None, None, past * heads, HEAD_DIM),
        lambda i: (jnp.maximum(i - n_prompt_tiles, 0) // tiles_per_seq, l, 0, 0))
    kernel = functools.partial(_attn_kernel, n_prompt_tiles=n_prompt_tiles, seq=seq,
                               dec_seq=dec_seq, n_pairs=n_pairs)
    return pl.pallas_call(
        kernel,
        grid=(n_tiles,),
        in_specs=[
            pl.BlockSpec((TOKEN_TILE, na_w), lambda i: (i, 0)),
            kv_spec, kv_spec, ctx_spec, ctx_spec,
            _resident(bias.shape, lambda i: (0, 0, 0, 0)),
        ],
        out_specs=pl.BlockSpec((TOKEN_TILE, na_w), lambda i: (i, 0)),
        out_shape=jax.ShapeDtypeStruct((n, na_w), BF16),
        scratch_shapes=[pltpu.VMEM((2, n_pairs, n_keys, LANES), F32),
                        pltpu.VMEM((2, n_pairs, n_keys, LANES), BF16),
                        pltpu.VMEM((past, na_w), BF16),
                        pltpu.VMEM((past, na_w), BF16)],
        compiler_params=_params(),
        name=f"attn{l}",
    )(q, k, v, ctx_k, ctx_v, bias)


def _relative_bias_table(rpb_l):
    heads = rpb_l.shape[0]
    kc = np.arange(GRID_W)[:, None]
    qc = np.arange(GRID_W)[None, :]
    q_start = np.clip(qc - NA_WIN_C // 2, 0, GRID_W - NA_WIN_C)
    valid = (kc >= q_start) & (kc < q_start + NA_WIN_C)
    dc_idx = np.clip(kc - qc, -(NA_WIN_C - 1), NA_WIN_C - 1) + NA_WIN_C - 1
    t = jnp.zeros(rpb_l.shape[:2] + dc_idx.shape, F32)
    for c in range(rpb_l.shape[2]):
        t = jnp.where(dc_idx[None, None] == c, rpb_l[:, :, c, None, None].astype(F32), t)
    t = jnp.where(valid[None, None], t * LOG2_E, -jnp.inf)
    t = t.reshape(heads // 2, 2, 2 * NA_WIN_R - 1, GRID_W, GRID_W)
    return t.transpose(0, 2, 3, 1, 4).reshape(heads // 2, 2 * NA_WIN_R - 1, GRID_W, 2 * GRID_W)


def _mod_static(t, m):
    return t & (m - 1) if m & (m - 1) == 0 else lax.rem(t, m)


def _pool_mix(i, b, pu_ref, pp_ref, pn_ref, ext_ref, wp_ref, ps_ref, *, n_prompt_tiles, seq, dec_seq):
    sub = seq
    h = POOL_HALO
    is_prompt = i < n_prompt_tiles
    seq_len = jnp.where(is_prompt, seq, dec_seq)
    lo, hi = b * sub, (b + 1) * sub
    tok0 = i * TOKEN_TILE + lo
    pos0 = jnp.where(is_prompt, _mod_static(tok0, seq), _mod_static(tok0, dec_seq))
    prev = pp_ref[...] if lo == 0 else pu_ref[lo - h:lo, :]
    nxt = pn_ref[...] if hi == TOKEN_TILE else pu_ref[hi:hi + h, :]
    ext_ref[b, 0:h] = jnp.where(pos0 != 0, prev, 0.0)
    ext_ref[b, h:h + sub] = pu_ref[lo:hi, :]
    ext_ref[b, h + sub:] = jnp.where(pos0 + sub != seq_len, nxt, 0.0)
    left = pos0 + lax.broadcasted_iota(jnp.int32, (sub, LANES), 0)
    right = seq_len - left
    outs = []
    for g, window in enumerate(POOL_WINDOWS):
        half = window // 2
        lanes = slice(g * LANES, (g + 1) * LANES)
        total = ext_ref[b, h - half:h - half + sub, lanes]
        for j in range(1 - half, half):
            total = total + ext_ref[b, h + j:h + j + sub, lanes]
        count = (jnp.minimum(left, half) + jnp.minimum(right, half)).astype(F32)
        pooled = total / count - pu_ref[lo:hi, lanes]
        outs.append(jnp.dot(pooled.astype(BF16), wp_ref[g], preferred_element_type=F32))
    return jnp.concatenate(outs, axis=-1) * ps_ref[...]


def _postmix_chain(i, x_refs, pu_ref, pp_ref, pn_ref, at_ref, mod_ref, wp_ref, ps_ref, wo_ref,
                   g2_ref, ext_ref, tail, **seq_info):
    sub = seq_info["seq"]
    n_sub = TOKEN_TILE // sub
    mod = mod_ref[...]
    is_prompt = i < seq_info["n_prompt_tiles"]

    def rows(b):
        return slice(b * sub, (b + 1) * sub)

    def pool_stage(b, _):
        return _pool_mix(i, b, pu_ref, pp_ref, pn_ref, ext_ref, wp_ref, ps_ref, **seq_info)

    def mix_stage(b, pool_out):
        mixed = jnp.concatenate([pool_out.astype(BF16), at_ref[rows(b), :]], axis=-1)
        mix = jnp.dot(mixed, wo_ref[...], preferred_element_type=F32)
        if len(x_refs) == 1:
            x = x_refs[0][rows(b), :]
        else:
            x = jnp.where(is_prompt, x_refs[0][rows(b), :], x_refs[1][rows(b), :])
        return x + mod[2:3] * mix

    def norm_stage(b, x2):
        inv = lax.rsqrt(jnp.mean(x2 * x2, axis=-1, keepdims=True) + NORM_EPS)
        return x2, (x2 * inv * g2_ref[...]) * (1.0 + mod[4:5]) + mod[3:4]

    def tail_stage(b, x2_h2):
        tail(rows(b), *x2_h2)

    stages = (pool_stage, mix_stage, norm_stage, tail_stage)
    state = [None] * n_sub
    for step in range(n_sub + len(stages) - 1):
        for b in range(n_sub):
            if 0 <= step - b < len(stages):
                state[b] = stages[step - b](b, state[b])
    return mod[5:6]


def _postmix_ffn_kernel(*refs, n_x, **seq_info):
    (pu_ref, pp_ref, pn_ref, at_ref, mod_ref, wp_ref, ps_ref, wo_ref,
     g2_ref, wg_ref, wu_ref, wd_ref, y_ref, ext_ref, h_ref, acc_ref) = refs[n_x:]
    i = pl.program_id(0)

    def tail(rows, x2, h2):
        y_ref[rows, :] = x2
        h_ref[rows, :] = h2.astype(BF16)

    gate2 = _postmix_chain(i, refs[:n_x], pu_ref, pp_ref, pn_ref, at_ref, mod_ref, wp_ref,
                           ps_ref, wo_ref, g2_ref, ext_ref, tail, **seq_info)
    acc_ref[...] = jnp.zeros_like(acc_ref)

    h = h_ref[...]
    for c in range(wg_ref.shape[1] // FF_CHUNK):
        cols = slice(c * FF_CHUNK, (c + 1) * FF_CHUNK)
        a = jnp.dot(h, wg_ref[:, cols], preferred_element_type=F32)
        b = jnp.dot(h, wu_ref[:, cols], preferred_element_type=F32)
        act = (a * jax.nn.sigmoid(a) * b).astype(BF16)
        acc_ref[...] += jnp.dot(act, wd_ref[cols, :], preferred_element_type=F32)
    y_ref[...] = y_ref[...] + gate2 * acc_ref[...]


def _postmix_router_kernel(*refs, n_x, n_experts, **seq_info):
    (pu_ref, pp_ref, pn_ref, at_ref, mod_ref, wp_ref, ps_ref, wo_ref,
     g2_ref, rh_ref, rl_ref, tri_ref, x2_ref, hr_ref, rout_ref, g0_ref, g1_ref, cnt_ref,
     ext_ref) = refs[n_x:]
    i = pl.program_id(0)

    @pl.when(i == 0)
    def _():
        cnt_ref[...] = jnp.zeros_like(cnt_ref)

    def tail(rows, x2, h2):
        x2_ref[rows, :] = x2
        n_rows = rows.stop - rows.start
        for s in range(h2.shape[1] // LANES):
            hr_ref[pl.ds(rows.start * SUBLANES + s, n_rows, stride=SUBLANES), :] = (
                h2[:, s * LANES:(s + 1) * LANES])
        hi = h2.astype(BF16)
        lo = (h2 - hi.astype(F32)).astype(BF16)
        logits = (jnp.dot(hi, rh_ref[...], preferred_element_type=F32)
                  + jnp.dot(lo, rh_ref[...], preferred_element_type=F32)
                  + jnp.dot(hi, rl_ref[...], preferred_element_type=F32))
        lane = lax.broadcasted_iota(jnp.int32, logits.shape, 1).astype(F32)
        logits = jnp.where(lane < n_experts, logits, -jnp.inf)
        m1 = jnp.max(logits, axis=-1, keepdims=True)
        i1 = jnp.min(jnp.where(logits == m1, lane, float(ROUTER_LANES)), axis=-1, keepdims=True)
        rest = jnp.where(lane == i1, -jnp.inf, logits)
        m2 = jnp.max(rest, axis=-1, keepdims=True)
        i2 = jnp.min(jnp.where(rest == m2, lane, float(ROUTER_LANES)), axis=-1, keepdims=True)
        e = jnp.exp(m2 - m1)
        pick0 = jnp.where(lane == i1, 1.0, 0.0)
        pick1 = jnp.where(lane == i2, 1.0, 0.0)
        picks = pick0 + pick1
        before = cnt_ref[...] + jnp.dot(tri_ref[...], picks.astype(BF16),
                                        preferred_element_type=F32)
        rank0 = jnp.sum(before * pick0, axis=-1, keepdims=True)
        rank1 = jnp.sum(before * pick1, axis=-1, keepdims=True)
        cnt_ref[...] = before[n_rows - 1:n_rows, :] + picks[n_rows - 1:n_rows, :]
        rout_ref[rows, :] = jnp.where(
            lane == 0, i1, jnp.where(lane == 1, i2, jnp.where(lane == 2, rank0,
                                                              jnp.where(lane == 3, rank1, 0.0))))
        g0_ref[rows, :] = jnp.broadcast_to(1.0 / (1.0 + e), (n_rows, LANES))
        g1_ref[rows, :] = jnp.broadcast_to(e / (1.0 + e), (n_rows, LANES))

    _postmix_chain(i, refs[:n_x], pu_ref, pp_ref, pn_ref, at_ref, mod_ref, wp_ref,
                   ps_ref, wo_ref, g2_ref, ext_ref, tail, **seq_info)


def _pool_scratch_shape(seq_info, pool_w):
    sub = seq_info["seq"]
    return (TOKEN_TILE // sub, sub + 2 * POOL_HALO, pool_w)


def _postmix_specs(x, n, d, pool_w, na_w, l, cond_row, n_prompt_tiles):
    n_halo_blocks = n // POOL_HALO
    per_tile = TOKEN_TILE // POOL_HALO
    tok = lambda w: pl.BlockSpec((TOKEN_TILE, w), lambda i: (i, 0))
    return _x_specs(x, n_prompt_tiles) + [
        tok(pool_w),
        pl.BlockSpec((POOL_HALO, pool_w), lambda i: (jnp.maximum(i * per_tile - 1, 0), 0)),
        pl.BlockSpec((POOL_HALO, pool_w),
                     lambda i: (jnp.minimum((i + 1) * per_tile, n_halo_blocks - 1), 0)),
        tok(na_w),
        pl.BlockSpec((None, None, N_MOD, d), lambda i: (l, cond_row(i), 0, 0)),
        _resident((len(POOL_WINDOWS), LANES, LANES), lambda i: (0, 0, 0)),
        pl.BlockSpec((1, pool_w), lambda i: (0, 0)),
        _resident((d, d), lambda i: (0, 0)),
        pl.BlockSpec((1, d), lambda i: (0, 0)),
    ]


def _postmix_ffn(x, pu, attn, mod, l, w_pool, pool_scale, w_out, norm_g, wg, wu, wd,
                 seq_info, cond_row):
    xs = _as_tuple(x)
    n, d = pu.shape[0], xs[0].shape[1]
    pool_w = pu.shape[1]
    d_ff = wg.shape[1]
    kernel = functools.partial(_postmix_ffn_kernel, n_x=len(xs), **seq_info)
    return pl.pallas_call(
        kernel,
        grid=(n // TOKEN_TILE,),
        in_specs=_postmix_specs(x, n, d, pool_w, attn.shape[1], l, cond_row,
                                seq_info["n_prompt_tiles"]) + [
            _resident((d, d_ff), lambda i: (0, 0)),
            _resident((d, d_ff), lambda i: (0, 0)),
            _resident((d_ff, d), lambda i: (0, 0)),
        ],
        out_specs=pl.BlockSpec((TOKEN_TILE, d), lambda i: (i, 0)),
        out_shape=jax.ShapeDtypeStruct((n, d), F32),
        scratch_shapes=[
            pltpu.VMEM(_pool_scratch_shape(seq_info, pool_w), F32),
            pltpu.VMEM((TOKEN_TILE, d), BF16),
            pltpu.VMEM((TOKEN_TILE, d), F32),
        ],
        compiler_params=_params(),
        name=f"postmix_ffn{l}",
    )(*xs, pu, pu, pu, attn, mod, w_pool, pool_scale, w_out, norm_g, wg, wu, wd)


def _postmix_router(x, pu, attn, mod, l, w_pool, pool_scale, w_out, norm_g, r_hi, r_lo,
                    n_experts, seq_info, cond_row):
    xs = _as_tuple(x)
    sub = seq_info["seq"]
    tri = jnp.asarray(np.tril(np.ones((sub, sub)), -1), BF16)
    n, d = pu.shape[0], xs[0].shape[1]
    pool_w = pu.shape[1]
    kernel = functools.partial(_postmix_router_kernel, n_x=len(xs), n_experts=n_experts,
                               **seq_info)
    row_tile = TOKEN_TILE * d // LANES
    return pl.pallas_call(
        kernel,
        grid=(n // TOKEN_TILE,),
        in_specs=_postmix_specs(x, n, d, pool_w, attn.shape[1], l, cond_row,
                                seq_info["n_prompt_tiles"]) + [
            pl.BlockSpec((d, ROUTER_LANES), lambda i: (0, 0)),
            pl.BlockSpec((d, ROUTER_LANES), lambda i: (0, 0)),
            pl.BlockSpec((sub, sub), lambda i: (0, 0)),
        ],
        out_specs=[
            pl.BlockSpec((TOKEN_TILE, d), lambda i: (i, 0)),
            pl.BlockSpec((row_tile, LANES), lambda i: (i, 0)),
            pl.BlockSpec((TOKEN_TILE, ROUTER_LANES), lambda i: (i, 0)),
            pl.BlockSpec((TOKEN_TILE, LANES), lambda i: (i, 0)),
            pl.BlockSpec((TOKEN_TILE, LANES), lambda i: (i, 0)),
            pl.BlockSpec((1, ROUTER_LANES), lambda i: (0, 0)),
        ],
        out_shape=[
            jax.ShapeDtypeStruct((n, d), F32),
            jax.ShapeDtypeStruct((n * d // LANES, LANES), F32),
            jax.ShapeDtypeStruct((n, ROUTER_LANES), F32),
            jax.ShapeDtypeStruct((n, LANES), F32),
            jax.ShapeDtypeStruct((n, LANES), F32),
            jax.ShapeDtypeStruct((1, ROUTER_LANES), F32),
        ],
        scratch_shapes=[pltpu.VMEM(_pool_scratch_shape(seq_info, pool_w), F32)],
        compiler_params=_params(),
        name=f"postmix_router{l}",
    )(*xs, pu, pu, pu, attn, mod, w_pool, pool_scale, w_out, norm_g, r_hi, r_lo, tri)


def _moe_kernel(te_ref, na_ref, rf_ref, h_hbm, wg_ref, wu_ref, wd_ref, y_hbm,
                xbuf, ybuf, xb16, acc_ref, gsem, ssem, zsem, *,
                cpr, rows_per_step, n_flat, n_dump_tiles):
    j = pl.program_id(0)
    f = pl.program_id(1)
    n_f = pl.num_programs(1)
    tme = MOE_ROW_TILE
    tile_rows = tme * cpr
    n_active = na_ref[0]
    last_token = n_flat // TOP_K - 1

    def gather_row(tile, slot, row):
        v = rf_ref[(tile + 1) * tme + row]
        tok = jnp.minimum(lax.shift_right_logical(v, 1), last_token)
        return pltpu.make_async_copy(
            h_hbm.at[pl.ds(pl.multiple_of(tok * cpr, cpr), cpr)],
            xbuf.at[slot, pl.ds(pl.multiple_of(row * cpr, cpr), cpr)],
            gsem.at[slot])

    def scatter_row(tile, slot, row):
        v = rf_ref[(tile + 1) * tme + row]
        return pltpu.make_async_copy(
            ybuf.at[slot, pl.ds(pl.multiple_of(row * cpr, cpr), cpr)],
            y_hbm.at[pl.ds(pl.multiple_of(v * cpr, cpr), cpr)],
            ssem.at[slot])

    def wait_gather(slot):
        pltpu.make_async_copy(h_hbm.at[pl.ds(0, tile_rows)], xbuf.at[slot], gsem.at[slot]).wait()

    def wait_scatter(slot):
        pltpu.make_async_copy(ybuf.at[slot], y_hbm.at[pl.ds(0, tile_rows)], ssem.at[slot]).wait()

    def dump_fill(t):
        return pltpu.make_async_copy(
            ybuf.at[1], y_hbm.at[pl.ds((n_flat + t * tme) * cpr, tile_rows)], zsem)

    @pl.when((f == 0) & (j == 0))
    def _():
        ybuf[1] = jnp.zeros(ybuf.shape[1:], ybuf.dtype)
        for t in range(n_dump_tiles):
            dump_fill(t).start()
        for t in range(n_dump_tiles):
            dump_fill(t).wait()

        def body(r, carry):
            gather_row(0, 0, r).start()
            return carry

        lax.fori_loop(0, tme, body, 0)

    def step(par, first, last):
        if last:
            pl.when(j >= 1)(lambda: wait_scatter(par))
        if first:
            wait_gather(par)
            for s in range(cpr):
                xb16[:, s * LANES:(s + 1) * LANES] = (
                    xbuf[par, pl.ds(s, tme, stride=cpr), :].astype(BF16))
        for t in range(rows_per_step):
            gather_row(j + 1, 1 - par, t * n_f + f).start()
            scatter_row(j - 1, 1 - par, t * n_f + f).start()
        x = xb16[...]
        a = jnp.dot(x, wg_ref[...].astype(BF16), preferred_element_type=F32)
        b = jnp.dot(x, wu_ref[...].astype(BF16), preferred_element_type=F32)
        act = (a * jax.nn.sigmoid(a) * b).astype(BF16)
        part = jnp.dot(act, wd_ref[...].astype(BF16), preferred_element_type=F32)
        total = part if first else acc_ref[...] + part
        if not last:
            acc_ref[...] = total
        else:
            for s in range(cpr):
                ybuf[par, pl.ds(s, tme, stride=cpr), :] = total[:, s * LANES:(s + 1) * LANES]

    for par in range(2):
        on = (j < n_active) & (j % 2 == par)
        pl.when(on & (f == 0))(functools.partial(step, par, True, False))
        pl.when(on & (f > 0) & (f < n_f - 1))(functools.partial(step, par, False, False))
        pl.when(on & (f == n_f - 1))(functools.partial(step, par, False, True))

        @pl.when((j == n_active) & (f == 0) & (j % 2 == par))
        def _(par=par):
            def body(r, carry):
                scatter_row(j - 1, 1 - par, r).start()
                return carry

            lax.fori_loop(0, tme, body, 0)
            wait_gather(par)
            wait_scatter(par)
            wait_scatter(1 - par)


def _moe(tile_expert, n_active, row_flat, h_rows, wg, wu, wd, d, n_flat, n_dump_tiles):
    n_steps = tile_expert.shape[0]
    d_exp = wg.shape[2]
    n_f = d_exp // MOE_F_CHUNK
    cpr = d // LANES
    tile_rows = MOE_ROW_TILE * cpr
    assert MOE_ROW_TILE % n_f == 0 and n_f >= 2

    def f_idx(j, f, na):
        return jnp.where(j < na[0], f, n_f - 1)

    grid_spec = pltpu.PrefetchScalarGridSpec(
        num_scalar_prefetch=3,
        grid=(n_steps, n_f),
        in_specs=[
            pl.BlockSpec(memory_space=pl.ANY),
            pl.BlockSpec((None, d, MOE_F_CHUNK), lambda j, f, te, na, rf: (te[j], 0, f_idx(j, f, na))),
            pl.BlockSpec((None, d, MOE_F_CHUNK), lambda j, f, te, na, rf: (te[j], 0, f_idx(j, f, na))),
            pl.BlockSpec((None, MOE_F_CHUNK, d), lambda j, f, te, na, rf: (te[j], f_idx(j, f, na), 0)),
        ],
        out_specs=pl.BlockSpec(memory_space=pl.ANY),
        scratch_shapes=[
            pltpu.VMEM((2, tile_rows, LANES), F32),
            pltpu.VMEM((2, tile_rows, LANES), F32),
            pltpu.VMEM((MOE_ROW_TILE, d), BF16),
            pltpu.VMEM((MOE_ROW_TILE, d), F32),
            pltpu.SemaphoreType.DMA((2,)),
            pltpu.SemaphoreType.DMA((2,)),
            pltpu.SemaphoreType.DMA,
        ],
    )
    kernel = functools.partial(_moe_kernel, cpr=cpr, rows_per_step=MOE_ROW_TILE // n_f,
                               n_flat=n_flat, n_dump_tiles=n_dump_tiles)
    return pl.pallas_call(
        kernel,
        grid_spec=grid_spec,
        out_shape=jax.ShapeDtypeStruct(((n_flat + n_dump_tiles * MOE_ROW_TILE) * cpr, LANES), F32),
        compiler_params=_params(2),
        name="moe",
    )(tile_expert, n_active, row_flat, h_rows, wg, wu, wd)


def _combine_kernel(y_ref, x2_ref, g0_ref, g1_ref, mod_ref, *o_refs, cpr, n_prompt_tiles):
    i = pl.program_id(0)
    tm = x2_ref.shape[0]
    gate2 = mod_ref[...][5:6]
    stride = TOP_K * cpr

    def write(o_ref):
        g0 = g0_ref[...]
        g1 = g1_ref[...]
        for s in range(cpr):
            lanes = slice(s * LANES, (s + 1) * LANES)
            ya = y_ref[pl.ds(s, tm, stride=stride), :]
            yb = y_ref[pl.ds(cpr + s, tm, stride=stride), :]
            o_ref[:, lanes] = x2_ref[:, lanes] + gate2[:, lanes] * (g0 * ya + g1 * yb)

    if len(o_refs) == 1:
        write(o_refs[0])
    else:
        pl.when(i < n_prompt_tiles)(lambda: write(o_refs[0]))
        pl.when(i >= n_prompt_tiles)(lambda: write(o_refs[1]))


def _combine(y_rows, x2, g0, g1, mod, l, cond_row, n_prompt, split_output):
    n, d = x2.shape
    cpr = d // LANES
    n_prompt_tiles = n_prompt // TOKEN_TILE
    kernel = functools.partial(_combine_kernel, cpr=cpr, n_prompt_tiles=n_prompt_tiles)
    if split_output:
        out_specs = [
            pl.BlockSpec((TOKEN_TILE, d), lambda i: (jnp.minimum(i, n_prompt_tiles - 1), 0)),
            pl.BlockSpec((TOKEN_TILE, d), lambda i: (jnp.maximum(i - n_prompt_tiles, 0), 0))]
        out_shape = [jax.ShapeDtypeStruct((n_prompt, d), F32),
                     jax.ShapeDtypeStruct((n - n_prompt, d), F32)]
    else:
        out_specs = pl.BlockSpec((TOKEN_TILE, d), lambda i: (i, 0))
        out_shape = jax.ShapeDtypeStruct((n, d), F32)
    return pl.pallas_call(
        kernel,
        grid=(n // TOKEN_TILE,),
        in_specs=[
            pl.BlockSpec((TOKEN_TILE * TOP_K * cpr, LANES), lambda i: (i, 0)),
            pl.BlockSpec((TOKEN_TILE, d), lambda i: (i, 0)),
            pl.BlockSpec((TOKEN_TILE, LANES), lambda i: (i, 0)),
            pl.BlockSpec((TOKEN_TILE, LANES), lambda i: (i, 0)),
            pl.BlockSpec((None, None, N_MOD, d), lambda i: (l, cond_row(i), 0, 0)),
        ],
        out_specs=out_specs,
        out_shape=out_shape,
        compiler_params=_params(),
        name="combine",
    )(y_rows, x2, g0, g1, mod)


def _dispatch_plan(expert_ids, ranks, counts, n_tiles_max):
    tme = MOE_ROW_TILE
    n_experts = counts.shape[0]
    n = expert_ids[0].shape[0]
    n_flat = TOP_K * n
    tiles = (counts + tme - 1) // tme
    tile_end = jnp.cumsum(tiles)
    row_off = (tile_end - tiles) * tme
    token = jnp.arange(n, dtype=jnp.int32)
    pos, flat = [], []
    for k in range(TOP_K):
        off = jnp.sum(jnp.where(expert_ids[k][:, None] == jnp.arange(n_experts)[None, :],
                                row_off[None, :], 0), axis=1)
        pos.append(tme + off + ranks[k])
        flat.append(TOP_K * token + k)
    n_active = tile_end[-1]
    n_steps = n_tiles_max + 1
    tile_ids = jnp.minimum(jnp.arange(n_steps), n_active - 1)
    tile_expert = jnp.sum(tile_ids[:, None] >= tile_end[None, :], axis=1).astype(jnp.int32)
    n_dump_tiles = n_experts + 1
    rf = jnp.full(((n_steps + 1) * tme,), -1, jnp.int32).at[jnp.concatenate(pos)].set(
        jnp.concatenate(flat), unique_indices=True)
    is_pad = rf < 0
    pad_slot = jnp.minimum(jnp.cumsum(is_pad.astype(jnp.int32)) - 1, n_dump_tiles * tme - 1)
    rf = jnp.where(is_pad, n_flat + pad_slot, rf)
    return rf, tile_expert, n_active.astype(jnp.int32).reshape(1), n_dump_tiles


def kernel(x_prompt, x_sample, cache_k, cache_v, c, c_ctx, norm1_g, norm2_g, w_ada, b_ada, w_in,
           q_norm_g, k_norm_g, w_pool, pool_scale, rpb, w_out, ffn_w_gate, ffn_w_up, ffn_w_down,
           moe_router, moe_w_gate, moe_w_up, moe_w_down):
    batch, seq, d = x_prompt.shape
    dec_batch, dec_seq, _ = x_sample.shape
    depth = w_in.shape[0]
    heads = cache_k.shape[3]
    na_w = heads * HEAD_DIM
    pool_w = d - na_w
    n_prompt = batch * seq
    n = n_prompt + dec_batch * dec_seq
    n_experts = moe_router.shape[2]
    assert pool_w == len(POOL_WINDOWS) * LANES and na_w % LANES == 0
    assert TOKEN_TILE % seq == 0 and n_prompt % dec_seq == 0 and dec_seq % TOKEN_TILE == 0
    assert dec_seq // GRID_W >= NA_WIN_R and dec_batch < COND_ROWS
    n_prompt_tiles = n_prompt // TOKEN_TILE
    seq_info = dict(n_prompt_tiles=n_prompt_tiles, seq=seq, dec_seq=dec_seq)

    def cond_row(i):
        start = i * TOKEN_TILE
        return jnp.where(start < n_prompt, dec_batch, (start - n_prompt) // dec_seq)

    cond = jnp.zeros((COND_ROWS, d), F32).at[:dec_batch].set(c).at[dec_batch].set(c_ctx)
    mod = _ada(cond, w_ada, b_ada).reshape(depth, COND_ROWS, N_MOD, d)

    hsum = jnp.asarray(np.kron(np.eye(heads), np.ones((HEAD_DIM, HEAD_DIM))), BF16)
    ctx_k = cache_k.reshape(dec_batch, depth, cache_k.shape[2] * heads, HEAD_DIM)
    ctx_v = cache_v.reshape(dec_batch, depth, cache_v.shape[2] * heads, HEAD_DIM)

    x = (x_prompt.reshape(n_prompt, d), x_sample.reshape(-1, d))
    cache_kv = None
    cache_shape = (batch, depth, seq * heads, HEAD_DIM)
    for l in range(depth):
        pu, q, k, v, *cache_kv = _premix(
            x, mod, l, norm1_g[l][None], w_in[l].astype(BF16),
            jnp.tile(q_norm_g[l], heads)[None], jnp.tile(k_norm_g[l], heads)[None], hsum,
            n_prompt, cond_row, cache_kv, cache_shape, seq)
        attn = _attention(q, k, v, ctx_k, ctx_v, _relative_bias_table(rpb[l]), l,
                          n_prompt, seq, dec_seq)
        mix_args = (mod, l, w_pool[l].astype(BF16), pool_scale[l][None], w_out[l].astype(BF16),
                    norm2_g[l][None])
        li = l // 2
        if l % 2 == 0:
            assert ffn_w_gate.shape[2] % FF_CHUNK == 0
            x = _postmix_ffn(x, pu, attn, *mix_args, ffn_w_gate[li].astype(BF16),
                             ffn_w_up[li].astype(BF16), ffn_w_down[li].astype(BF16),
                             seq_info, cond_row)
        else:
            router = jnp.zeros((d, ROUTER_LANES), F32).at[:, :n_experts].set(moe_router[li])
            r_hi = router.astype(BF16)
            r_lo = (router - r_hi.astype(F32)).astype(BF16)
            x2, h_rows, rout, g0, g1, counts = _postmix_router(
                x, pu, attn, *mix_args, r_hi, r_lo, n_experts, seq_info, cond_row)
            routing = [rout[:, t].astype(jnp.int32) for t in range(2 * TOP_K)]
            n_tiles_max = (TOP_K * n + n_experts * (MOE_ROW_TILE - 1)) // MOE_ROW_TILE
            row_flat, tile_expert, n_active, n_dump_tiles = _dispatch_plan(
                routing[:TOP_K], routing[TOP_K:], counts[0, :n_experts].astype(jnp.int32),
                n_tiles_max)
            y_rows = _moe(tile_expert, n_active, row_flat, h_rows,
                          moe_w_gate[li], moe_w_up[li], moe_w_down[li], d, TOP_K * n, n_dump_tiles)
            x = _combine(y_rows, x2, g0, g1, mod, l, cond_row, n_prompt,
                         split_output=(l == depth - 1))

    if not isinstance(x, tuple):
        x = (x[:n_prompt], x[n_prompt:])
    y_prompt = x[0].reshape(batch, seq, d)
    y_sample = x[1].reshape(dec_batch, dec_seq, d)
    new_k, new_v = (a.reshape(batch, depth, seq, heads, HEAD_DIM) for a in cache_kv)
    return (y_prompt, y_sample, new_k, new_v)
```

```python
import functools

import numpy as np
import jax
import jax.numpy as jnp
from jax import lax
from jax.experimental import pallas as pl
from jax.experimental.pallas import tpu as pltpu

F32 = jnp.float32
BF16 = jnp.bfloat16

GRID_W = 64
POOL_WINDOWS = (2, 4, 8, 16)
HEAD_DIM = 64
NA_WIN_R = 8
NA_WIN_C = 16
N_MOD = 6
TOP_K = 2
NORM_EPS = 1e-6
LOG2_E = 1.4426950408889634

LANES = 128
SUBLANES = 8
VMEM_LIMIT_BYTES = 56 * 1024 * 1024

TOKEN_TILE = 512
POOL_HALO = 8
FF_CHUNK = 256
MOE_ROW_TILE = 1008
MOE_F_CHUNK = 512
ADA_COL_TILE = 1024
COND_ROWS = 16
ROUTER_LANES = 128


def _params(n_axes=1):
    return pltpu.CompilerParams(
        dimension_semantics=("arbitrary",) * n_axes,
        vmem_limit_bytes=VMEM_LIMIT_BYTES,
    )


def _resident(shape, index_map):
    return pl.BlockSpec(shape, index_map, pipeline_mode=pl.Buffered(1))


def _x_specs(x, n_prompt_tiles):
    if not isinstance(x, tuple):
        return [pl.BlockSpec((TOKEN_TILE, x.shape[1]), lambda i: (i, 0))]
    d = x[0].shape[1]
    return [pl.BlockSpec((TOKEN_TILE, d), lambda i: (jnp.minimum(i, n_prompt_tiles - 1), 0)),
            pl.BlockSpec((TOKEN_TILE, d), lambda i: (jnp.maximum(i - n_prompt_tiles, 0), 0))]


def _load_x(i, x_refs, n_prompt_tiles):
    if len(x_refs) == 1:
        return x_refs[0][...]
    return jnp.where(i < n_prompt_tiles, x_refs[0][...], x_refs[1][...])


def _as_tuple(x):
    return x if isinstance(x, tuple) else (x,)


def _ada_kernel(c_ref, w_ref, b_ref, o_ref):
    c = c_ref[...]
    s = c * jax.nn.sigmoid(c)
    o_ref[...] = jnp.dot(s.astype(BF16), w_ref[...].astype(BF16),
                         preferred_element_type=F32) + b_ref[...]


def _ada(cond, w_ada, b_ada):
    depth, d, width = w_ada.shape
    return pl.pallas_call(
        _ada_kernel,
        grid=(depth, width // ADA_COL_TILE),
        in_specs=[
            pl.BlockSpec((COND_ROWS, d), lambda l, j: (0, 0)),
            pl.BlockSpec((None, d, ADA_COL_TILE), lambda l, j: (l, 0, j)),
            pl.BlockSpec((None, 1, ADA_COL_TILE), lambda l, j: (l, 0, j)),
        ],
        out_specs=pl.BlockSpec((None, COND_ROWS, ADA_COL_TILE), lambda l, j: (l, 0, j)),
        out_shape=jax.ShapeDtypeStruct((depth, COND_ROWS, width), F32),
        compiler_params=_params(2),
        name="ada",
    )(cond, w_ada, b_ada.reshape(depth, 1, width))


def _premix_kernel(*refs, n_x, n_prompt_tiles, pool_w, na_w, seq, layer, creates_cache):
    n_in = n_x + (6 if creates_cache else 8)
    mod_ref, g_ref, w_ref, qg_ref, kg_ref, hsum_ref = refs[n_x:n_x + 6]
    pu_ref, q_ref, k_ref, v_ref, kf_ref, vf_ref = refs[n_in:]
    i = pl.program_id(0)
    x = _load_x(i, refs[:n_x], n_prompt_tiles)
    inv = lax.rsqrt(jnp.mean(x * x, axis=-1, keepdims=True) + NORM_EPS)
    mod = mod_ref[...]
    h = (x * inv * g_ref[...]) * (1.0 + mod[1:2]) + mod[0:1]
    u = jnp.dot(h.astype(BF16), w_ref[...], preferred_element_type=F32)
    pu_ref[...] = u[:, :pool_w]
    q = u[:, pool_w:pool_w + na_w]
    k = u[:, pool_w + na_w:pool_w + 2 * na_w]
    v = u[:, pool_w + 2 * na_w:]

    def head_norm(t, g):
        ms = jnp.dot((t * t).astype(BF16), hsum_ref[...],
                     preferred_element_type=F32) * (1.0 / HEAD_DIM)
        return t * lax.rsqrt(ms + NORM_EPS) * g

    qn = head_norm(q, qg_ref[...])
    kn = head_norm(k, kg_ref[...])
    q_ref[...] = (qn * (HEAD_DIM ** -0.5 * LOG2_E)).astype(BF16)
    k_ref[...] = kn.astype(BF16)
    v_ref[...] = v.astype(BF16)

    @pl.when(i < n_prompt_tiles)
    def _():
        heads = na_w // HEAD_DIM
        if creates_cache:
            for lz in range(kf_ref.shape[1]):
                if lz != layer:
                    kf_ref[:, lz] = jnp.zeros((kf_ref.shape[0],) + kf_ref.shape[2:], F32)
                    vf_ref[:, lz] = jnp.zeros((vf_ref.shape[0],) + vf_ref.shape[2:], F32)
            kf_l, vf_l = kf_ref.at[:, layer], vf_ref.at[:, layer]
        else:
            kf_l, vf_l = kf_ref, vf_ref
        for b in range(TOKEN_TILE // seq):
            for hd in range(heads):
                rows, cols = slice(b * seq, (b + 1) * seq), slice(hd * HEAD_DIM, (hd + 1) * HEAD_DIM)
                kf_l[b, pl.ds(hd, seq, stride=heads), :] = kn[rows, cols]
                vf_l[b, pl.ds(hd, seq, stride=heads), :] = v[rows, cols]


def _premix(x, mod, l, norm_g, w_in, q_g, k_g, hsum, n_prompt, cond_row, cache_kv, cache_shape,
            seq):
    xs = _as_tuple(x)
    creates_cache = cache_kv is None
    n, d = sum(a.shape[0] for a in xs), xs[0].shape[1]
    pool_w = d // 2
    na_w = d - pool_w
    n_tiles = n // TOKEN_TILE
    n_prompt_tiles = n_prompt // TOKEN_TILE
    last_p = n_prompt_tiles - 1
    tok = lambda w: pl.BlockSpec((TOKEN_TILE, w), lambda i: (i, 0))
    if creates_cache:
        cache_spec = pl.BlockSpec((TOKEN_TILE // seq,) + cache_shape[1:],
                                  lambda i: (jnp.minimum(i, last_p), 0, 0, 0))
        cache_in, cache_in_specs, aliases = (), [], {}
    else:
        cache_spec = pl.BlockSpec((TOKEN_TILE // seq, None) + cache_shape[2:],
                                  lambda i: (jnp.minimum(i, last_p), l, 0, 0))
        cache_in, cache_in_specs = tuple(cache_kv), [pl.BlockSpec(memory_space=pl.ANY)] * 2
        aliases = {len(xs) + 6: 4, len(xs) + 7: 5}
    kernel = functools.partial(_premix_kernel, n_x=len(xs), n_prompt_tiles=n_prompt_tiles,
                               pool_w=pool_w, na_w=na_w, seq=seq, layer=l,
                               creates_cache=creates_cache)
    return pl.pallas_call(
        kernel,
        grid=(n_tiles,),
        in_specs=_x_specs(x, n_prompt_tiles) + [
            pl.BlockSpec((None, None, N_MOD, d), lambda i: (l, cond_row(i), 0, 0)),
            pl.BlockSpec((1, d), lambda i: (0, 0)),
            _resident((d, w_in.shape[1]), lambda i: (0, 0)),
            pl.BlockSpec((1, na_w), lambda i: (0, 0)),
            pl.BlockSpec((1, na_w), lambda i: (0, 0)),
            _resident((na_w, na_w), lambda i: (0, 0)),
        ] + cache_in_specs,
        out_specs=[tok(pool_w), tok(na_w), tok(na_w), tok(na_w), cache_spec, cache_spec],
        out_shape=[
            jax.ShapeDtypeStruct((n, pool_w), F32),
            jax.ShapeDtypeStruct((n, na_w), BF16),
            jax.ShapeDtypeStruct((n, na_w), BF16),
            jax.ShapeDtypeStruct((n, na_w), BF16),
            jax.ShapeDtypeStruct(cache_shape, F32),
            jax.ShapeDtypeStruct(cache_shape, F32),
        ],
        input_output_aliases=aliases,
        compiler_params=_params(),
        name=f"premix{l}",
    )(*xs, mod, norm_g, w_in, q_g, k_g, hsum, *cache_in)


_NT = (((1,), (1,)), ((), ()))
_TN = (((0,), (0,)), ((), ()))


def _block_diag_queries(q2):
    lo = lax.broadcasted_iota(jnp.int32, q2.shape, 1) < HEAD_DIM
    zero = jnp.zeros_like(q2)
    return jnp.concatenate([jnp.where(lo, q2, zero), jnp.where(lo, zero, q2)], axis=0)


def _pick_head_blocks(o, nq):
    lo = lax.broadcasted_iota(jnp.int32, (nq, LANES), 1) < HEAD_DIM
    return jnp.where(lo, o[:nq], o[nq:])


def _pair_attention(q2, k, v):
    nq = q2.shape[0]
    s = lax.dot_general(k, _block_diag_queries(q2), _NT, preferred_element_type=F32)
    p = jnp.exp2(s - jnp.max(s, axis=0, keepdims=True))
    r = 1.0 / jnp.sum(p, axis=0, keepdims=True)
    o = lax.dot_general((p * r).astype(BF16), v, _TN, preferred_element_type=F32)
    return _pick_head_blocks(o, nq)


def _attn_kernel(q_ref, k_ref, v_ref, ckf_ref, cvf_ref, bias_ref, o_ref, s_ref, p_ref,
                 ck_ref, cv_ref, *, n_prompt_tiles, seq, dec_seq, n_pairs):
    i = pl.program_id(0)
    tiles_per_seq = dec_seq // TOKEN_TILE
    rows_per_tile = TOKEN_TILE // GRID_W
    rows = dec_seq // GRID_W
    win_keys = NA_WIN_R * GRID_W
    block_off = (i * TOKEN_TILE) % dec_seq

    @pl.when(i < n_prompt_tiles)
    def _():
        for s in range(TOKEN_TILE // seq):
            start = pl.multiple_of(block_off + s * seq, seq)
            for hp in range(n_pairs):
                lanes = slice(hp * LANES, (hp + 1) * LANES)
                out = _pair_attention(q_ref[s * seq:(s + 1) * seq, lanes],
                                      k_ref[pl.ds(start, seq), lanes],
                                      v_ref[pl.ds(start, seq), lanes])
                o_ref[s * seq:(s + 1) * seq, lanes] = out.astype(o_ref.dtype)

    @pl.when(i >= n_prompt_tiles)
    def _():
        tile_in_seq = (i - n_prompt_tiles) % tiles_per_seq
        row0 = tile_in_seq * rows_per_tile

        @pl.when(tile_in_seq == 0)
        def _():
            past = ck_ref.shape[0]
            heads = 2 * n_pairs
            for src, dst in ((ckf_ref, ck_ref), (cvf_ref, cv_ref)):
                for hp in range(n_pairs):
                    pair = [src[pl.ds(2 * hp + t, past, stride=heads), :] for t in range(2)]
                    dst[:, hp * LANES:(hp + 1) * LANES] = (
                        jnp.concatenate(pair, axis=-1).astype(BF16))

        def indices(rl):
            r = row0 + rl
            r0 = jnp.clip(r - NA_WIN_R // 2, 0, rows - NA_WIN_R)
            return r0 - r + NA_WIN_R - 1, pl.multiple_of(r0 * GRID_W, GRID_W)

        def scores(rl):
            d0, kstart = indices(rl)
            for hp in range(n_pairs):
                lanes = slice(hp * LANES, (hp + 1) * LANES)
                qbd = _block_diag_queries(q_ref[rl * GRID_W:(rl + 1) * GRID_W, lanes])
                bias = bias_ref[hp, pl.ds(d0, NA_WIN_R)].reshape(win_keys, LANES)
                s_ref[rl % 2, hp, 0:win_keys] = lax.dot_general(
                    k_ref[pl.ds(kstart, win_keys), lanes], qbd, _NT,
                    preferred_element_type=F32) + bias
                s_ref[rl % 2, hp, win_keys:] = lax.dot_general(
                    ck_ref[:, lanes], qbd, _NT, preferred_element_type=F32)

        def softmax(rl):
            for hp in range(n_pairs):
                s = s_ref[rl % 2, hp]
                p = jnp.exp2(s - jnp.max(s, axis=0, keepdims=True))
                rr = 1.0 / jnp.sum(p, axis=0, keepdims=True)
                p_ref[rl % 2, hp] = (p * rr).astype(BF16)

        def values(rl):
            _, kstart = indices(rl)
            for hp in range(n_pairs):
                lanes = slice(hp * LANES, (hp + 1) * LANES)
                o = (lax.dot_general(p_ref[rl % 2, hp, 0:win_keys],
                                     v_ref[pl.ds(kstart, win_keys), lanes],
                                     _TN, preferred_element_type=F32)
                     + lax.dot_general(p_ref[rl % 2, hp, win_keys:], cv_ref[:, lanes], _TN,
                                       preferred_element_type=F32))
                o_ref[rl * GRID_W:(rl + 1) * GRID_W, lanes] = (
                    _pick_head_blocks(o, GRID_W).astype(o_ref.dtype))

        for step in range(rows_per_tile + 2):
            if step < rows_per_tile:
                scores(step)
            if 1 <= step <= rows_per_tile:
                softmax(step - 1)
            if step >= 2:
                values(step - 2)


def _attention(q, k, v, ctx_k, ctx_v, bias, l, n_prompt, seq, dec_seq):
    n, na_w = q.shape
    n_tiles = n // TOKEN_TILE
    n_prompt_tiles = n_prompt // TOKEN_TILE
    tiles_per_seq = dec_seq // TOKEN_TILE
    heads = na_w // HEAD_DIM
    past = ctx_k.shape[2] // heads
    n_pairs = na_w // LANES
    n_keys = NA_WIN_R * GRID_W + past
    kv_spec = pl.BlockSpec((dec_seq, na_w), lambda i: (i * TOKEN_TILE // dec_seq, 0))
    ctx_spec = pl.BlockSpec(
        (None, None, past * heads, HEAD_DIM),
        lambda i: (jnp.maximum(i - n_prompt_tiles, 0) // tiles_per_seq, l, 0, 0))
    kernel = functools.partial(_attn_kernel, n_prompt_tiles=n_prompt_tiles, seq=seq,
                               dec_seq=dec_seq, n_pairs=n_pairs)
    return pl.pallas_call(
        kernel,
        grid=(n_tiles,),
        in_specs=[
            pl.BlockSpec((TOKEN_TILE, na_w), lambda i: (i, 0)),
            kv_spec, kv_spec, ctx_spec, ctx_spec,
            _resident(bias.shape, lambda i: (0, 0, 0, 0)),
        ],
        out_specs=pl.BlockSpec((TOKEN_TILE, na_w), lambda i: (i, 0)),
        out_shape=jax.ShapeDtypeStruct((n, na_w), BF16),
        scratch_shapes=[pltpu.VMEM((2, n_pairs, n_keys, LANES), F32),
                        pltpu.VMEM((2, n_pairs, n_keys, LANES), BF16),
                        pltpu.VMEM((past, na_w), BF16),
                        pltpu.VMEM((past, na_w), BF16)],
        compiler_params=_params(),
        name=f"attn{l}",
    )(q, k, v, ctx_k, ctx_v, bias)


def _relative_bias_table(rpb_l):
    heads = rpb_l.shape[0]
    kc = np.arange(GRID_W)[:, None]
    qc = np.arange(GRID_W)[None, :]
    q_start = np.clip(qc - NA_WIN_C // 2, 0, GRID_W - NA_WIN_C)
    valid = (kc >= q_start) & (kc < q_start + NA_WIN_C)
    dc_idx = np.clip(kc - qc, -(NA_WIN_C - 1), NA_WIN_C - 1) + NA_WIN_C - 1
    t = jnp.zeros(rpb_l.shape[:2] + dc_idx.shape, F32)
    for c in range(rpb_l.shape[2]):
        t = jnp.where(dc_idx[None, None] == c, rpb_l[:, :, c, None, None].astype(F32), t)
    t = jnp.where(valid[None, None], t * LOG2_E, -jnp.inf)
    t = t.reshape(heads // 2, 2, 2 * NA_WIN_R - 1, GRID_W, GRID_W)
    return t.transpose(0, 2, 3, 1, 4).reshape(heads // 2, 2 * NA_WIN_R - 1, GRID_W, 2 * GRID_W)


def _mod_static(t, m):
    return t & (m - 1) if m & (m - 1) == 0 else lax.rem(t, m)


def _pool_mix(i, b, pu_ref, pp_ref, pn_ref, ext_ref, wp_ref, ps_ref, *, n_prompt_tiles, seq, dec_seq):
    sub = seq
    h = POOL_HALO
    is_prompt = i < n_prompt_tiles
    seq_len = jnp.where(is_prompt, seq, dec_seq)
    lo, hi = b * sub, (b + 1) * sub
    tok0 = i * TOKEN_TILE + lo
    pos0 = jnp.where(is_prompt, _mod_static(tok0, seq), _mod_static(tok0, dec_seq))
    prev = pp_ref[...] if lo == 0 else pu_ref[lo - h:lo, :]
    nxt = pn_ref[...] if hi == TOKEN_TILE else pu_ref[hi:hi + h, :]
    ext_ref[b, 0:h] = jnp.where(pos0 != 0, prev, 0.0)
    ext_ref[b, h:h + sub] = pu_ref[lo:hi, :]
    ext_ref[b, h + sub:] = jnp.where(pos0 + sub != seq_len, nxt, 0.0)
    left = pos0 + lax.broadcasted_iota(jnp.int32, (sub, LANES), 0)
    right = seq_len - left
    outs = []
    for g, window in enumerate(POOL_WINDOWS):
        half = window // 2
        lanes = slice(g * LANES, (g + 1) * LANES)
        total = ext_ref[b, h - half:h - half + sub, lanes]
        for j in range(1 - half, half):
            total = total + ext_ref[b, h + j:h + j + sub, lanes]
        count = (jnp.minimum(left, half) + jnp.minimum(right, half)).astype(F32)
        pooled = total / count - pu_ref[lo:hi, lanes]
        outs.append(jnp.dot(pooled.astype(BF16), wp_ref[g], preferred_element_type=F32))
    return jnp.concatenate(outs, axis=-1) * ps_ref[...]


def _postmix_chain(i, x_refs, pu_ref, pp_ref, pn_ref, at_ref, mod_ref, wp_ref, ps_ref, wo_ref,
                   g2_ref, ext_ref, tail, **seq_info):
    sub = seq_info["seq"]
    n_sub = TOKEN_TILE // sub
    mod = mod_ref[...]
    is_prompt = i < seq_info["n_prompt_tiles"]

    def rows(b):
        return slice(b * sub, (b + 1) * sub)

    def pool_stage(b, _):
        return _pool_mix(i, b, pu_ref, pp_ref, pn_ref, ext_ref, wp_ref, ps_ref, **seq_info)

    def mix_stage(b, pool_out):
        mixed = jnp.concatenate([pool_out.astype(BF16), at_ref[rows(b), :]], axis=-1)
        mix = jnp.dot(mixed, wo_ref[...], preferred_element_type=F32)
        if len(x_refs) == 1:
            x = x_refs[0][rows(b), :]
        else:
            x = jnp.where(is_prompt, x_refs[0][rows(b), :], x_refs[1][rows(b), :])
        return x + mod[2:3] * mix

    def norm_stage(b, x2):
        inv = lax.rsqrt(jnp.mean(x2 * x2, axis=-1, keepdims=True) + NORM_EPS)
        return x2, (x2 * inv * g2_ref[...]) * (1.0 + mod[4:5]) + mod[3:4]

    def tail_stage(b, x2_h2):
        tail(rows(b), *x2_h2)

    stages = (pool_stage, mix_stage, norm_stage, tail_stage)
    state = [None] * n_sub
    for step in range(n_sub + len(stages) - 1):
        for b in range(n_sub):
            if 0 <= step - b < len(stages):
                state[b] = stages[step - b](b, state[b])
    return mod[5:6]


def _postmix_ffn_kernel(*refs, n_x, **seq_info):
    (pu_ref, pp_ref, pn_ref, at_ref, mod_ref, wp_ref, ps_ref, wo_ref,
     g2_ref, wg_ref, wu_ref, wd_ref, y_ref, ext_ref, h_ref, acc_ref) = refs[n_x:]
    i = pl.program_id(0)

    def tail(rows, x2, h2):
        y_ref[rows, :] = x2
        h_ref[rows, :] = h2.astype(BF16)

    gate2 = _postmix_chain(i, refs[:n_x], pu_ref, pp_ref, pn_ref, at_ref, mod_ref, wp_ref,
                           ps_ref, wo_ref, g2_ref, ext_ref, tail, **seq_info)
    acc_ref[...] = jnp.zeros_like(acc_ref)

    h = h_ref[...]
    for c in range(wg_ref.shape[1] // FF_CHUNK):
        cols = slice(c * FF_CHUNK, (c + 1) * FF_CHUNK)
        a = jnp.dot(h, wg_ref[:, cols], preferred_element_type=F32)
        b = jnp.dot(h, wu_ref[:, cols], preferred_element_type=F32)
        act = (a * jax.nn.sigmoid(a) * b).astype(BF16)
        acc_ref[...] += jnp.dot(act, wd_ref[cols, :], preferred_element_type=F32)
    y_ref[...] = y_ref[...] + gate2 * acc_ref[...]


def _postmix_router_kernel(*refs, n_x, n_experts, **seq_info):
    (pu_ref, pp_ref, pn_ref, at_ref, mod_ref, wp_ref, ps_ref, wo_ref,
     g2_ref, rh_ref, rl_ref, tri_ref, x2_ref, hr_ref, rout_ref, g0_ref, g1_ref, cnt_ref,
     ext_ref) = refs[n_x:]
    i = pl.program_id(0)

    @pl.when(i == 0)
    def _():
        cnt_ref[...] = jnp.zeros_like(cnt_ref)

    def tail(rows, x2, h2):
        x2_ref[rows, :] = x2
        n_rows = rows.stop - rows.start
        for s in range(h2.shape[1] // LANES):
            hr_ref[pl.ds(rows.start * SUBLANES + s, n_rows, stride=SUBLANES), :] = (
                h2[:, s * LANES:(s + 1) * LANES])
        hi = h2.astype(BF16)
        lo = (h2 - hi.astype(F32)).astype(BF16)
        logits = (jnp.dot(hi, rh_ref[...], preferred_element_type=F32)
                  + jnp.dot(lo, rh_ref[...], preferred_element_type=F32)
                  + jnp.dot(hi, rl_ref[...], preferred_element_type=F32))
        lane = lax.broadcasted_iota(jnp.int32, logits.shape, 1).astype(F32)
        logits = jnp.where(lane < n_experts, logits, -jnp.inf)
        m1 = jnp.max(logits, axis=-1, keepdims=True)
        i1 = jnp.min(jnp.where(logits == m1, lane, float(ROUTER_LANES)), axis=-1, keepdims=True)
        rest = jnp.where(lane == i1, -jnp.inf, logits)
        m2 = jnp.max(rest, axis=-1, keepdims=True)
        i2 = jnp.min(jnp.where(rest == m2, lane, float(ROUTER_LANES)), axis=-1, keepdims=True)
        e = jnp.exp(m2 - m1)
        pick0 = jnp.where(lane == i1, 1.0, 0.0)
        pick1 = jnp.where(lane == i2, 1.0, 0.0)
        picks = pick0 + pick1
        before = cnt_ref[...] + jnp.dot(tri_ref[...], picks.astype(BF16),
                                        preferred_element_type=F32)
        rank0 = jnp.sum(before * pick0, axis=-1, keepdims=True)
        rank1 = jnp.sum(before * pick1, axis=-1, keepdims=True)
        cnt_ref[...] = before[n_rows - 1:n_rows, :] + picks[n_rows - 1:n_rows, :]
        packed = jnp.where(lane == 0, i1, jnp.where(lane == 1, i2, jnp.where(
            lane == 2, rank0, jnp.where(lane == 3, rank1, 0.0))))
        rout_ref[:, rows] = packed.T[:SUBLANES]
        g0_ref[rows, :] = jnp.broadcast_to(1.0 / (1.0 + e), (n_rows, LANES))
        g1_ref[rows, :] = jnp.broadcast_to(e / (1.0 + e), (n_rows, LANES))

    _postmix_chain(i, refs[:n_x], pu_ref, pp_ref, pn_ref, at_ref, mod_ref, wp_ref,
                   ps_ref, wo_ref, g2_ref, ext_ref, tail, **seq_info)


def _pool_scratch_shape(seq_info, pool_w):
    sub = seq_info["seq"]
    return (TOKEN_TILE // sub, sub + 2 * POOL_HALO, pool_w)


def _postmix_specs(x, n, d, pool_w, na_w, l, cond_row, n_prompt_tiles):
    n_halo_blocks = n // POOL_HALO
    per_tile = TOKEN_TILE // POOL_HALO
    tok = lambda w: pl.BlockSpec((TOKEN_TILE, w), lambda i: (i, 0))
    return _x_specs(x, n_prompt_tiles) + [
        tok(pool_w),
        pl.BlockSpec((POOL_HALO, pool_w), lambda i: (jnp.maximum(i * per_tile - 1, 0), 0)),
        pl.BlockSpec((POOL_HALO, pool_w),
                     lambda i: (jnp.minimum((i + 1) * per_tile, n_halo_blocks - 1), 0)),
        tok(na_w),
        pl.BlockSpec((None, None, N_MOD, d), lambda i: (l, cond_row(i), 0, 0)),
        _resident((len(POOL_WINDOWS), LANES, LANES), lambda i: (0, 0, 0)),
        pl.BlockSpec((1, pool_w), lambda i: (0, 0)),
        _resident((d, d), lambda i: (0, 0)),
        pl.BlockSpec((1, d), lambda i: (0, 0)),
    ]


def _postmix_ffn(x, pu, attn, mod, l, w_pool, pool_scale, w_out, norm_g, wg, wu, wd,
                 seq_info, cond_row):
    xs = _as_tuple(x)
    n, d = pu.shape[0], xs[0].shape[1]
    pool_w = pu.shape[1]
    d_ff = wg.shape[1]
    kernel = functools.partial(_postmix_ffn_kernel, n_x=len(xs), **seq_info)
    return pl.pallas_call(
        kernel,
        grid=(n // TOKEN_TILE,),
        in_specs=_postmix_specs(x, n, d, pool_w, attn.shape[1], l, cond_row,
                                seq_info["n_prompt_tiles"]) + [
            _resident((d, d_ff), lambda i: (0, 0)),
            _resident((d, d_ff), lambda i: (0, 0)),
            _resident((d_ff, d), lambda i: (0, 0)),
        ],
        out_specs=pl.BlockSpec((TOKEN_TILE, d), lambda i: (i, 0)),
        out_shape=jax.ShapeDtypeStruct((n, d), F32),
        scratch_shapes=[
            pltpu.VMEM(_pool_scratch_shape(seq_info, pool_w), F32),
            pltpu.VMEM((TOKEN_TILE, d), BF16),
            pltpu.VMEM((TOKEN_TILE, d), F32),
        ],
        compiler_params=_params(),
        name=f"postmix_ffn{l}",
    )(*xs, pu, pu, pu, attn, mod, w_pool, pool_scale, w_out, norm_g, wg, wu, wd)


def _postmix_router(x, pu, attn, mod, l, w_pool, pool_scale, w_out, norm_g, r_hi, r_lo,
                    n_experts, seq_info, cond_row):
    xs = _as_tuple(x)
    sub = seq_info["seq"]
    tri = jnp.asarray(np.tril(np.ones((sub, sub)), -1), BF16)
    n, d = pu.shape[0], xs[0].shape[1]
    pool_w = pu.shape[1]
    kernel = functools.partial(_postmix_router_kernel, n_x=len(xs), n_experts=n_experts,
                               **seq_info)
    row_tile = TOKEN_TILE * d // LANES
    return pl.pallas_call(
        kernel,
        grid=(n // TOKEN_TILE,),
        in_specs=_postmix_specs(x, n, d, pool_w, attn.shape[1], l, cond_row,
                                seq_info["n_prompt_tiles"]) + [
            pl.BlockSpec((d, ROUTER_LANES), lambda i: (0, 0)),
            pl.BlockSpec((d, ROUTER_LANES), lambda i: (0, 0)),
            pl.BlockSpec((sub, sub), lambda i: (0, 0)),
        ],
        out_specs=[
            pl.BlockSpec((TOKEN_TILE, d), lambda i: (i, 0)),
            pl.BlockSpec((row_tile, LANES), lambda i: (i, 0)),
            pl.BlockSpec((SUBLANES, TOKEN_TILE), lambda i: (0, i)),
            pl.BlockSpec((TOKEN_TILE, LANES), lambda i: (i, 0)),
            pl.BlockSpec((TOKEN_TILE, LANES), lambda i: (i, 0)),
            pl.BlockSpec((1, ROUTER_LANES), lambda i: (0, 0)),
        ],
        out_shape=[
            jax.ShapeDtypeStruct((n, d), F32),
            jax.ShapeDtypeStruct((n * d // LANES, LANES), F32),
            jax.ShapeDtypeStruct((SUBLANES, n), F32),
            jax.ShapeDtypeStruct((n, LANES), F32),
            jax.ShapeDtypeStruct((n, LANES), F32),
            jax.ShapeDtypeStruct((1, ROUTER_LANES), F32),
        ],
        scratch_shapes=[pltpu.VMEM(_pool_scratch_shape(seq_info, pool_w), F32)],
        compiler_params=_params(),
        name=f"postmix_router{l}",
    )(*xs, pu, pu, pu, attn, mod, w_pool, pool_scale, w_out, norm_g, r_hi, r_lo, tri)


def _moe_kernel(te_ref, na_ref, rf_ref, h_hbm, wg_ref, wu_ref, wd_ref, y_hbm,
                xbuf, ybuf, xb16, acc_ref, gsem, ssem, zsem, *,
                cpr, rows_per_step, n_flat, n_dump_tiles):
    j = pl.program_id(0)
    f = pl.program_id(1)
    n_f = pl.num_programs(1)
    tme = MOE_ROW_TILE
    tile_rows = tme * cpr
    n_active = na_ref[0]
    last_token = n_flat // TOP_K - 1

    def gather_row(tile, slot, row):
        v = rf_ref[(tile + 1) * tme + row]
        tok = jnp.minimum(lax.shift_right_logical(v, 1), last_token)
        return pltpu.make_async_copy(
            h_hbm.at[pl.ds(pl.multiple_of(tok * cpr, cpr), cpr)],
            xbuf.at[slot, pl.ds(pl.multiple_of(row * cpr, cpr), cpr)],
            gsem.at[slot])

    def scatter_row(tile, slot, row):
        v = rf_ref[(tile + 1) * tme + row]
        return pltpu.make_async_copy(
            ybuf.at[slot, pl.ds(pl.multiple_of(row * cpr, cpr), cpr)],
            y_hbm.at[pl.ds(pl.multiple_of(v * cpr, cpr), cpr)],
            ssem.at[slot])

    def wait_gather(slot):
        pltpu.make_async_copy(h_hbm.at[pl.ds(0, tile_rows)], xbuf.at[slot], gsem.at[slot]).wait()

    def wait_scatter(slot):
        pltpu.make_async_copy(ybuf.at[slot], y_hbm.at[pl.ds(0, tile_rows)], ssem.at[slot]).wait()

    def dump_fill(t):
        return pltpu.make_async_copy(
            ybuf.at[1], y_hbm.at[pl.ds((n_flat + t * tme) * cpr, tile_rows)], zsem)

    @pl.when((f == 0) & (j == 0))
    def _():
        ybuf[1] = jnp.zeros(ybuf.shape[1:], ybuf.dtype)
        for t in range(n_dump_tiles):
            dump_fill(t).start()
        for t in range(n_dump_tiles):
            dump_fill(t).wait()

        def body(r, carry):
            gather_row(0, 0, r).start()
            return carry

        lax.fori_loop(0, tme, body, 0)

    def step(par, first, last):
        if last:
            pl.when(j >= 1)(lambda: wait_scatter(par))
        if first:
            wait_gather(par)
            for s in range(cpr):
                xb16[:, s * LANES:(s + 1) * LANES] = (
                    xbuf[par, pl.ds(s, tme, stride=cpr), :].astype(BF16))
        for t in range(rows_per_step):
            gather_row(j + 1, 1 - par, t * n_f + f).start()
            scatter_row(j - 1, 1 - par, t * n_f + f).start()
        x = xb16[...]
        a = jnp.dot(x, wg_ref[...].astype(BF16), preferred_element_type=F32)
        b = jnp.dot(x, wu_ref[...].astype(BF16), preferred_element_type=F32)
        act = (a * jax.nn.sigmoid(a) * b).astype(BF16)
        part = jnp.dot(act, wd_ref[...].astype(BF16), preferred_element_type=F32)
        total = part if first else acc_ref[...] + part
        if not last:
            acc_ref[...] = total
        else:
            for s in range(cpr):
                ybuf[par, pl.ds(s, tme, stride=cpr), :] = total[:, s * LANES:(s + 1) * LANES]

    for par in range(2):
        on = (j < n_active) & (j % 2 == par)
        pl.when(on & (f == 0))(functools.partial(step, par, True, False))
        pl.when(on & (f > 0) & (f < n_f - 1))(functools.partial(step, par, False, False))
        pl.when(on & (f == n_f - 1))(functools.partial(step, par, False, True))

        @pl.when((j == n_active) & (f == 0) & (j % 2 == par))
        def _(par=par):
            def body(r, carry):
                scatter_row(j - 1, 1 - par, r).start()
                return carry

            lax.fori_loop(0, tme, body, 0)
            wait_gather(par)
            wait_scatter(par)
            wait_scatter(1 - par)


def _moe(tile_expert, n_active, row_flat, h_rows, wg, wu, wd, d, n_flat, n_dump_tiles):
    n_steps = tile_expert.shape[0]
    d_exp = wg.shape[2]
    n_f = d_exp // MOE_F_CHUNK
    cpr = d // LANES
    tile_rows = MOE_ROW_TILE * cpr
    assert MOE_ROW_TILE % n_f == 0 and n_f >= 2

    def f_idx(j, f, na):
        return jnp.where(j < na[0], f, n_f - 1)

    grid_spec = pltpu.PrefetchScalarGridSpec(
        num_scalar_prefetch=3,
        grid=(n_steps, n_f),
        in_specs=[
            pl.BlockSpec(memory_space=pl.ANY),
            pl.BlockSpec((None, d, MOE_F_CHUNK), lambda j, f, te, na, rf: (te[j], 0, f_idx(j, f, na))),
            pl.BlockSpec((None, d, MOE_F_CHUNK), lambda j, f, te, na, rf: (te[j], 0, f_idx(j, f, na))),
            pl.BlockSpec((None, MOE_F_CHUNK, d), lambda j, f, te, na, rf: (te[j], f_idx(j, f, na), 0)),
        ],
        out_specs=pl.BlockSpec(memory_space=pl.ANY),
        scratch_shapes=[
            pltpu.VMEM((2, tile_rows, LANES), F32),
            pltpu.VMEM((2, tile_rows, LANES), F32),
            pltpu.VMEM((MOE_ROW_TILE, d), BF16),
            pltpu.VMEM((MOE_ROW_TILE, d), F32),
            pltpu.SemaphoreType.DMA((2,)),
            pltpu.SemaphoreType.DMA((2,)),
            pltpu.SemaphoreType.DMA,
        ],
    )
    kernel = functools.partial(_moe_kernel, cpr=cpr, rows_per_step=MOE_ROW_TILE // n_f,
                               n_flat=n_flat, n_dump_tiles=n_dump_tiles)
    return pl.pallas_call(
        kernel,
        grid_spec=grid_spec,
        out_shape=jax.ShapeDtypeStruct(((n_flat + n_dump_tiles * MOE_ROW_TILE) * cpr, LANES), F32),
        compiler_params=_params(2),
        name="moe",
    )(tile_expert, n_active, row_flat, h_rows, wg, wu, wd)


def _combine_kernel(y_ref, x2_ref, g0_ref, g1_ref, mod_ref, *o_refs, cpr, n_prompt_tiles):
    i = pl.program_id(0)
    tm = x2_ref.shape[0]
    gate2 = mod_ref[...][5:6]
    stride = TOP_K * cpr

    def write(o_ref):
        g0 = g0_ref[...]
        g1 = g1_ref[...]
        for s in range(cpr):
            lanes = slice(s * LANES, (s + 1) * LANES)
            ya = y_ref[pl.ds(s, tm, stride=stride), :]
            yb = y_ref[pl.ds(cpr + s, tm, stride=stride), :]
            o_ref[:, lanes] = x2_ref[:, lanes] + gate2[:, lanes] * (g0 * ya + g1 * yb)

    if len(o_refs) == 1:
        write(o_refs[0])
    else:
        pl.when(i < n_prompt_tiles)(lambda: write(o_refs[0]))
        pl.when(i >= n_prompt_tiles)(lambda: write(o_refs[1]))


def _combine(y_rows, x2, g0, g1, mod, l, cond_row, n_prompt, split_output):
    n, d = x2.shape
    cpr = d // LANES
    n_prompt_tiles = n_prompt // TOKEN_TILE
    kernel = functools.partial(_combine_kernel, cpr=cpr, n_prompt_tiles=n_prompt_tiles)
    if split_output:
        out_specs = [
            pl.BlockSpec((TOKEN_TILE, d), lambda i: (jnp.minimum(i, n_prompt_tiles - 1), 0)),
            pl.BlockSpec((TOKEN_TILE, d), lambda i: (jnp.maximum(i - n_prompt_tiles, 0), 0))]
        out_shape = [jax.ShapeDtypeStruct((n_prompt, d), F32),
                     jax.ShapeDtypeStruct((n - n_prompt, d), F32)]
    else:
        out_specs = pl.BlockSpec((TOKEN_TILE, d), lambda i: (i, 0))
        out_shape = jax.ShapeDtypeStruct((n, d), F32)
    return pl.pallas_call(
        kernel,
        grid=(n // TOKEN_TILE,),
        in_specs=[
            pl.BlockSpec((TOKEN_TILE * TOP_K * cpr, LANES), lambda i: (i, 0)),
            pl.BlockSpec((TOKEN_TILE, d), lambda i: (i, 0)),
            pl.BlockSpec((TOKEN_TILE, LANES), lambda i: (i, 0)),
            pl.BlockSpec((TOKEN_TILE, LANES), lambda i: (i, 0)),
            pl.BlockSpec((None, None, N_MOD, d), lambda i: (l, cond_row(i), 0, 0)),
        ],
        out_specs=out_specs,
        out_shape=out_shape,
        compiler_params=_params(),
        name="combine",
    )(y_rows, x2, g0, g1, mod)


def _dispatch_plan(expert_ids, ranks, counts, n_tiles_max):
    tme = MOE_ROW_TILE
    n_experts = counts.shape[0]
    n = expert_ids[0].shape[0]
    n_flat = TOP_K * n
    tiles = (counts + tme - 1) // tme
    tile_end = jnp.cumsum(tiles)
    row_off = (tile_end - tiles) * tme
    token = jnp.arange(n, dtype=jnp.int32)
    pos, flat = [], []
    for k in range(TOP_K):
        off = jnp.sum(jnp.where(expert_ids[k][:, None] == jnp.arange(n_experts)[None, :],
                                row_off[None, :], 0), axis=1)
        pos.append(tme + off + ranks[k])
        flat.append(TOP_K * token + k)
    n_active = tile_end[-1]
    n_steps = n_tiles_max + 1
    tile_ids = jnp.minimum(jnp.arange(n_steps), n_active - 1)
    tile_expert = jnp.sum(tile_ids[:, None] >= tile_end[None, :], axis=1).astype(jnp.int32)
    n_dump_tiles = n_experts + 1
    rf = jnp.full(((n_steps + 1) * tme,), -1, jnp.int32).at[jnp.concatenate(pos)].set(
        jnp.concatenate(flat), unique_indices=True)
    is_pad = rf < 0
    pad_slot = jnp.minimum(jnp.cumsum(is_pad.astype(jnp.int32)) - 1, n_dump_tiles * tme - 1)
    rf = jnp.where(is_pad, n_flat + pad_slot, rf)
    return rf, tile_expert, n_active.astype(jnp.int32).reshape(1), n_dump_tiles


def kernel(x_prompt, x_sample, cache_k, cache_v, c, c_ctx, norm1_g, norm2_g, w_ada, b_ada, w_in,
           q_norm_g, k_norm_g, w_pool, pool_scale, rpb, w_out, ffn_w_gate, ffn_w_up, ffn_w_down,
           moe_router, moe_w_gate, moe_w_up, moe_w_down):
    batch, seq, d = x_prompt.shape
    dec_batch, dec_seq, _ = x_sample.shape
    depth = w_in.shape[0]
    heads = cache_k.shape[3]
    na_w = heads * HEAD_DIM
    pool_w = d - na_w
    n_prompt = batch * seq
    n = n_prompt + dec_batch * dec_seq
    n_experts = moe_router.shape[2]
    assert pool_w == len(POOL_WINDOWS) * LANES and na_w % LANES == 0
    assert TOKEN_TILE % seq == 0 and n_prompt % dec_seq == 0 and dec_seq % TOKEN_TILE == 0
    assert dec_seq // GRID_W >= NA_WIN_R and dec_batch < COND_ROWS
    n_prompt_tiles = n_prompt // TOKEN_TILE
    seq_info = dict(n_prompt_tiles=n_prompt_tiles, seq=seq, dec_seq=dec_seq)

    def cond_row(i):
        start = i * TOKEN_TILE
        return jnp.where(start < n_prompt, dec_batch, (start - n_prompt) // dec_seq)

    cond = jnp.zeros((COND_ROWS, d), F32).at[:dec_batch].set(c).at[dec_batch].set(c_ctx)
    mod = _ada(cond, w_ada, b_ada).reshape(depth, COND_ROWS, N_MOD, d)

    hsum = jnp.asarray(np.kron(np.eye(heads), np.ones((HEAD_DIM, HEAD_DIM))), BF16)
    ctx_k = cache_k.reshape(dec_batch, depth, cache_k.shape[2] * heads, HEAD_DIM)
    ctx_v = cache_v.reshape(dec_batch, depth, cache_v.shape[2] * heads, HEAD_DIM)

    x = (x_prompt.reshape(n_prompt, d), x_sample.reshape(-1, d))
    cache_kv = None
    cache_shape = (batch, depth, seq * heads, HEAD_DIM)
    for l in range(depth):
        pu, q, k, v, *cache_kv = _premix(
            x, mod, l, norm1_g[l][None], w_in[l].astype(BF16),
            jnp.tile(q_norm_g[l], heads)[None], jnp.tile(k_norm_g[l], heads)[None], hsum,
            n_prompt, cond_row, cache_kv, cache_shape, seq)
        attn = _attention(q, k, v, ctx_k, ctx_v, _relative_bias_table(rpb[l]), l,
                          n_prompt, seq, dec_seq)
        mix_args = (mod, l, w_pool[l].astype(BF16), pool_scale[l][None], w_out[l].astype(BF16),
                    norm2_g[l][None])
        li = l // 2
        if l % 2 == 0:
            assert ffn_w_gate.shape[2] % FF_CHUNK == 0
            x = _postmix_ffn(x, pu, attn, *mix_args, ffn_w_gate[li].astype(BF16),
                             ffn_w_up[li].astype(BF16), ffn_w_down[li].astype(BF16),
                             seq_info, cond_row)
        else:
            router = jnp.zeros((d, ROUTER_LANES), F32).at[:, :n_experts].set(moe_router[li])
            r_hi = router.astype(BF16)
            r_lo = (router - r_hi.astype(F32)).astype(BF16)
            x2, h_rows, rout, g0, g1, counts = _postmix_router(
                x, pu, attn, *mix_args, r_hi, r_lo, n_experts, seq_info, cond_row)
            routing = [rout[t].astype(jnp.int32) for t in range(2 * TOP_K)]
            n_tiles_max = (TOP_K * n + n_experts * (MOE_ROW_TILE - 1)) // MOE_ROW_TILE
            row_flat, tile_expert, n_active, n_dump_tiles = _dispatch_plan(
                routing[:TOP_K], routing[TOP_K:], counts[0, :n_experts].astype(jnp.int32),
                n_tiles_max)
            y_rows = _moe(tile_expert, n_active, row_flat, h_rows,
                          moe_w_gate[li], moe_w_up[li], moe_w_down[li], d, TOP_K * n, n_dump_tiles)
            x = _combine(y_rows, x2, g0, g1, mod, l, cond_row, n_prompt,
                         split_output=(l == depth - 1))

    if not isinstance(x, tuple):
        x = (x[:n_prompt], x[n_prompt:])
    y_prompt = x[0].reshape(batch, seq, d)
    y_sample = x[1].reshape(dec_batch, dec_seq, d)
    new_k, new_v = (a.reshape(batch, depth, seq, heads, HEAD_DIM) for a in cache_kv)
    return (y_prompt, y_sample, new_k, new_v)
```

```python
import functools

import numpy as np
import jax
import jax.numpy as jnp
from jax import lax
from jax.experimental import pallas as pl
from jax.experimental.pallas import tpu as pltpu

F32 = jnp.float32
BF16 = jnp.bfloat16

GRID_W = 64
POOL_WINDOWS = (2, 4, 8, 16)
HEAD_DIM = 64
NA_WIN_R = 8
NA_WIN_C = 16
N_MOD = 6
TOP_K = 2
NORM_EPS = 1e-6
LOG2_E = 1.4426950408889634

LANES = 128
SUBLANES = 8
VMEM_LIMIT_BYTES = 56 * 1024 * 1024

TOKEN_TILE = 512
POOL_HALO = 8
FF_CHUNK = 256
MOE_ROW_TILE = 1008
MOE_F_CHUNK = 512
ADA_COL_TILE = 1024
COND_ROWS = 16
ROUTER_LANES = 128


def _params(n_axes=1):
    return pltpu.CompilerParams(
        dimension_semantics=("arbitrary",) * n_axes,
        vmem_limit_bytes=VMEM_LIMIT_BYTES,
    )


def _resident(shape, index_map):
    return pl.BlockSpec(shape, index_map, pipeline_mode=pl.Buffered(1))


def _x_specs(x, n_prompt_tiles):
    if not isinstance(x, tuple):
        return [pl.BlockSpec((TOKEN_TILE, x.shape[1]), lambda i: (i, 0))]
    d = x[0].shape[1]
    return [pl.BlockSpec((TOKEN_TILE, d), lambda i: (jnp.minimum(i, n_prompt_tiles - 1), 0)),
            pl.BlockSpec((TOKEN_TILE, d), lambda i: (jnp.maximum(i - n_prompt_tiles, 0), 0))]


def _load_x(i, x_refs, n_prompt_tiles):
    if len(x_refs) == 1:
        return x_refs[0][...]
    return jnp.where(i < n_prompt_tiles, x_refs[0][...], x_refs[1][...])


def _as_tuple(x):
    return x if isinstance(x, tuple) else (x,)


def _ada_kernel(c_ref, w_ref, b_ref, o_ref):
    c = c_ref[...]
    s = c * jax.nn.sigmoid(c)
    o_ref[...] = jnp.dot(s.astype(BF16), w_ref[...].astype(BF16),
                         preferred_element_type=F32) + b_ref[...]


def _ada(cond, w_ada, b_ada):
    depth, d, width = w_ada.shape
    return pl.pallas_call(
        _ada_kernel,
        grid=(depth, width // ADA_COL_TILE),
        in_specs=[
            pl.BlockSpec((COND_ROWS, d), lambda l, j: (0, 0)),
            pl.BlockSpec((None, d, ADA_COL_TILE), lambda l, j: (l, 0, j)),
            pl.BlockSpec((None, 1, ADA_COL_TILE), lambda l, j: (l, 0, j)),
        ],
        out_specs=pl.BlockSpec((None, COND_ROWS, ADA_COL_TILE), lambda l, j: (l, 0, j)),
        out_shape=jax.ShapeDtypeStruct((depth, COND_ROWS, width), F32),
        compiler_params=_params(2),
        name="ada",
    )(cond, w_ada, b_ada.reshape(depth, 1, width))


def _premix_kernel(*refs, n_x, n_prompt_tiles, pool_w, na_w, seq, layer, creates_cache):
    n_in = n_x + (6 if creates_cache else 8)
    mod_ref, g_ref, w_ref, qg_ref, kg_ref, hsum_ref = refs[n_x:n_x + 6]
    pu_ref, q_ref, k_ref, v_ref, kf_ref, vf_ref = refs[n_in:]
    i = pl.program_id(0)
    x = _load_x(i, refs[:n_x], n_prompt_tiles)
    inv = lax.rsqrt(jnp.mean(x * x, axis=-1, keepdims=True) + NORM_EPS)
    mod = mod_ref[...]
    h = (x * inv * g_ref[...]) * (1.0 + mod[1:2]) + mod[0:1]
    u = jnp.dot(h.astype(BF16), w_ref[...], preferred_element_type=F32)
    pu_ref[...] = u[:, :pool_w]
    q = u[:, pool_w:pool_w + na_w]
    k = u[:, pool_w + na_w:pool_w + 2 * na_w]
    v = u[:, pool_w + 2 * na_w:]

    def head_norm(t, g):
        ms = jnp.dot((t * t).astype(BF16), hsum_ref[...],
                     preferred_element_type=F32) * (1.0 / HEAD_DIM)
        return t * lax.rsqrt(ms + NORM_EPS) * g

    qn = head_norm(q, qg_ref[...])
    kn = head_norm(k, kg_ref[...])
    q_ref[...] = (qn * (HEAD_DIM ** -0.5 * LOG2_E)).astype(BF16)
    k_ref[...] = kn.astype(BF16)
    v_ref[...] = v.astype(BF16)

    @pl.when(i < n_prompt_tiles)
    def _():
        heads = na_w // HEAD_DIM
        if creates_cache:
            for lz in range(kf_ref.shape[1]):
                if lz != layer:
                    kf_ref[:, lz] = jnp.zeros((kf_ref.shape[0],) + kf_ref.shape[2:], F32)
                    vf_ref[:, lz] = jnp.zeros((vf_ref.shape[0],) + vf_ref.shape[2:], F32)
            kf_l, vf_l = kf_ref.at[:, layer], vf_ref.at[:, layer]
        else:
            kf_l, vf_l = kf_ref, vf_ref
        for b in range(TOKEN_TILE // seq):
            for hd in range(heads):
                rows, cols = slice(b * seq, (b + 1) * seq), slice(hd * HEAD_DIM, (hd + 1) * HEAD_DIM)
                kf_l[b, pl.ds(hd, seq, stride=heads), :] = kn[rows, cols]
                vf_l[b, pl.ds(hd, seq, stride=heads), :] = v[rows, cols]


def _premix(x, mod, l, norm_g, w_in, q_g, k_g, hsum, n_prompt, cond_row, cache_kv, cache_shape,
            seq):
    xs = _as_tuple(x)
    creates_cache = cache_kv is None
    n, d = sum(a.shape[0] for a in xs), xs[0].shape[1]
    pool_w = d // 2
    na_w = d - pool_w
    n_tiles = n // TOKEN_TILE
    n_prompt_tiles = n_prompt // TOKEN_TILE
    last_p = n_prompt_tiles - 1
    tok = lambda w: pl.BlockSpec((TOKEN_TILE, w), lambda i: (i, 0))
    if creates_cache:
        cache_spec = pl.BlockSpec((TOKEN_TILE // seq,) + cache_shape[1:],
                                  lambda i: (jnp.minimum(i, last_p), 0, 0, 0))
        cache_in, cache_in_specs, aliases = (), [], {}
    else:
        cache_spec = pl.BlockSpec((TOKEN_TILE // seq, None) + cache_shape[2:],
                                  lambda i: (jnp.minimum(i, last_p), l, 0, 0))
        cache_in, cache_in_specs = tuple(cache_kv), [pl.BlockSpec(memory_space=pl.ANY)] * 2
        aliases = {len(xs) + 6: 4, len(xs) + 7: 5}
    kernel = functools.partial(_premix_kernel, n_x=len(xs), n_prompt_tiles=n_prompt_tiles,
                               pool_w=pool_w, na_w=na_w, seq=seq, layer=l,
                               creates_cache=creates_cache)
    return pl.pallas_call(
        kernel,
        grid=(n_tiles,),
        in_specs=_x_specs(x, n_prompt_tiles) + [
            pl.BlockSpec((None, None, N_MOD, d), lambda i: (l, cond_row(i), 0, 0)),
            pl.BlockSpec((1, d), lambda i: (0, 0)),
            _resident((d, w_in.shape[1]), lambda i: (0, 0)),
            pl.BlockSpec((1, na_w), lambda i: (0, 0)),
            pl.BlockSpec((1, na_w), lambda i: (0, 0)),
            _resident((na_w, na_w), lambda i: (0, 0)),
        ] + cache_in_specs,
        out_specs=[tok(pool_w), tok(na_w), tok(na_w), tok(na_w), cache_spec, cache_spec],
        out_shape=[
            jax.ShapeDtypeStruct((n, pool_w), F32),
            jax.ShapeDtypeStruct((n, na_w), BF16),
            jax.ShapeDtypeStruct((n, na_w), BF16),
            jax.ShapeDtypeStruct((n, na_w), BF16),
            jax.ShapeDtypeStruct(cache_shape, F32),
            jax.ShapeDtypeStruct(cache_shape, F32),
        ],
        input_output_aliases=aliases,
        compiler_params=_params(),
        name=f"premix{l}",
    )(*xs, mod, norm_g, w_in, q_g, k_g, hsum, *cache_in)


_NT = (((1,), (1,)), ((), ()))
_TN = (((0,), (0,)), ((), ()))


def _block_diag_queries(q2):
    lo = lax.broadcasted_iota(jnp.int32, q2.shape, 1) < HEAD_DIM
    zero = jnp.zeros_like(q2)
    return jnp.concatenate([jnp.where(lo, q2, zero), jnp.where(lo, zero, q2)], axis=0)


def _pick_head_blocks(o, nq):
    lo = lax.broadcasted_iota(jnp.int32, (nq, LANES), 1) < HEAD_DIM
    return jnp.where(lo, o[:nq], o[nq:])


def _pair_attention(q2, k, v):
    nq = q2.shape[0]
    s = lax.dot_general(k, _block_diag_queries(q2), _NT, preferred_element_type=F32)
    p = jnp.exp2(s - jnp.max(s, axis=0, keepdims=True))
    r = 1.0 / jnp.sum(p, axis=0, keepdims=True)
    o = lax.dot_general((p * r).astype(BF16), v, _TN, preferred_element_type=F32)
    return _pick_head_blocks(o, nq)


def _attn_kernel(q_ref, k_ref, v_ref, ckf_ref, cvf_ref, bias_ref, o_ref, s_ref, p_ref,
                 ck_ref, cv_ref, *, n_prompt_tiles, seq, dec_seq, n_pairs):
    i = pl.program_id(0)
    tiles_per_seq = dec_seq // TOKEN_TILE
    rows_per_tile = TOKEN_TILE // GRID_W
    rows = dec_seq // GRID_W
    win_keys = NA_WIN_R * GRID_W
    block_off = (i * TOKEN_TILE) % dec_seq

    @pl.when(i < n_prompt_tiles)
    def _():
        for s in range(TOKEN_TILE // seq):
            start = pl.multiple_of(block_off + s * seq, seq)
            for hp in range(n_pairs):
                lanes = slice(hp * LANES, (hp + 1) * LANES)
                out = _pair_attention(q_ref[s * seq:(s + 1) * seq, lanes],
                                      k_ref[pl.ds(start, seq), lanes],
                                      v_ref[pl.ds(start, seq), lanes])
                o_ref[s * seq:(s + 1) * seq, lanes] = out.astype(o_ref.dtype)

    @pl.when(i >= n_prompt_tiles)
    def _():
        tile_in_seq = (i - n_prompt_tiles) % tiles_per_seq
        row0 = tile_in_seq * rows_per_tile

        @pl.when(tile_in_seq == 0)
        def _():
            past = ck_ref.shape[0]
            heads = 2 * n_pairs
            for src, dst in ((ckf_ref, ck_ref), (cvf_ref, cv_ref)):
                for hp in range(n_pairs):
                    pair = [src[pl.ds(2 * hp + t, past, stride=heads), :] for t in range(2)]
                    dst[:, hp * LANES:(hp + 1) * LANES] = (
                        jnp.concatenate(pair, axis=-1).astype(BF16))

        def indices(rl):
            r = row0 + rl
            r0 = jnp.clip(r - NA_WIN_R // 2, 0, rows - NA_WIN_R)
            return r0 - r + NA_WIN_R - 1, pl.multiple_of(r0 * GRID_W, GRID_W)

        def scores(rl):
            d0, kstart = indices(rl)
            for hp in range(n_pairs):
                lanes = slice(hp * LANES, (hp + 1) * LANES)
                qbd = _block_diag_queries(q_ref[rl * GRID_W:(rl + 1) * GRID_W, lanes])
                bias = bias_ref[hp, pl.ds(d0, NA_WIN_R)].reshape(win_keys, LANES)
                s_ref[rl % 2, hp, 0:win_keys] = lax.dot_general(
                    k_ref[pl.ds(kstart, win_keys), lanes], qbd, _NT,
                    preferred_element_type=F32) + bias
                s_ref[rl % 2, hp, win_keys:] = lax.dot_general(
                    ck_ref[:, lanes], qbd, _NT, preferred_element_type=F32)

        def softmax(rl):
            for hp in range(n_pairs):
                s = s_ref[rl % 2, hp]
                p = jnp.exp2(s - jnp.max(s, axis=0, keepdims=True))
                rr = 1.0 / jnp.sum(p, axis=0, keepdims=True)
                p_ref[rl % 2, hp] = (p * rr).astype(BF16)

        def values(rl):
            _, kstart = indices(rl)
            for hp in range(n_pairs):
                lanes = slice(hp * LANES, (hp + 1) * LANES)
                o = (lax.dot_general(p_ref[rl % 2, hp, 0:win_keys],
                                     v_ref[pl.ds(kstart, win_keys), lanes],
                                     _TN, preferred_element_type=F32)
                     + lax.dot_general(p_ref[rl % 2, hp, win_keys:], cv_ref[:, lanes], _TN,
                                       preferred_element_type=F32))
                o_ref[rl * GRID_W:(rl + 1) * GRID_W, lanes] = (
                    _pick_head_blocks(o, GRID_W).astype(o_ref.dtype))

        for step in range(rows_per_tile + 2):
            if step < rows_per_tile:
                scores(step)
            if 1 <= step <= rows_per_tile:
                softmax(step - 1)
            if step >= 2:
                values(step - 2)


def _attention(q, k, v, ctx_k, ctx_v, bias, l, n_prompt, seq, dec_seq):
    n, na_w = q.shape
    n_tiles = n // TOKEN_TILE
    n_prompt_tiles = n_prompt // TOKEN_TILE
    tiles_per_seq = dec_seq // TOKEN_TILE
    heads = na_w // HEAD_DIM
    past = ctx_k.shape[2] // heads
    n_pairs = na_w // LANES
    n_keys = NA_WIN_R * GRID_W + past
    kv_spec = pl.BlockSpec((dec_seq, na_w), lambda i: (i * TOKEN_TILE // dec_seq, 0))
    ctx_spec = pl.BlockSpec(
        (None, None, past * heads, HEAD_DIM),
        lambda i: (jnp.maximum(i - n_prompt_tiles, 0) // tiles_per_seq, l, 0, 0))
    kernel = functools.partial(_attn_kernel, n_prompt_tiles=n_prompt_tiles, seq=seq,
                               dec_seq=dec_seq, n_pairs=n_pairs)
    return pl.pallas_call(
        kernel,
        grid=(n_tiles,),
        in_specs=[
            pl.BlockSpec((TOKEN_TILE, na_w), lambda i: (i, 0)),
            kv_spec, kv_spec, ctx_spec, ctx_spec,
            _resident(bias.shape, lambda i: (0, 0, 0, 0)),
        ],
        out_specs=pl.BlockSpec((TOKEN_TILE, na_w), lambda i: (i, 0)),
        out_shape=jax.ShapeDtypeStruct((n, na_w), BF16),
        scratch_shapes=[pltpu.VMEM((2, n_pairs, n_keys, LANES), F32),
                        pltpu.VMEM((2, n_pairs, n_keys, LANES), BF16),
                        pltpu.VMEM((past, na_w), BF16),
                        pltpu.VMEM((past, na_w), BF16)],
        compiler_params=_params(),
        name=f"attn{l}",
    )(q, k, v, ctx_k, ctx_v, bias)


def _relative_bias_table(rpb_l):
    heads = rpb_l.shape[0]
    kc = np.arange(GRID_W)[:, None]
    qc = np.arange(GRID_W)[None, :]
    q_start = np.clip(qc - NA_WIN_C // 2, 0, GRID_W - NA_WIN_C)
    valid = (kc >= q_start) & (kc < q_start + NA_WIN_C)
    dc_idx = np.clip(kc - qc, -(NA_WIN_C - 1), NA_WIN_C - 1) + NA_WIN_C - 1
    t = jnp.zeros(rpb_l.shape[:2] + dc_idx.shape, F32)
    for c in range(rpb_l.shape[2]):
        t = jnp.where(dc_idx[None, None] == c, rpb_l[:, :, c, None, None].astype(F32), t)
    t = jnp.where(valid[None, None], t * LOG2_E, -jnp.inf)
    t = t.reshape(heads // 2, 2, 2 * NA_WIN_R - 1, GRID_W, GRID_W)
    return t.transpose(0, 2, 3, 1, 4).reshape(heads // 2, 2 * NA_WIN_R - 1, GRID_W, 2 * GRID_W)


def _mod_static(t, m):
    return t & (m - 1) if m & (m - 1) == 0 else lax.rem(t, m)


def _pool_mix(i, b, pu_ref, pp_ref, pn_ref, ext_ref, wp_ref, ps_ref, *, n_prompt_tiles, seq, dec_seq):
    sub = seq
    h = POOL_HALO
    is_prompt = i < n_prompt_tiles
    seq_len = jnp.where(is_prompt, seq, dec_seq)
    lo, hi = b * sub, (b + 1) * sub
    tok0 = i * TOKEN_TILE + lo
    pos0 = jnp.where(is_prompt, _mod_static(tok0, seq), _mod_static(tok0, dec_seq))
    prev = pp_ref[...] if lo == 0 else pu_ref[lo - h:lo, :]
    nxt = pn_ref[...] if hi == TOKEN_TILE else pu_ref[hi:hi + h, :]
    ext_ref[b, 0:h] = jnp.where(pos0 != 0, prev, 0.0)
    ext_ref[b, h:h + sub] = pu_ref[lo:hi, :]
    ext_ref[b, h + sub:] = jnp.where(pos0 + sub != seq_len, nxt, 0.0)
    left = pos0 + lax.broadcasted_iota(jnp.int32, (sub, LANES), 0)
    right = seq_len - left
    outs = []
    for g, window in enumerate(POOL_WINDOWS):
        half = window // 2
        lanes = slice(g * LANES, (g + 1) * LANES)
        total = ext_ref[b, h - half:h - half + sub, lanes]
        for j in range(1 - half, half):
            total = total + ext_ref[b, h + j:h + j + sub, lanes]
        count = (jnp.minimum(left, half) + jnp.minimum(right, half)).astype(F32)
        pooled = total / count - pu_ref[lo:hi, lanes]
        outs.append(jnp.dot(pooled.astype(BF16), wp_ref[g], preferred_element_type=F32))
    return jnp.concatenate(outs, axis=-1) * ps_ref[...]


def _postmix_chain(i, x_refs, pu_ref, pp_ref, pn_ref, at_ref, mod_ref, wp_ref, ps_ref, wo_ref,
                   g2_ref, ext_ref, tail, **seq_info):
    sub = seq_info["seq"]
    n_sub = TOKEN_TILE // sub
    mod = mod_ref[...]
    is_prompt = i < seq_info["n_prompt_tiles"]

    def rows(b):
        return slice(b * sub, (b + 1) * sub)

    def pool_stage(b, _):
        return _pool_mix(i, b, pu_ref, pp_ref, pn_ref, ext_ref, wp_ref, ps_ref, **seq_info)

    def mix_stage(b, pool_out):
        mixed = jnp.concatenate([pool_out.astype(BF16), at_ref[rows(b), :]], axis=-1)
        mix = jnp.dot(mixed, wo_ref[...], preferred_element_type=F32)
        if len(x_refs) == 1:
            x = x_refs[0][rows(b), :]
        else:
            x = jnp.where(is_prompt, x_refs[0][rows(b), :], x_refs[1][rows(b), :])
        return x + mod[2:3] * mix

    def norm_stage(b, x2):
        inv = lax.rsqrt(jnp.mean(x2 * x2, axis=-1, keepdims=True) + NORM_EPS)
        return x2, (x2 * inv * g2_ref[...]) * (1.0 + mod[4:5]) + mod[3:4]

    def tail_stage(b, x2_h2):
        tail(rows(b), *x2_h2)

    stages = (pool_stage, mix_stage, norm_stage, tail_stage)
    state = [None] * n_sub
    for step in range(n_sub + len(stages) - 1):
        for b in range(n_sub):
            if 0 <= step - b < len(stages):
                state[b] = stages[step - b](b, state[b])
    return mod[5:6]


def _postmix_ffn_kernel(*refs, n_x, **seq_info):
    (pu_ref, pp_ref, pn_ref, at_ref, mod_ref, wp_ref, ps_ref, wo_ref,
     g2_ref, wg_ref, wu_ref, wd_ref, y_ref, ext_ref, h_ref, acc_ref) = refs[n_x:]
    i = pl.program_id(0)

    def tail(rows, x2, h2):
        y_ref[rows, :] = x2
        h_ref[rows, :] = h2.astype(BF16)

    gate2 = _postmix_chain(i, refs[:n_x], pu_ref, pp_ref, pn_ref, at_ref, mod_ref, wp_ref,
                           ps_ref, wo_ref, g2_ref, ext_ref, tail, **seq_info)
    acc_ref[...] = jnp.zeros_like(acc_ref)

    h = h_ref[...]
    for c in range(wg_ref.shape[1] // FF_CHUNK):
        cols = slice(c * FF_CHUNK, (c + 1) * FF_CHUNK)
        a = jnp.dot(h, wg_ref[:, cols], preferred_element_type=F32)
        b = jnp.dot(h, wu_ref[:, cols], preferred_element_type=F32)
        act = (a * jax.nn.sigmoid(a) * b).astype(BF16)
        acc_ref[...] += jnp.dot(act, wd_ref[cols, :], preferred_element_type=F32)
    y_ref[...] = y_ref[...] + gate2 * acc_ref[...]


def _postmix_router_kernel(*refs, n_x, n_experts, **seq_info):
    (pu_ref, pp_ref, pn_ref, at_ref, mod_ref, wp_ref, ps_ref, wo_ref,
     g2_ref, rh_ref, rl_ref, tri_ref, x2_ref, hr_ref, rout_ref, g0_ref, g1_ref, cnt_ref,
     ext_ref) = refs[n_x:]
    i = pl.program_id(0)

    @pl.when(i == 0)
    def _():
        cnt_ref[...] = jnp.zeros_like(cnt_ref)

    def tail(rows, x2, h2):
        x2_ref[rows, :] = x2
        n_rows = rows.stop - rows.start
        for s in range(h2.shape[1] // LANES):
            hr_ref[pl.ds(rows.start * SUBLANES + s, n_rows, stride=SUBLANES), :] = (
                h2[:, s * LANES:(s + 1) * LANES])
        hi = h2.astype(BF16)
        lo = (h2 - hi.astype(F32)).astype(BF16)
        logits = (jnp.dot(hi, rh_ref[...], preferred_element_type=F32)
                  + jnp.dot(lo, rh_ref[...], preferred_element_type=F32)
                  + jnp.dot(hi, rl_ref[...], preferred_element_type=F32))
        lane = lax.broadcasted_iota(jnp.int32, logits.shape, 1).astype(F32)
        logits = jnp.where(lane < n_experts, logits, -jnp.inf)
        m1 = jnp.max(logits, axis=-1, keepdims=True)
        i1 = jnp.min(jnp.where(logits == m1, lane, float(ROUTER_LANES)), axis=-1, keepdims=True)
        rest = jnp.where(lane == i1, -jnp.inf, logits)
        m2 = jnp.max(rest, axis=-1, keepdims=True)
        i2 = jnp.min(jnp.where(rest == m2, lane, float(ROUTER_LANES)), axis=-1, keepdims=True)
        e = jnp.exp(m2 - m1)
        pick0 = jnp.where(lane == i1, 1.0, 0.0)
        pick1 = jnp.where(lane == i2, 1.0, 0.0)
        picks = pick0 + pick1
        before = cnt_ref[...] + jnp.dot(tri_ref[...], picks.astype(BF16),
                                        preferred_element_type=F32)
        rank0 = jnp.sum(before * pick0, axis=-1, keepdims=True)
        rank1 = jnp.sum(before * pick1, axis=-1, keepdims=True)
        cnt_ref[...] = before[n_rows - 1:n_rows, :] + picks[n_rows - 1:n_rows, :]
        packed = jnp.where(lane == 0, i1, jnp.where(lane == 1, i2, jnp.where(
            lane == 2, rank0, jnp.where(lane == 3, rank1, 0.0))))
        rout_ref[:, rows] = packed.T[:SUBLANES]
        g0_ref[rows, :] = jnp.broadcast_to(1.0 / (1.0 + e), (n_rows, LANES))
        g1_ref[rows, :] = jnp.broadcast_to(e / (1.0 + e), (n_rows, LANES))

    _postmix_chain(i, refs[:n_x], pu_ref, pp_ref, pn_ref, at_ref, mod_ref, wp_ref,
                   ps_ref, wo_ref, g2_ref, ext_ref, tail, **seq_info)


def _pool_scratch_shape(seq_info, pool_w):
    sub = seq_info["seq"]
    return (TOKEN_TILE // sub, sub + 2 * POOL_HALO, pool_w)


def _postmix_specs(x, n, d, pool_w, na_w, l, cond_row, n_prompt_tiles):
    n_halo_blocks = n // POOL_HALO
    per_tile = TOKEN_TILE // POOL_HALO
    tok = lambda w: pl.BlockSpec((TOKEN_TILE, w), lambda i: (i, 0))
    return _x_specs(x, n_prompt_tiles) + [
        tok(pool_w),
        pl.BlockSpec((POOL_HALO, pool_w), lambda i: (jnp.maximum(i * per_tile - 1, 0), 0)),
        pl.BlockSpec((POOL_HALO, pool_w),
                     lambda i: (jnp.minimum((i + 1) * per_tile, n_halo_blocks - 1), 0)),
        tok(na_w),
        pl.BlockSpec((None, None, N_MOD, d), lambda i: (l, cond_row(i), 0, 0)),
        _resident((len(POOL_WINDOWS), LANES, LANES), lambda i: (0, 0, 0)),
        pl.BlockSpec((1, pool_w), lambda i: (0, 0)),
        _resident((d, d), lambda i: (0, 0)),
        pl.BlockSpec((1, d), lambda i: (0, 0)),
    ]


def _postmix_ffn(x, pu, attn, mod, l, w_pool, pool_scale, w_out, norm_g, wg, wu, wd,
                 seq_info, cond_row):
    xs = _as_tuple(x)
    n, d = pu.shape[0], xs[0].shape[1]
    pool_w = pu.shape[1]
    d_ff = wg.shape[1]
    kernel = functools.partial(_postmix_ffn_kernel, n_x=len(xs), **seq_info)
    return pl.pallas_call(
        kernel,
        grid=(n // TOKEN_TILE,),
        in_specs=_postmix_specs(x, n, d, pool_w, attn.shape[1], l, cond_row,
                                seq_info["n_prompt_tiles"]) + [
            _resident((d, d_ff), lambda i: (0, 0)),
            _resident((d, d_ff), lambda i: (0, 0)),
            _resident((d_ff, d), lambda i: (0, 0)),
        ],
        out_specs=pl.BlockSpec((TOKEN_TILE, d), lambda i: (i, 0)),
        out_shape=jax.ShapeDtypeStruct((n, d), F32),
        scratch_shapes=[
            pltpu.VMEM(_pool_scratch_shape(seq_info, pool_w), F32),
            pltpu.VMEM((TOKEN_TILE, d), BF16),
            pltpu.VMEM((TOKEN_TILE, d), F32),
        ],
        compiler_params=_params(),
        name=f"postmix_ffn{l}",
    )(*xs, pu, pu, pu, attn, mod, w_pool, pool_scale, w_out, norm_g, wg, wu, wd)


def _postmix_router(x, pu, attn, mod, l, w_pool, pool_scale, w_out, norm_g, r_hi, r_lo,
                    n_experts, seq_info, cond_row):
    xs = _as_tuple(x)
    sub = seq_info["seq"]
    tri = jnp.asarray(np.tril(np.ones((sub, sub)), -1), BF16)
    n, d = pu.shape[0], xs[0].shape[1]
    pool_w = pu.shape[1]
    kernel = functools.partial(_postmix_router_kernel, n_x=len(xs), n_experts=n_experts,
                               **seq_info)
    row_tile = TOKEN_TILE * d // LANES
    return pl.pallas_call(
        kernel,
        grid=(n // TOKEN_TILE,),
        in_specs=_postmix_specs(x, n, d, pool_w, attn.shape[1], l, cond_row,
                                seq_info["n_prompt_tiles"]) + [
            pl.BlockSpec((d, ROUTER_LANES), lambda i: (0, 0)),
            pl.BlockSpec((d, ROUTER_LANES), lambda i: (0, 0)),
            pl.BlockSpec((sub, sub), lambda i: (0, 0)),
        ],
        out_specs=[
            pl.BlockSpec((TOKEN_TILE, d), lambda i: (i, 0)),
            pl.BlockSpec((row_tile, LANES), lambda i: (i, 0)),
            pl.BlockSpec((SUBLANES, TOKEN_TILE), lambda i: (0, i)),
            pl.BlockSpec((TOKEN_TILE, LANES), lambda i: (i, 0)),
            pl.BlockSpec((TOKEN_TILE, LANES), lambda i: (i, 0)),
            pl.BlockSpec((1, ROUTER_LANES), lambda i: (0, 0)),
        ],
        out_shape=[
            jax.ShapeDtypeStruct((n, d), F32),
            jax.ShapeDtypeStruct((n * d // LANES, LANES), F32),
            jax.ShapeDtypeStruct((SUBLANES, n), F32),
            jax.ShapeDtypeStruct((n, LANES), F32),
            jax.ShapeDtypeStruct((n, LANES), F32),
            jax.ShapeDtypeStruct((1, ROUTER_LANES), F32),
        ],
        scratch_shapes=[pltpu.VMEM(_pool_scratch_shape(seq_info, pool_w), F32)],
        compiler_params=_params(),
        name=f"postmix_router{l}",
    )(*xs, pu, pu, pu, attn, mod, w_pool, pool_scale, w_out, norm_g, r_hi, r_lo, tri)


def _moe_kernel(te_ref, na_ref, rf_ref, h_hbm, wg_ref, wu_ref, wd_ref, y_hbm,
                xbuf, ybuf, xb16, acc_ref, gsem, ssem, zsem, *,
                cpr, rows_per_step, n_flat, n_dump_tiles):
    j = pl.program_id(0)
    f = pl.program_id(1)
    n_f = pl.num_programs(1)
    tme = MOE_ROW_TILE
    tile_rows = tme * cpr
    n_active = na_ref[0]
    last_token = n_flat // TOP_K - 1

    def gather_row(tile, slot, row):
        v = rf_ref[(tile + 1) * tme + row]
        tok = jnp.minimum(lax.shift_right_logical(v, 1), last_token)
        return pltpu.make_async_copy(
            h_hbm.at[pl.ds(pl.multiple_of(tok * cpr, cpr), cpr)],
            xbuf.at[slot, pl.ds(pl.multiple_of(row * cpr, cpr), cpr)],
            gsem.at[slot])

    def scatter_row(tile, slot, row):
        v = rf_ref[(tile + 1) * tme + row]
        return pltpu.make_async_copy(
            ybuf.at[slot, pl.ds(pl.multiple_of(row * cpr, cpr), cpr)],
            y_hbm.at[pl.ds(pl.multiple_of(v * cpr, cpr), cpr)],
            ssem.at[slot])

    def wait_gather(slot):
        pltpu.make_async_copy(h_hbm.at[pl.ds(0, tile_rows)], xbuf.at[slot], gsem.at[slot]).wait()

    def wait_scatter(slot):
        pltpu.make_async_copy(ybuf.at[slot], y_hbm.at[pl.ds(0, tile_rows)], ssem.at[slot]).wait()

    def dump_fill(t):
        return pltpu.make_async_copy(
            ybuf.at[1], y_hbm.at[pl.ds((n_flat + t * tme) * cpr, tile_rows)], zsem)

    @pl.when((f == 0) & (j == 0))
    def _():
        ybuf[1] = jnp.zeros(ybuf.shape[1:], ybuf.dtype)
        for t in range(n_dump_tiles):
            dump_fill(t).start()
        for t in range(n_dump_tiles):
            dump_fill(t).wait()

        def body(r, carry):
            gather_row(0, 0, r).start()
            return carry

        lax.fori_loop(0, tme, body, 0)

    def step(par, first, last):
        if last:
            pl.when(j >= 1)(lambda: wait_scatter(par))
        if first:
            wait_gather(par)
            for s in range(cpr):
                xb16[:, s * LANES:(s + 1) * LANES] = (
                    xbuf[par, pl.ds(s, tme, stride=cpr), :].astype(BF16))
        for t in range(rows_per_step):
            gather_row(j + 1, 1 - par, t * n_f + f).start()
            scatter_row(j - 1, 1 - par, t * n_f + f).start()
        x = xb16[...]
        a = jnp.dot(x, wg_ref[...].astype(BF16), preferred_element_type=F32)
        b = jnp.dot(x, wu_ref[...].astype(BF16), preferred_element_type=F32)
        act = (a * jax.nn.sigmoid(a) * b).astype(BF16)
        part = jnp.dot(act, wd_ref[...].astype(BF16), preferred_element_type=F32)
        total = part if first else acc_ref[...] + part
        if not last:
            acc_ref[...] = total
        else:
            for s in range(cpr):
                ybuf[par, pl.ds(s, tme, stride=cpr), :] = total[:, s * LANES:(s + 1) * LANES]

    on = j < n_active
    par = j % 2
    pl.when(on & (f == 0))(functools.partial(step, par, True, False))
    pl.when(on & (f > 0) & (f < n_f - 1))(functools.partial(step, par, False, False))
    pl.when(on & (f == n_f - 1))(functools.partial(step, par, False, True))

    @pl.when((j == n_active) & (f == 0))
    def _():
        def body(r, carry):
            scatter_row(j - 1, 1 - par, r).start()
            return carry

        lax.fori_loop(0, tme, body, 0)
        wait_gather(par)
        wait_scatter(par)
        wait_scatter(1 - par)


def _moe(tile_expert, n_active, row_flat, h_rows, wg, wu, wd, d, n_flat, n_dump_tiles):
    n_steps = tile_expert.shape[0]
    d_exp = wg.shape[2]
    n_f = d_exp // MOE_F_CHUNK
    cpr = d // LANES
    tile_rows = MOE_ROW_TILE * cpr
    assert MOE_ROW_TILE % n_f == 0 and n_f >= 2

    def f_idx(j, f, na):
        return jnp.where(j < na[0], f, n_f - 1)

    grid_spec = pltpu.PrefetchScalarGridSpec(
        num_scalar_prefetch=3,
        grid=(n_steps, n_f),
        in_specs=[
            pl.BlockSpec(memory_space=pl.ANY),
            pl.BlockSpec((None, d, MOE_F_CHUNK), lambda j, f, te, na, rf: (te[j], 0, f_idx(j, f, na))),
            pl.BlockSpec((None, d, MOE_F_CHUNK), lambda j, f, te, na, rf: (te[j], 0, f_idx(j, f, na))),
            pl.BlockSpec((None, MOE_F_CHUNK, d), lambda j, f, te, na, rf: (te[j], f_idx(j, f, na), 0)),
        ],
        out_specs=pl.BlockSpec(memory_space=pl.ANY),
        scratch_shapes=[
            pltpu.VMEM((2, tile_rows, LANES), F32),
            pltpu.VMEM((2, tile_rows, LANES), F32),
            pltpu.VMEM((MOE_ROW_TILE, d), BF16),
            pltpu.VMEM((MOE_ROW_TILE, d), F32),
            pltpu.SemaphoreType.DMA((2,)),
            pltpu.SemaphoreType.DMA((2,)),
            pltpu.SemaphoreType.DMA,
        ],
    )
    kernel = functools.partial(_moe_kernel, cpr=cpr, rows_per_step=MOE_ROW_TILE // n_f,
                               n_flat=n_flat, n_dump_tiles=n_dump_tiles)
    return pl.pallas_call(
        kernel,
        grid_spec=grid_spec,
        out_shape=jax.ShapeDtypeStruct(((n_flat + n_dump_tiles * MOE_ROW_TILE) * cpr, LANES), F32),
        compiler_params=_params(2),
        name="moe",
    )(tile_expert, n_active, row_flat, h_rows, wg, wu, wd)


def _combine_kernel(y_ref, x2_ref, g0_ref, g1_ref, mod_ref, *o_refs, cpr, n_prompt_tiles):
    i = pl.program_id(0)
    tm = x2_ref.shape[0]
    gate2 = mod_ref[...][5:6]
    stride = TOP_K * cpr

    def write(o_ref):
        g0 = g0_ref[...]
        g1 = g1_ref[...]
        for s in range(cpr):
            lanes = slice(s * LANES, (s + 1) * LANES)
            ya = y_ref[pl.ds(s, tm, stride=stride), :]
            yb = y_ref[pl.ds(cpr + s, tm, stride=stride), :]
            o_ref[:, lanes] = x2_ref[:, lanes] + gate2[:, lanes] * (g0 * ya + g1 * yb)

    if len(o_refs) == 1:
        write(o_refs[0])
    else:
        pl.when(i < n_prompt_tiles)(lambda: write(o_refs[0]))
        pl.when(i >= n_prompt_tiles)(lambda: write(o_refs[1]))


def _combine(y_rows, x2, g0, g1, mod, l, cond_row, n_prompt, split_output):
    n, d = x2.shape
    cpr = d // LANES
    n_prompt_tiles = n_prompt // TOKEN_TILE
    kernel = functools.partial(_combine_kernel, cpr=cpr, n_prompt_tiles=n_prompt_tiles)
    if split_output:
        out_specs = [
            pl.BlockSpec((TOKEN_TILE, d), lambda i: (jnp.minimum(i, n_prompt_tiles - 1), 0)),
            pl.BlockSpec((TOKEN_TILE, d), lambda i: (jnp.maximum(i - n_prompt_tiles, 0), 0))]
        out_shape = [jax.ShapeDtypeStruct((n_prompt, d), F32),
                     jax.ShapeDtypeStruct((n - n_prompt, d), F32)]
    else:
        out_specs = pl.BlockSpec((TOKEN_TILE, d), lambda i: (i, 0))
        out_shape = jax.ShapeDtypeStruct((n, d), F32)
    return pl.pallas_call(
        kernel,
        grid=(n // TOKEN_TILE,),
        in_specs=[
            pl.BlockSpec((TOKEN_TILE * TOP_K * cpr, LANES), lambda i: (i, 0)),
            pl.BlockSpec((TOKEN_TILE, d), lambda i: (i, 0)),
            pl.BlockSpec((TOKEN_TILE, LANES), lambda i: (i, 0)),
            pl.BlockSpec((TOKEN_TILE, LANES), lambda i: (i, 0)),
            pl.BlockSpec((None, None, N_MOD, d), lambda i: (l, cond_row(i), 0, 0)),
        ],
        out_specs=out_specs,
        out_shape=out_shape,
        compiler_params=_params(),
        name="combine",
    )(y_rows, x2, g0, g1, mod)


def _dispatch_plan(expert_ids, ranks, counts, n_tiles_max):
    tme = MOE_ROW_TILE
    n_experts = counts.shape[0]
    n = expert_ids[0].shape[0]
    n_flat = TOP_K * n
    tiles = (counts + tme - 1) // tme
    tile_end = jnp.cumsum(tiles)
    row_off = (tile_end - tiles) * tme
    token = jnp.arange(n, dtype=jnp.int32)
    pos, flat = [], []
    for k in range(TOP_K):
        off = jnp.sum(jnp.where(expert_ids[k][:, None] == jnp.arange(n_experts)[None, :],
                                row_off[None, :], 0), axis=1)
        pos.append(tme + off + ranks[k])
        flat.append(TOP_K * token + k)
    n_active = tile_end[-1]
    n_steps = n_tiles_max + 1
    tile_ids = jnp.minimum(jnp.arange(n_steps), n_active - 1)
    tile_expert = jnp.sum(tile_ids[:, None] >= tile_end[None, :], axis=1).astype(jnp.int32)
    n_dump_tiles = n_experts + 1
    rf = jnp.full(((n_steps + 1) * tme,), -1, jnp.int32).at[jnp.concatenate(pos)].set(
        jnp.concatenate(flat), unique_indices=True)
    is_pad = rf < 0
    pad_slot = jnp.minimum(jnp.cumsum(is_pad.astype(jnp.int32)) - 1, n_dump_tiles * tme - 1)
    rf = jnp.where(is_pad, n_flat + pad_slot, rf)
    return rf, tile_expert, n_active.astype(jnp.int32).reshape(1), n_dump_tiles


def kernel(x_prompt, x_sample, cache_k, cache_v, c, c_ctx, norm1_g, norm2_g, w_ada, b_ada, w_in,
           q_norm_g, k_norm_g, w_pool, pool_scale, rpb, w_out, ffn_w_gate, ffn_w_up, ffn_w_down,
           moe_router, moe_w_gate, moe_w_up, moe_w_down):
    batch, seq, d = x_prompt.shape
    dec_batch, dec_seq, _ = x_sample.shape
    depth = w_in.shape[0]
    heads = cache_k.shape[3]
    na_w = heads * HEAD_DIM
    pool_w = d - na_w
    n_prompt = batch * seq
    n = n_prompt + dec_batch * dec_seq
    n_experts = moe_router.shape[2]
    assert pool_w == len(POOL_WINDOWS) * LANES and na_w % LANES == 0
    assert TOKEN_TILE % seq == 0 and n_prompt % dec_seq == 0 and dec_seq % TOKEN_TILE == 0
    assert dec_seq // GRID_W >= NA_WIN_R and dec_batch < COND_ROWS
    n_prompt_tiles = n_prompt // TOKEN_TILE
    seq_info = dict(n_prompt_tiles=n_prompt_tiles, seq=seq, dec_seq=dec_seq)

    def cond_row(i):
        start = i * TOKEN_TILE
        return jnp.where(start < n_prompt, dec_batch, (start - n_prompt) // dec_seq)

    cond = jnp.zeros((COND_ROWS, d), F32).at[:dec_batch].set(c).at[dec_batch].set(c_ctx)
    mod = _ada(cond, w_ada, b_ada).reshape(depth, COND_ROWS, N_MOD, d)

    hsum = jnp.asarray(np.kron(np.eye(heads), np.ones((HEAD_DIM, HEAD_DIM))), BF16)
    ctx_k = cache_k.reshape(dec_batch, depth, cache_k.shape[2] * heads, HEAD_DIM)
    ctx_v = cache_v.reshape(dec_batch, depth, cache_v.shape[2] * heads, HEAD_DIM)

    x = (x_prompt.reshape(n_prompt, d), x_sample.reshape(-1, d))
    cache_kv = None
    cache_shape = (batch, depth, seq * heads, HEAD_DIM)
    for l in range(depth):
        pu, q, k, v, *cache_kv = _premix(
            x, mod, l, norm1_g[l][None], w_in[l].astype(BF16),
            jnp.tile(q_norm_g[l], heads)[None], jnp.tile(k_norm_g[l], heads)[None], hsum,
            n_prompt, cond_row, cache_kv, cache_shape, seq)
        attn = _attention(q, k, v, ctx_k, ctx_v, _relative_bias_table(rpb[l]), l,
                          n_prompt, seq, dec_seq)
        mix_args = (mod, l, w_pool[l].astype(BF16), pool_scale[l][None], w_out[l].astype(BF16),
                    norm2_g[l][None])
        li = l // 2
        if l % 2 == 0:
            assert ffn_w_gate.shape[2] % FF_CHUNK == 0
            x = _postmix_ffn(x, pu, attn, *mix_args, ffn_w_gate[li].astype(BF16),
                             ffn_w_up[li].astype(BF16), ffn_w_down[li].astype(BF16),
                             seq_info, cond_row)
        else:
            router = jnp.zeros((d, ROUTER_LANES), F32).at[:, :n_experts].set(moe_router[li])
            r_hi = router.astype(BF16)
            r_lo = (router - r_hi.astype(F32)).astype(BF16)
            x2, h_rows, rout, g0, g1, counts = _postmix_router(
                x, pu, attn, *mix_args, r_hi, r_lo, n_experts, seq_info, cond_row)
            routing = [rout[t].astype(jnp.int32) for t in range(2 * TOP_K)]
            n_tiles_max = (TOP_K * n + n_experts * (MOE_ROW_TILE - 1)) // MOE_ROW_TILE
            row_flat, tile_expert, n_active, n_dump_tiles = _dispatch_plan(
                routing[:TOP_K], routing[TOP_K:], counts[0, :n_experts].astype(jnp.int32),
                n_tiles_max)
            y_rows = _moe(tile_expert, n_active, row_flat, h_rows,
                          moe_w_gate[li], moe_w_up[li], moe_w_down[li], d, TOP_K * n, n_dump_tiles)
            x = _combine(y_rows, x2, g0, g1, mod, l, cond_row, n_prompt,
                         split_output=(l == depth - 1))

    if not isinstance(x, tuple):
        x = (x[:n_prompt], x[n_prompt:])
    y_prompt = x[0].reshape(batch, seq, d)
    y_sample = x[1].reshape(dec_batch, dec_seq, d)
    new_k, new_v = (a.reshape(batch, depth, seq, heads, HEAD_DIM) for a in cache_kv)
    return (y_prompt, y_sample, new_k, new_v)
```

```python
import functools

import numpy as np
import jax
import jax.numpy as jnp
from jax import lax
from jax.experimental import pallas as pl
from jax.experimental.pallas import tpu as pltpu

F32 = jnp.float32
BF16 = jnp.bfloat16

GRID_W = 64
POOL_WINDOWS = (2, 4, 8, 16)
HEAD_DIM = 64
NA_WIN_R = 8
NA_WIN_C = 16
N_MOD = 6
TOP_K = 2
NORM_EPS = 1e-6
LOG2_E = 1.4426950408889634

LANES = 128
SUBLANES = 8
VMEM_LIMIT_BYTES = 56 * 1024 * 1024

TOKEN_TILE = 512
POOL_HALO = 8
FF_CHUNK = 256
MOE_ROW_TILE = 1008
MOE_F_CHUNK = 512
ADA_COL_TILE = 1024
COND_ROWS = 16
ROUTER_LANES = 128


def _params(n_axes=1):
    return pltpu.CompilerParams(
        dimension_semantics=("arbitrary",) * n_axes,
        vmem_limit_bytes=VMEM_LIMIT_BYTES,
    )


def _resident(shape, index_map):
    return pl.BlockSpec(shape, index_map, pipeline_mode=pl.Buffered(1))


def _x_specs(x, n_prompt_tiles):
    if not isinstance(x, tuple):
        return [pl.BlockSpec((TOKEN_TILE, x.shape[1]), lambda i: (i, 0))]
    d = x[0].shape[1]
    return [pl.BlockSpec((TOKEN_TILE, d), lambda i: (jnp.minimum(i, n_prompt_tiles - 1), 0)),
            pl.BlockSpec((TOKEN_TILE, d), lambda i: (jnp.maximum(i - n_prompt_tiles, 0), 0))]


def _load_x(i, x_refs, n_prompt_tiles):
    if len(x_refs) == 1:
        return x_refs[0][...]
    return jnp.where(i < n_prompt_tiles, x_refs[0][...], x_refs[1][...])


def _as_tuple(x):
    return x if isinstance(x, tuple) else (x,)


def _ada_kernel(c_ref, w_ref, b_ref, o_ref):
    c = c_ref[...]
    s = c * jax.nn.sigmoid(c)
    o_ref[...] = jnp.dot(s.astype(BF16), w_ref[...].astype(BF16),
                         preferred_element_type=F32) + b_ref[...]


def _ada(cond, w_ada, b_ada):
    depth, d, width = w_ada.shape
    return pl.pallas_call(
        _ada_kernel,
        grid=(depth, width // ADA_COL_TILE),
        in_specs=[
            pl.BlockSpec((COND_ROWS, d), lambda l, j: (0, 0)),
            pl.BlockSpec((None, d, ADA_COL_TILE), lambda l, j: (l, 0, j)),
            pl.BlockSpec((None, 1, ADA_COL_TILE), lambda l, j: (l, 0, j)),
        ],
        out_specs=pl.BlockSpec((None, COND_ROWS, ADA_COL_TILE), lambda l, j: (l, 0, j)),
        out_shape=jax.ShapeDtypeStruct((depth, COND_ROWS, width), F32),
        compiler_params=_params(2),
        name="ada",
    )(cond, w_ada, b_ada.reshape(depth, 1, width))


def _premix_kernel(*refs, n_x, n_prompt_tiles, pool_w, na_w, seq, layer, creates_cache):
    n_in = n_x + (6 if creates_cache else 8)
    mod_ref, g_ref, w_ref, qg_ref, kg_ref, hsum_ref = refs[n_x:n_x + 6]
    pu_ref, q_ref, k_ref, v_ref, kf_ref, vf_ref = refs[n_in:]
    i = pl.program_id(0)
    x = _load_x(i, refs[:n_x], n_prompt_tiles)
    inv = lax.rsqrt(jnp.mean(x * x, axis=-1, keepdims=True) + NORM_EPS)
    mod = mod_ref[...]
    h = (x * inv * g_ref[...]) * (1.0 + mod[1:2]) + mod[0:1]
    u = jnp.dot(h.astype(BF16), w_ref[...], preferred_element_type=F32)
    pu_ref[...] = u[:, :pool_w]
    q = u[:, pool_w:pool_w + na_w]
    k = u[:, pool_w + na_w:pool_w + 2 * na_w]
    v = u[:, pool_w + 2 * na_w:]

    def head_norm(t, g):
        ms = jnp.dot((t * t).astype(BF16), hsum_ref[...],
                     preferred_element_type=F32) * (1.0 / HEAD_DIM)
        return t * lax.rsqrt(ms + NORM_EPS) * g

    qn = head_norm(q, qg_ref[...])
    kn = head_norm(k, kg_ref[...])
    q_ref[...] = (qn * (HEAD_DIM ** -0.5 * LOG2_E)).astype(BF16)
    k_ref[...] = kn.astype(BF16)
    v_ref[...] = v.astype(BF16)

    @pl.when(i < n_prompt_tiles)
    def _():
        heads = na_w // HEAD_DIM
        if creates_cache:
            for lz in range(kf_ref.shape[1]):
                if lz != layer:
                    kf_ref[:, lz] = jnp.zeros((kf_ref.shape[0],) + kf_ref.shape[2:], F32)
                    vf_ref[:, lz] = jnp.zeros((vf_ref.shape[0],) + vf_ref.shape[2:], F32)
            kf_l, vf_l = kf_ref.at[:, layer], vf_ref.at[:, layer]
        else:
            kf_l, vf_l = kf_ref, vf_ref
        for b in range(TOKEN_TILE // seq):
            for hd in range(heads):
                rows, cols = slice(b * seq, (b + 1) * seq), slice(hd * HEAD_DIM, (hd + 1) * HEAD_DIM)
                kf_l[b, pl.ds(hd, seq, stride=heads), :] = kn[rows, cols]
                vf_l[b, pl.ds(hd, seq, stride=heads), :] = v[rows, cols]


def _premix(x, mod, l, norm_g, w_in, q_g, k_g, hsum, n_prompt, cond_row, cache_kv, cache_shape,
            seq):
    xs = _as_tuple(x)
    creates_cache = cache_kv is None
    n, d = sum(a.shape[0] for a in xs), xs[0].shape[1]
    pool_w = d // 2
    na_w = d - pool_w
    n_tiles = n // TOKEN_TILE
    n_prompt_tiles = n_prompt // TOKEN_TILE
    last_p = n_prompt_tiles - 1
    tok = lambda w: pl.BlockSpec((TOKEN_TILE, w), lambda i: (i, 0))
    if creates_cache:
        cache_spec = pl.BlockSpec((TOKEN_TILE // seq,) + cache_shape[1:],
                                  lambda i: (jnp.minimum(i, last_p), 0, 0, 0))
        cache_in, cache_in_specs, aliases = (), [], {}
    else:
        cache_spec = pl.BlockSpec((TOKEN_TILE // seq, None) + cache_shape[2:],
                                  lambda i: (jnp.minimum(i, last_p), l, 0, 0))
        cache_in, cache_in_specs = tuple(cache_kv), [pl.BlockSpec(memory_space=pl.ANY)] * 2
        aliases = {len(xs) + 6: 4, len(xs) + 7: 5}
    kernel = functools.partial(_premix_kernel, n_x=len(xs), n_prompt_tiles=n_prompt_tiles,
                               pool_w=pool_w, na_w=na_w, seq=seq, layer=l,
                               creates_cache=creates_cache)
    return pl.pallas_call(
        kernel,
        grid=(n_tiles,),
        in_specs=_x_specs(x, n_prompt_tiles) + [
            pl.BlockSpec((None, None, N_MOD, d), lambda i: (l, cond_row(i), 0, 0)),
            pl.BlockSpec((1, d), lambda i: (0, 0)),
            _resident((d, w_in.shape[1]), lambda i: (0, 0)),
            pl.BlockSpec((1, na_w), lambda i: (0, 0)),
            pl.BlockSpec((1, na_w), lambda i: (0, 0)),
            _resident((na_w, na_w), lambda i: (0, 0)),
        ] + cache_in_specs,
        out_specs=[tok(pool_w), tok(na_w), tok(na_w), tok(na_w), cache_spec, cache_spec],
        out_shape=[
            jax.ShapeDtypeStruct((n, pool_w), F32),
            jax.ShapeDtypeStruct((n, na_w), BF16),
            jax.ShapeDtypeStruct((n, na_w), BF16),
            jax.ShapeDtypeStruct((n, na_w), BF16),
            jax.ShapeDtypeStruct(cache_shape, F32),
            jax.ShapeDtypeStruct(cache_shape, F32),
        ],
        input_output_aliases=aliases,
        compiler_params=_params(),
        name=f"premix{l}",
    )(*xs, mod, norm_g, w_in, q_g, k_g, hsum, *cache_in)


_NT = (((1,), (1,)), ((), ()))
_TN = (((0,), (0,)), ((), ()))


def _block_diag_queries(q2):
    lo = lax.broadcasted_iota(jnp.int32, q2.shape, 1) < HEAD_DIM
    zero = jnp.zeros_like(q2)
    return jnp.concatenate([jnp.where(lo, q2, zero), jnp.where(lo, zero, q2)], axis=0)


def _pick_head_blocks(o, nq):
    lo = lax.broadcasted_iota(jnp.int32, (nq, LANES), 1) < HEAD_DIM
    return jnp.where(lo, o[:nq], o[nq:])


def _pair_attention(q2, k, v):
    nq = q2.shape[0]
    s = lax.dot_general(k, _block_diag_queries(q2), _NT, preferred_element_type=F32)
    p = jnp.exp2(s - jnp.max(s, axis=0, keepdims=True))
    r = 1.0 / jnp.sum(p, axis=0, keepdims=True)
    o = lax.dot_general((p * r).astype(BF16), v, _TN, preferred_element_type=F32)
    return _pick_head_blocks(o, nq)


def _attn_kernel(q_ref, k_ref, v_ref, ckf_ref, cvf_ref, bias_ref, o_ref, s_ref, p_ref,
                 ck_ref, cv_ref, *, n_prompt_tiles, seq, dec_seq, n_pairs):
    i = pl.program_id(0)
    tiles_per_seq = dec_seq // TOKEN_TILE
    rows_per_tile = TOKEN_TILE // GRID_W
    rows = dec_seq // GRID_W
    win_keys = NA_WIN_R * GRID_W
    block_off = (i * TOKEN_TILE) % dec_seq

    @pl.when(i < n_prompt_tiles)
    def _():
        for s in range(TOKEN_TILE // seq):
            start = pl.multiple_of(block_off + s * seq, seq)
            for hp in range(n_pairs):
                lanes = slice(hp * LANES, (hp + 1) * LANES)
                out = _pair_attention(q_ref[s * seq:(s + 1) * seq, lanes],
                                      k_ref[pl.ds(start, seq), lanes],
                                      v_ref[pl.ds(start, seq), lanes])
                o_ref[s * seq:(s + 1) * seq, lanes] = out.astype(o_ref.dtype)

    @pl.when(i >= n_prompt_tiles)
    def _():
        tile_in_seq = (i - n_prompt_tiles) % tiles_per_seq
        row0 = tile_in_seq * rows_per_tile

        @pl.when(tile_in_seq == 0)
        def _():
            past = ck_ref.shape[0]
            heads = 2 * n_pairs
            for src, dst in ((ckf_ref, ck_ref), (cvf_ref, cv_ref)):
                for hp in range(n_pairs):
                    pair = [src[pl.ds(2 * hp + t, past, stride=heads), :] for t in range(2)]
                    dst[:, hp * LANES:(hp + 1) * LANES] = (
                        jnp.concatenate(pair, axis=-1).astype(BF16))

        def indices(rl):
            r = row0 + rl
            r0 = jnp.clip(r - NA_WIN_R // 2, 0, rows - NA_WIN_R)
            return r0 - r + NA_WIN_R - 1, pl.multiple_of(r0 * GRID_W, GRID_W)

        def scores(rl):
            d0, kstart = indices(rl)
            for hp in range(n_pairs):
                lanes = slice(hp * LANES, (hp + 1) * LANES)
                qbd = _block_diag_queries(q_ref[rl * GRID_W:(rl + 1) * GRID_W, lanes])
                bias = bias_ref[hp, pl.ds(d0, NA_WIN_R)].reshape(win_keys, LANES)
                s_ref[rl % 2, hp, 0:win_keys] = lax.dot_general(
                    k_ref[pl.ds(kstart, win_keys), lanes], qbd, _NT,
                    preferred_element_type=F32) + bias
                s_ref[rl % 2, hp, win_keys:] = lax.dot_general(
                    ck_ref[:, lanes], qbd, _NT, preferred_element_type=F32)

        def softmax(rl):
            for hp in range(n_pairs):
                s = s_ref[rl % 2, hp]
                p = jnp.exp2(s - jnp.max(s, axis=0, keepdims=True))
                rr = 1.0 / jnp.sum(p, axis=0, keepdims=True)
                p_ref[rl % 2, hp] = (p * rr).astype(BF16)

        def values(rl):
            _, kstart = indices(rl)
            for hp in range(n_pairs):
                lanes = slice(hp * LANES, (hp + 1) * LANES)
                o = (lax.dot_general(p_ref[rl % 2, hp, 0:win_keys],
                                     v_ref[pl.ds(kstart, win_keys), lanes],
                                     _TN, preferred_element_type=F32)
                     + lax.dot_general(p_ref[rl % 2, hp, win_keys:], cv_ref[:, lanes], _TN,
                                       preferred_element_type=F32))
                o_ref[rl * GRID_W:(rl + 1) * GRID_W, lanes] = (
                    _pick_head_blocks(o, GRID_W).astype(o_ref.dtype))

        for step in range(rows_per_tile + 2):
            if step < rows_per_tile:
                scores(step)
            if 1 <= step <= rows_per_tile:
                softmax(step - 1)
            if step >= 2:
                values(step - 2)


def _attention(q, k, v, ctx_k, ctx_v, bias, l, n_prompt, seq, dec_seq):
    n, na_w = q.shape
    n_tiles = n // TOKEN_TILE
    n_prompt_tiles = n_prompt // TOKEN_TILE
    tiles_per_seq = dec_seq // TOKEN_TILE
    heads = na_w // HEAD_DIM
    past = ctx_k.shape[2] // heads
    n_pairs = na_w // LANES
    n_keys = NA_WIN_R * GRID_W + past
    kv_spec = pl.BlockSpec((dec_seq, na_w), lambda i: (i * TOKEN_TILE // dec_seq, 0))
    ctx_spec = pl.BlockSpec(
        (None, None, past * heads, HEAD_DIM),
        lambda i: (jnp.maximum(i - n_prompt_tiles, 0) // tiles_per_seq, l, 0, 0))
    kernel = functools.partial(_attn_kernel, n_prompt_tiles=n_prompt_tiles, seq=seq,
                               dec_seq=dec_seq, n_pairs=n_pairs)
    return pl.pallas_call(
        kernel,
        grid=(n_tiles,),
        in_specs=[
            pl.BlockSpec((TOKEN_TILE, na_w), lambda i: (i, 0)),
            kv_spec, kv_spec, ctx_spec, ctx_spec,
            _resident(bias.shape, lambda i: (0, 0, 0, 0)),
        ],
        out_specs=pl.BlockSpec((TOKEN_TILE, na_w), lambda i: (i, 0)),
        out_shape=jax.ShapeDtypeStruct((n, na_w), BF16),
        scratch_shapes=[pltpu.VMEM((2, n_pairs, n_keys, LANES), F32),
                        pltpu.VMEM((2, n_pairs, n_keys, LANES), BF16),
                        pltpu.VMEM((past, na_w), BF16),
                        pltpu.VMEM((past, na_w), BF16)],
        compiler_params=_params(),
        name=f"attn{l}",
    )(q, k, v, ctx_k, ctx_v, bias)


def _relative_bias_table(rpb_l):
    heads = rpb_l.shape[0]
    kc = np.arange(GRID_W)[:, None]
    qc = np.arange(GRID_W)[None, :]
    q_start = np.clip(qc - NA_WIN_C // 2, 0, GRID_W - NA_WIN_C)
    valid = (kc >= q_start) & (kc < q_start + NA_WIN_C)
    dc_idx = np.clip(kc - qc, -(NA_WIN_C - 1), NA_WIN_C - 1) + NA_WIN_C - 1
    t = jnp.zeros(rpb_l.shape[:2] + dc_idx.shape, F32)
    for c in range(rpb_l.shape[2]):
        t = jnp.where(dc_idx[None, None] == c, rpb_l[:, :, c, None, None].astype(F32), t)
    t = jnp.where(valid[None, None], t * LOG2_E, -jnp.inf)
    t = t.reshape(heads // 2, 2, 2 * NA_WIN_R - 1, GRID_W, GRID_W)
    return t.transpose(0, 2, 3, 1, 4).reshape(heads // 2, 2 * NA_WIN_R - 1, GRID_W, 2 * GRID_W)


def _mod_static(t, m):
    return t & (m - 1) if m & (m - 1) == 0 else lax.rem(t, m)


def _pool_mix(i, b, pu_ref, pp_ref, pn_ref, ext_ref, wp_ref, ps_ref, *, n_prompt_tiles, seq, dec_seq):
    sub = seq
    h = POOL_HALO
    is_prompt = i < n_prompt_tiles
    seq_len = jnp.where(is_prompt, seq, dec_seq)
    lo, hi = b * sub, (b + 1) * sub
    tok0 = i * TOKEN_TILE + lo
    pos0 = jnp.where(is_prompt, _mod_static(tok0, seq), _mod_static(tok0, dec_seq))
    prev = pp_ref[...] if lo == 0 else pu_ref[lo - h:lo, :]
    nxt = pn_ref[...] if hi == TOKEN_TILE else pu_ref[hi:hi + h, :]
    ext_ref[b, 0:h] = jnp.where(pos0 != 0, prev, 0.0)
    ext_ref[b, h:h + sub] = pu_ref[lo:hi, :]
    ext_ref[b, h + sub:] = jnp.where(pos0 + sub != seq_len, nxt, 0.0)
    left = pos0 + lax.broadcasted_iota(jnp.int32, (sub, LANES), 0)
    right = seq_len - left
    outs = []
    for g, window in enumerate(POOL_WINDOWS):
        half = window // 2
        lanes = slice(g * LANES, (g + 1) * LANES)
        total = ext_ref[b, h - half:h - half + sub, lanes]
        for j in range(1 - half, half):
            total = total + ext_ref[b, h + j:h + j + sub, lanes]
        count = (jnp.minimum(left, half) + jnp.minimum(right, half)).astype(F32)
        pooled = total / count - pu_ref[lo:hi, lanes]
        outs.append(jnp.dot(pooled.astype(BF16), wp_ref[g], preferred_element_type=F32))
    return jnp.concatenate(outs, axis=-1) * ps_ref[...]


def _postmix_chain(i, x_refs, pu_ref, pp_ref, pn_ref, at_ref, mod_ref, wp_ref, ps_ref, wo_ref,
                   g2_ref, ext_ref, tail, **seq_info):
    sub = seq_info["seq"]
    n_sub = TOKEN_TILE // sub
    mod = mod_ref[...]
    is_prompt = i < seq_info["n_prompt_tiles"]

    def rows(b):
        return slice(b * sub, (b + 1) * sub)

    def pool_stage(b, _):
        return _pool_mix(i, b, pu_ref, pp_ref, pn_ref, ext_ref, wp_ref, ps_ref, **seq_info)

    def mix_stage(b, pool_out):
        mixed = jnp.concatenate([pool_out.astype(BF16), at_ref[rows(b), :]], axis=-1)
        mix = jnp.dot(mixed, wo_ref[...], preferred_element_type=F32)
        if len(x_refs) == 1:
            x = x_refs[0][rows(b), :]
        else:
            x = jnp.where(is_prompt, x_refs[0][rows(b), :], x_refs[1][rows(b), :])
        return x + mod[2:3] * mix

    def norm_stage(b, x2):
        inv = lax.rsqrt(jnp.mean(x2 * x2, axis=-1, keepdims=True) + NORM_EPS)
        return x2, (x2 * inv * g2_ref[...]) * (1.0 + mod[4:5]) + mod[3:4]

    def tail_stage(b, x2_h2):
        tail(rows(b), *x2_h2)

    stages = (pool_stage, mix_stage, norm_stage, tail_stage)
    state = [None] * n_sub
    for step in range(n_sub + len(stages) - 1):
        for b in range(n_sub):
            if 0 <= step - b < len(stages):
                state[b] = stages[step - b](b, state[b])
    return mod[5:6]


def _postmix_ffn_kernel(*refs, n_x, **seq_info):
    (pu_ref, pp_ref, pn_ref, at_ref, mod_ref, wp_ref, ps_ref, wo_ref,
     g2_ref, wg_ref, wu_ref, wd_ref, y_ref, ext_ref, h_ref, acc_ref) = refs[n_x:]
    i = pl.program_id(0)

    def tail(rows, x2, h2):
        y_ref[rows, :] = x2
        h_ref[rows, :] = h2.astype(BF16)

    gate2 = _postmix_chain(i, refs[:n_x], pu_ref, pp_ref, pn_ref, at_ref, mod_ref, wp_ref,
                           ps_ref, wo_ref, g2_ref, ext_ref, tail, **seq_info)
    acc_ref[...] = jnp.zeros_like(acc_ref)

    h = h_ref[...]
    for c in range(wg_ref.shape[1] // FF_CHUNK):
        cols = slice(c * FF_CHUNK, (c + 1) * FF_CHUNK)
        a = jnp.dot(h, wg_ref[:, cols], preferred_element_type=F32)
        b = jnp.dot(h, wu_ref[:, cols], preferred_element_type=F32)
        act = (a * jax.nn.sigmoid(a) * b).astype(BF16)
        acc_ref[...] += jnp.dot(act, wd_ref[cols, :], preferred_element_type=F32)
    y_ref[...] = y_ref[...] + gate2 * acc_ref[...]


def _postmix_router_kernel(*refs, n_x, n_experts, **seq_info):
    (pu_ref, pp_ref, pn_ref, at_ref, mod_ref, wp_ref, ps_ref, wo_ref,
     g2_ref, rh_ref, rl_ref, tri_ref, x2_ref, hr_ref, rout_ref, g0_ref, g1_ref, cnt_ref,
     ext_ref) = refs[n_x:]
    i = pl.program_id(0)

    @pl.when(i == 0)
    def _():
        cnt_ref[...] = jnp.zeros_like(cnt_ref)

    def tail(rows, x2, h2):
        x2_ref[rows, :] = x2
        n_rows = rows.stop - rows.start
        for s in range(h2.shape[1] // LANES):
            hr_ref[pl.ds(rows.start * SUBLANES + s, n_rows, stride=SUBLANES), :] = (
                h2[:, s * LANES:(s + 1) * LANES])
        hi = h2.astype(BF16)
        lo = (h2 - hi.astype(F32)).astype(BF16)
        logits = (jnp.dot(hi, rh_ref[...], preferred_element_type=F32)
                  + jnp.dot(lo, rh_ref[...], preferred_element_type=F32)
                  + jnp.dot(hi, rl_ref[...], preferred_element_type=F32))
        lane = lax.broadcasted_iota(jnp.int32, logits.shape, 1).astype(F32)
        logits = jnp.where(lane < n_experts, logits, -jnp.inf)
        m1 = jnp.max(logits, axis=-1, keepdims=True)
        i1 = jnp.min(jnp.where(logits == m1, lane, float(ROUTER_LANES)), axis=-1, keepdims=True)
        rest = jnp.where(lane == i1, -jnp.inf, logits)
        m2 = jnp.max(rest, axis=-1, keepdims=True)
        i2 = jnp.min(jnp.where(rest == m2, lane, float(ROUTER_LANES)), axis=-1, keepdims=True)
        e = jnp.exp(m2 - m1)
        pick0 = jnp.where(lane == i1, 1.0, 0.0)
        pick1 = jnp.where(lane == i2, 1.0, 0.0)
        picks = pick0 + pick1
        before = cnt_ref[...] + jnp.dot(tri_ref[...], picks.astype(BF16),
                                        preferred_element_type=F32)
        rank0 = jnp.sum(before * pick0, axis=-1, keepdims=True)
        rank1 = jnp.sum(before * pick1, axis=-1, keepdims=True)
        cnt_ref[...] = before[n_rows - 1:n_rows, :] + picks[n_rows - 1:n_rows, :]
        rout_ref[rows, :] = jnp.where(
            lane == 0, i1, jnp.where(lane == 1, i2, jnp.where(lane == 2, rank0,
                                                              jnp.where(lane == 3, rank1, 0.0))))
        g0_ref[rows, :] = jnp.broadcast_to(1.0 / (1.0 + e), (n_rows, LANES))
        g1_ref[rows, :] = jnp.broadcast_to(e / (1.0 + e), (n_rows, LANES))

    _postmix_chain(i, refs[:n_x], pu_ref, pp_ref, pn_ref, at_ref, mod_ref, wp_ref,
                   ps_ref, wo_ref, g2_ref, ext_ref, tail, **seq_info)


def _pool_scratch_shape(seq_info, pool_w):
    sub = seq_info["seq"]
    return (TOKEN_TILE // sub, sub + 2 * POOL_HALO, pool_w)


def _postmix_specs(x, n, d, pool_w, na_w, l, cond_row, n_prompt_tiles):
    n_halo_blocks = n // POOL_HALO
    per_tile = TOKEN_TILE // POOL_HALO
    tok = lambda w: pl.BlockSpec((TOKEN_TILE, w), lambda i: (i, 0))
    return _x_specs(x, n_prompt_tiles) + [
        tok(pool_w),
        pl.BlockSpec((POOL_HALO, pool_w), lambda i: (jnp.maximum(i * per_tile - 1, 0), 0)),
        pl.BlockSpec((POOL_HALO, pool_w),
                     lambda i: (jnp.minimum((i + 1) * per_tile, n_halo_blocks - 1), 0)),
        tok(na_w),
        pl.BlockSpec((None, None, N_MOD, d), lambda i: (l, cond_row(i), 0, 0)),
        _resident((len(POOL_WINDOWS), LANES, LANES), lambda i: (0, 0, 0)),
        pl.BlockSpec((1, pool_w), lambda i: (0, 0)),
        _resident((d, d), lambda i: (0, 0)),
        pl.BlockSpec((1, d), lambda i: (0, 0)),
    ]


def _postmix_ffn(x, pu, attn, mod, l, w_pool, pool_scale, w_out, norm_g, wg, wu, wd,
                 seq_info, cond_row):
    xs = _as_tuple(x)
    n, d = pu.shape[0], xs[0].shape[1]
    pool_w = pu.shape[1]
    d_ff = wg.shape[1]
    kernel = functools.partial(_postmix_ffn_kernel, n_x=len(xs), **seq_info)
    return pl.pallas_call(
        kernel,
        grid=(n // TOKEN_TILE,),
        in_specs=_postmix_specs(x, n, d, pool_w, attn.shape[1], l, cond_row,
                                seq_info["n_prompt_tiles"]) + [
            _resident((d, d_ff), lambda i: (0, 0)),
            _resident((d, d_ff), lambda i: (0, 0)),
            _resident((d_ff, d), lambda i: (0, 0)),
        ],
        out_specs=pl.BlockSpec((TOKEN_TILE, d), lambda i: (i, 0)),
        out_shape=jax.ShapeDtypeStruct((n, d), F32),
        scratch_shapes=[
            pltpu.VMEM(_pool_scratch_shape(seq_info, pool_w), F32),
            pltpu.VMEM((TOKEN_TILE, d), BF16),
            pltpu.VMEM((TOKEN_TILE, d), F32),
        ],
        compiler_params=_params(),
        name=f"postmix_ffn{l}",
    )(*xs, pu, pu, pu, attn, mod, w_pool, pool_scale, w_out, norm_g, wg, wu, wd)


def _postmix_router(x, pu, attn, mod, l, w_pool, pool_scale, w_out, norm_g, r_hi, r_lo,
                    n_experts, seq_info, cond_row):
    xs = _as_tuple(x)
    sub = seq_info["seq"]
    tri = jnp.asarray(np.tril(np.ones((sub, sub)), -1), BF16)
    n, d = pu.shape[0], xs[0].shape[1]
    pool_w = pu.shape[1]
    kernel = functools.partial(_postmix_router_kernel, n_x=len(xs), n_experts=n_experts,
                               **seq_info)
    row_tile = TOKEN_TILE * d // LANES
    return pl.pallas_call(
        kernel,
        grid=(n // TOKEN_TILE,),
        in_specs=_postmix_specs(x, n, d, pool_w, attn.shape[1], l, cond_row,
                                seq_info["n_prompt_tiles"]) + [
            pl.BlockSpec((d, ROUTER_LANES), lambda i: (0, 0)),
            pl.BlockSpec((d, ROUTER_LANES), lambda i: (0, 0)),
            pl.BlockSpec((sub, sub), lambda i: (0, 0)),
        ],
        out_specs=[
            pl.BlockSpec((TOKEN_TILE, d), lambda i: (i, 0)),
            pl.BlockSpec((row_tile, LANES), lambda i: (i, 0)),
            pl.BlockSpec((TOKEN_TILE, ROUTER_LANES), lambda i: (i, 0)),
            pl.BlockSpec((TOKEN_TILE, LANES), lambda i: (i, 0)),
            pl.BlockSpec((TOKEN_TILE, LANES), lambda i: (i, 0)),
            pl.BlockSpec((1, ROUTER_LANES), lambda i: (0, 0)),
        ],
        out_shape=[
            jax.ShapeDtypeStruct((n, d), F32),
            jax.ShapeDtypeStruct((n * d // LANES, LANES), F32),
            jax.ShapeDtypeStruct((n, ROUTER_LANES), F32),
            jax.ShapeDtypeStruct((n, LANES), F32),
            jax.ShapeDtypeStruct((n, LANES), F32),
            jax.ShapeDtypeStruct((1, ROUTER_LANES), F32),
        ],
        scratch_shapes=[pltpu.VMEM(_pool_scratch_shape(seq_info, pool_w), F32)],
        compiler_params=_params(),
        name=f"postmix_router{l}",
    )(*xs, pu, pu, pu, attn, mod, w_pool, pool_scale, w_out, norm_g, r_hi, r_lo, tri)


def _moe_kernel(te_ref, na_ref, rf_ref, h_hbm, wg_ref, wu_ref, wd_ref, y_hbm,
                xbuf, ybuf, xb16, acc_ref, gsem, ssem, zsem, *,
                cpr, rows_per_step, n_flat, n_dump_tiles):
    j = pl.program_id(0)
    f = pl.program_id(1)
    n_f = pl.num_programs(1)
    tme = MOE_ROW_TILE
    tile_rows = tme * cpr
    n_active = na_ref[0]
    last_token = n_flat // TOP_K - 1

    def gather_row(tile, slot, row):
        v = rf_ref[(tile + 1) * tme + row]
        tok = jnp.minimum(lax.shift_right_logical(v, 1), last_token)
        return pltpu.make_async_copy(
            h_hbm.at[pl.ds(pl.multiple_of(tok * cpr, cpr), cpr)],
            xbuf.at[slot, pl.ds(pl.multiple_of(row * cpr, cpr), cpr)],
            gsem.at[slot])

    def scatter_row(tile, slot, row):
        v = rf_ref[(tile + 1) * tme + row]
        return pltpu.make_async_copy(
            ybuf.at[slot, pl.ds(pl.multiple_of(row * cpr, cpr), cpr)],
            y_hbm.at[pl.ds(pl.multiple_of(v * cpr, cpr), cpr)],
            ssem.at[slot])

    def wait_gather(slot):
        pltpu.make_async_copy(h_hbm.at[pl.ds(0, tile_rows)], xbuf.at[slot], gsem.at[slot]).wait()

    def wait_scatter(slot):
        pltpu.make_async_copy(ybuf.at[slot], y_hbm.at[pl.ds(0, tile_rows)], ssem.at[slot]).wait()

    def dump_fill(t):
        return pltpu.make_async_copy(
            ybuf.at[1], y_hbm.at[pl.ds((n_flat + t * tme) * cpr, tile_rows)], zsem)

    @pl.when((f == 0) & (j == 0))
    def _():
        ybuf[1] = jnp.zeros(ybuf.shape[1:], ybuf.dtype)
        for t in range(n_dump_tiles):
            dump_fill(t).start()
        for t in range(n_dump_tiles):
            dump_fill(t).wait()

        def body(r, carry):
            gather_row(0, 0, r).start()
            return carry

        lax.fori_loop(0, tme, body, 0)

    def step(par, first, last):
        if first:
            wait_gather(par)
            for s in range(cpr):
                xb16[:, s * LANES:(s + 1) * LANES] = (
                    xbuf[par, pl.ds(s, tme, stride=cpr), :].astype(BF16))
        for t in range(rows_per_step):
            gather_row(j + 1, 1 - par, t * n_f + f).start()
            scatter_row(j - 1, 1 - par, t * n_f + f).start()
        x = xb16[...]
        a = jnp.dot(x, wg_ref[...].astype(BF16), preferred_element_type=F32)
        b = jnp.dot(x, wu_ref[...].astype(BF16), preferred_element_type=F32)
        act = (a * jax.nn.sigmoid(a) * b).astype(BF16)
        part = jnp.dot(act, wd_ref[...].astype(BF16), preferred_element_type=F32)
        total = part if first else acc_ref[...] + part
        if not last:
            acc_ref[...] = total
        else:
            pl.when(j >= 1)(lambda: wait_scatter(par))
            for s in range(cpr):
                ybuf[par, pl.ds(s, tme, stride=cpr), :] = total[:, s * LANES:(s + 1) * LANES]

    for par in range(2):
        on = (j < n_active) & (j % 2 == par)
        pl.when(on & (f == 0))(functools.partial(step, par, True, False))
        pl.when(on & (f > 0) & (f < n_f - 1))(functools.partial(step, par, False, False))
        pl.when(on & (f == n_f - 1))(functools.partial(step, par, False, True))

        @pl.when((j == n_active) & (f == 0) & (j % 2 == par))
        def _(par=par):
            def body(r, carry):
                scatter_row(j - 1, 1 - par, r).start()
                return carry

            lax.fori_loop(0, tme, body, 0)
            wait_gather(par)
            wait_scatter(par)
            wait_scatter(1 - par)


def _moe(tile_expert, n_active, row_flat, h_rows, wg, wu, wd, d, n_flat, n_dump_tiles):
    n_steps = tile_expert.shape[0]
    d_exp = wg.shape[2]
    n_f = d_exp // MOE_F_CHUNK
    cpr = d // LANES
    tile_rows = MOE_ROW_TILE * cpr
    assert MOE_ROW_TILE % n_f == 0 and n_f >= 2

    def f_idx(j, f, na):
        return jnp.where(j < na[0], f, n_f - 1)

    grid_spec = pltpu.PrefetchScalarGridSpec(
        num_scalar_prefetch=3,
        grid=(n_steps, n_f),
        in_specs=[
            pl.BlockSpec(memory_space=pl.ANY),
            pl.BlockSpec((None, d, MOE_F_CHUNK), lambda j, f, te, na, rf: (te[j], 0, f_idx(j, f, na))),
            pl.BlockSpec((None, d, MOE_F_CHUNK), lambda j, f, te, na, rf: (te[j], 0, f_idx(j, f, na))),
            pl.BlockSpec((None, MOE_F_CHUNK, d), lambda j, f, te, na, rf: (te[j], f_idx(j, f, na), 0)),
        ],
        out_specs=pl.BlockSpec(memory_space=pl.ANY),
        scratch_shapes=[
            pltpu.VMEM((2, tile_rows, LANES), F32),
            pltpu.VMEM((2, tile_rows, LANES), F32),
            pltpu.VMEM((MOE_ROW_TILE, d), BF16),
            pltpu.VMEM((MOE_ROW_TILE, d), F32),
            pltpu.SemaphoreType.DMA((2,)),
            pltpu.SemaphoreType.DMA((2,)),
            pltpu.SemaphoreType.DMA,
        ],
    )
    kernel = functools.partial(_moe_kernel, cpr=cpr, rows_per_step=MOE_ROW_TILE // n_f,
                               n_flat=n_flat, n_dump_tiles=n_dump_tiles)
    return pl.pallas_call(
        kernel,
        grid_spec=grid_spec,
        out_shape=jax.ShapeDtypeStruct(((n_flat + n_dump_tiles * MOE_ROW_TILE) * cpr, LANES), F32),
        compiler_params=_params(2),
        name="moe",
    )(tile_expert, n_active, row_flat, h_rows, wg, wu, wd)


def _combine_kernel(y_ref, x2_ref, g0_ref, g1_ref, mod_ref, *o_refs, cpr, n_prompt_tiles):
    i = pl.program_id(0)
    tm = x2_ref.shape[0]
    gate2 = mod_ref[...][5:6]
    stride = TOP_K * cpr

    def write(o_ref):
        g0 = g0_ref[...]
        g1 = g1_ref[...]
        for s in range(cpr):
            lanes = slice(s * LANES, (s + 1) * LANES)
            ya = y_ref[pl.ds(s, tm, stride=stride), :]
            yb = y_ref[pl.ds(cpr + s, tm, stride=stride), :]
            o_ref[:, lanes] = x2_ref[:, lanes] + gate2[:, lanes] * (g0 * ya + g1 * yb)

    if len(o_refs) == 1:
        write(o_refs[0])
    else:
        pl.when(i < n_prompt_tiles)(lambda: write(o_refs[0]))
        pl.when(i >= n_prompt_tiles)(lambda: write(o_refs[1]))


def _combine(y_rows, x2, g0, g1, mod, l, cond_row, n_prompt, split_output):
    n, d = x2.shape
    cpr = d // LANES
    n_prompt_tiles = n_prompt // TOKEN_TILE
    kernel = functools.partial(_combine_kernel, cpr=cpr, n_prompt_tiles=n_prompt_tiles)
    if split_output:
        out_specs = [
            pl.BlockSpec((TOKEN_TILE, d), lambda i: (jnp.minimum(i, n_prompt_tiles - 1), 0)),
            pl.BlockSpec((TOKEN_TILE, d), lambda i: (jnp.maximum(i - n_prompt_tiles, 0), 0))]
        out_shape = [jax.ShapeDtypeStruct((n_prompt, d), F32),
                     jax.ShapeDtypeStruct((n - n_prompt, d), F32)]
    else:
        out_specs = pl.BlockSpec((TOKEN_TILE, d), lambda i: (i, 0))
        out_shape = jax.ShapeDtypeStruct((n, d), F32)
    return pl.pallas_call(
        kernel,
        grid=(n // TOKEN_TILE,),
        in_specs=[
            pl.BlockSpec((TOKEN_TILE * TOP_K * cpr, LANES), lambda i: (i, 0)),
            pl.BlockSpec((TOKEN_TILE, d), lambda i: (i, 0)),
            pl.BlockSpec((TOKEN_TILE, LANES), lambda i: (i, 0)),
            pl.BlockSpec((TOKEN_TILE, LANES), lambda i: (i, 0)),
            pl.BlockSpec((None, None, N_MOD, d), lambda i: (l, cond_row(i), 0, 0)),
        ],
        out_specs=out_specs,
        out_shape=out_shape,
        compiler_params=_params(),
        name="combine",
    )(y_rows, x2, g0, g1, mod)


def _dispatch_plan(expert_ids, ranks, counts, n_tiles_max):
    tme = MOE_ROW_TILE
    n_experts = counts.shape[0]
    n = expert_ids.shape[1]
    n_flat = TOP_K * n
    tiles = (counts + tme - 1) // tme
    tile_end = jnp.cumsum(tiles)
    row_off = (tile_end - tiles) * tme
    token = jnp.arange(n, dtype=jnp.int32)
    pos, flat = [], []
    for k in range(TOP_K):
        off = jnp.sum(jnp.where(expert_ids[k][:, None] == jnp.arange(n_experts)[None, :],
                                row_off[None, :], 0), axis=1)
        pos.append(tme + off + ranks[k])
        flat.append(TOP_K * token + k)
    n_active = tile_end[-1]
    n_steps = n_tiles_max + 1
    tile_ids = jnp.minimum(jnp.arange(n_steps), n_active - 1)
    tile_expert = jnp.sum(tile_ids[:, None] >= tile_end[None, :], axis=1).astype(jnp.int32)
    n_dump_tiles = n_experts + 1
    rf = jnp.full(((n_steps + 1) * tme,), -1, jnp.int32).at[jnp.concatenate(pos)].set(
        jnp.concatenate(flat), unique_indices=True)
    is_pad = rf < 0
    pad_slot = jnp.minimum(jnp.cumsum(is_pad.astype(jnp.int32)) - 1, n_dump_tiles * tme - 1)
    rf = jnp.where(is_pad, n_flat + pad_slot, rf)
    return rf, tile_expert, n_active.astype(jnp.int32).reshape(1), n_dump_tiles


def kernel(x_prompt, x_sample, cache_k, cache_v, c, c_ctx, norm1_g, norm2_g, w_ada, b_ada, w_in,
           q_norm_g, k_norm_g, w_pool, pool_scale, rpb, w_out, ffn_w_gate, ffn_w_up, ffn_w_down,
           moe_router, moe_w_gate, moe_w_up, moe_w_down):
    batch, seq, d = x_prompt.shape
    dec_batch, dec_seq, _ = x_sample.shape
    depth = w_in.shape[0]
    heads = cache_k.shape[3]
    na_w = heads * HEAD_DIM
    pool_w = d - na_w
    n_prompt = batch * seq
    n = n_prompt + dec_batch * dec_seq
    n_experts = moe_router.shape[2]
    assert pool_w == len(POOL_WINDOWS) * LANES and na_w % LANES == 0
    assert TOKEN_TILE % seq == 0 and n_prompt % dec_seq == 0 and dec_seq % TOKEN_TILE == 0
    assert dec_seq // GRID_W >= NA_WIN_R and dec_batch < COND_ROWS
    n_prompt_tiles = n_prompt // TOKEN_TILE
    seq_info = dict(n_prompt_tiles=n_prompt_tiles, seq=seq, dec_seq=dec_seq)

    def cond_row(i):
        start = i * TOKEN_TILE
        return jnp.where(start < n_prompt, dec_batch, (start - n_prompt) // dec_seq)

    cond = jnp.zeros((COND_ROWS, d), F32).at[:dec_batch].set(c).at[dec_batch].set(c_ctx)
    mod = _ada(cond, w_ada, b_ada).reshape(depth, COND_ROWS, N_MOD, d)

    hsum = jnp.asarray(np.kron(np.eye(heads), np.ones((HEAD_DIM, HEAD_DIM))), BF16)
    ctx_k = cache_k.reshape(dec_batch, depth, cache_k.shape[2] * heads, HEAD_DIM)
    ctx_v = cache_v.reshape(dec_batch, depth, cache_v.shape[2] * heads, HEAD_DIM)

    x = (x_prompt.reshape(n_prompt, d), x_sample.reshape(-1, d))
    cache_kv = None
    cache_shape = (batch, depth, seq * heads, HEAD_DIM)
    for l in range(depth):
        pu, q, k, v, *cache_kv = _premix(
            x, mod, l, norm1_g[l][None], w_in[l].astype(BF16),
            jnp.tile(q_norm_g[l], heads)[None], jnp.tile(k_norm_g[l], heads)[None], hsum,
            n_prompt, cond_row, cache_kv, cache_shape, seq)
        attn = _attention(q, k, v, ctx_k, ctx_v, _relative_bias_table(rpb[l]), l,
                          n_prompt, seq, dec_seq)
        mix_args = (mod, l, w_pool[l].astype(BF16), pool_scale[l][None], w_out[l].astype(BF16),
                    norm2_g[l][None])
        li = l // 2
        if l % 2 == 0:
            assert ffn_w_gate.shape[2] % FF_CHUNK == 0
            x = _postmix_ffn(x, pu, attn, *mix_args, ffn_w_gate[li].astype(BF16),
                             ffn_w_up[li].astype(BF16), ffn_w_down[li].astype(BF16),
                             seq_info, cond_row)
        else:
            router = jnp.zeros((d, ROUTER_LANES), F32).at[:, :n_experts].set(moe_router[li])
            r_hi = router.astype(BF16)
            r_lo = (router - r_hi.astype(F32)).astype(BF16)
            x2, h_rows, rout, g0, g1, counts = _postmix_router(
                x, pu, attn, *mix_args, r_hi, r_lo, n_experts, seq_info, cond_row)
            routing = rout[:, :2 * TOP_K].T.astype(jnp.int32)
            n_tiles_max = (TOP_K * n + n_experts * (MOE_ROW_TILE - 1)) // MOE_ROW_TILE
            row_flat, tile_expert, n_active, n_dump_tiles = _dispatch_plan(
                routing[:TOP_K], routing[TOP_K:], counts[0, :n_experts].astype(jnp.int32),
                n_tiles_max)
            y_rows = _moe(tile_expert, n_active, row_flat, h_rows,
                          moe_w_gate[li], moe_w_up[li], moe_w_down[li], d, TOP_K * n, n_dump_tiles)
            x = _combine(y_rows, x2, g0, g1, mod, l, cond_row, n_prompt,
                         split_output=(l == depth - 1))

    if not isinstance(x, tuple):
        x = (x[:n_prompt], x[n_prompt:])
    y_prompt = x[0].reshape(batch, seq, d)
    y_sample = x[1].reshape(dec_batch, dec_seq, d)
    new_k, new_v = (a.reshape(batch, depth, seq, heads, HEAD_DIM) for a in cache_kv)
    return (y_prompt, y_sample, new_k, new_v)
```

```python
import functools

import numpy as np
import jax
import jax.numpy as jnp
from jax import lax
from jax.experimental import pallas as pl
from jax.experimental.pallas import tpu as pltpu

F32 = jnp.float32
BF16 = jnp.bfloat16

GRID_W = 64
POOL_WINDOWS = (2, 4, 8, 16)
HEAD_DIM = 64
NA_WIN_R = 8
NA_WIN_C = 16
N_MOD = 6
TOP_K = 2
NORM_EPS = 1e-6
LOG2_E = 1.4426950408889634

LANES = 128
SUBLANES = 8
VMEM_LIMIT_BYTES = 56 * 1024 * 1024

TOKEN_TILE = 512
POOL_HALO = 8
FF_CHUNK = 256
MOE_ROW_TILE = 1008
MOE_F_CHUNK = 512
ADA_COL_TILE = 1024
COND_ROWS = 16
ROUTER_LANES = 128


def _params(n_axes=1):
    return pltpu.CompilerParams(
        dimension_semantics=("arbitrary",) * n_axes,
        vmem_limit_bytes=VMEM_LIMIT_BYTES,
    )


def _resident(shape, index_map):
    return pl.BlockSpec(shape, index_map, pipeline_mode=pl.Buffered(1))


def _x_specs(x, n_prompt_tiles):
    if not isinstance(x, tuple):
        return [pl.BlockSpec((TOKEN_TILE, x.shape[1]), lambda i: (i, 0))]
    d = x[0].shape[1]
    return [pl.BlockSpec((TOKEN_TILE, d), lambda i: (jnp.minimum(i, n_prompt_tiles - 1), 0)),
            pl.BlockSpec((TOKEN_TILE, d), lambda i: (jnp.maximum(i - n_prompt_tiles, 0), 0))]


def _load_x(i, x_refs, n_prompt_tiles):
    if len(x_refs) == 1:
        return x_refs[0][...]
    return jnp.where(i < n_prompt_tiles, x_refs[0][...], x_refs[1][...])


def _as_tuple(x):
    return x if isinstance(x, tuple) else (x,)


def _ada_kernel(c_ref, w_ref, b_ref, o_ref):
    c = c_ref[...]
    s = c * jax.nn.sigmoid(c)
    o_ref[...] = jnp.dot(s.astype(BF16), w_ref[...].astype(BF16),
                         preferred_element_type=F32) + b_ref[...]


def _ada(cond, w_ada, b_ada):
    depth, d, width = w_ada.shape
    return pl.pallas_call(
        _ada_kernel,
        grid=(depth, width // ADA_COL_TILE),
        in_specs=[
            pl.BlockSpec((COND_ROWS, d), lambda l, j: (0, 0)),
            pl.BlockSpec((None, d, ADA_COL_TILE), lambda l, j: (l, 0, j)),
            pl.BlockSpec((None, 1, ADA_COL_TILE), lambda l, j: (l, 0, j)),
        ],
        out_specs=pl.BlockSpec((None, COND_ROWS, ADA_COL_TILE), lambda l, j: (l, 0, j)),
        out_shape=jax.ShapeDtypeStruct((depth, COND_ROWS, width), F32),
        compiler_params=_params(2),
        name="ada",
    )(cond, w_ada, b_ada.reshape(depth, 1, width))


def _premix_kernel(*refs, n_x, n_prompt_tiles, pool_w, na_w, seq, layer, creates_cache):
    n_in = n_x + (6 if creates_cache else 8)
    mod_ref, g_ref, w_ref, qg_ref, kg_ref, hsum_ref = refs[n_x:n_x + 6]
    pu_ref, q_ref, k_ref, v_ref, kf_ref, vf_ref = refs[n_in:]
    i = pl.program_id(0)
    x = _load_x(i, refs[:n_x], n_prompt_tiles)
    inv = lax.rsqrt(jnp.mean(x * x, axis=-1, keepdims=True) + NORM_EPS)
    mod = mod_ref[...]
    h = (x * inv * g_ref[...]) * (1.0 + mod[1:2]) + mod[0:1]
    u = jnp.dot(h.astype(BF16), w_ref[...], preferred_element_type=F32)
    pu_ref[...] = u[:, :pool_w]
    q = u[:, pool_w:pool_w + na_w]
    k = u[:, pool_w + na_w:pool_w + 2 * na_w]
    v = u[:, pool_w + 2 * na_w:]

    def head_norm(t, g):
        ms = jnp.dot((t * t).astype(BF16), hsum_ref[...],
                     preferred_element_type=F32) * (1.0 / HEAD_DIM)
        return t * lax.rsqrt(ms + NORM_EPS) * g

    qn = head_norm(q, qg_ref[...])
    kn = head_norm(k, kg_ref[...])
    q_ref[...] = (qn * (HEAD_DIM ** -0.5 * LOG2_E)).astype(BF16)
    k_ref[...] = kn.astype(BF16)
    v_ref[...] = v.astype(BF16)

    @pl.when(i < n_prompt_tiles)
    def _():
        heads = na_w // HEAD_DIM
        if creates_cache:
            for lz in range(kf_ref.shape[1]):
                if lz != layer:
                    kf_ref[:, lz] = jnp.zeros((kf_ref.shape[0],) + kf_ref.shape[2:], F32)
                    vf_ref[:, lz] = jnp.zeros((vf_ref.shape[0],) + vf_ref.shape[2:], F32)
            kf_l, vf_l = kf_ref.at[:, layer], vf_ref.at[:, layer]
        else:
            kf_l, vf_l = kf_ref, vf_ref
        for b in range(TOKEN_TILE // seq):
            for hd in range(heads):
                rows, cols = slice(b * seq, (b + 1) * seq), slice(hd * HEAD_DIM, (hd + 1) * HEAD_DIM)
                kf_l[b, pl.ds(hd, seq, stride=heads), :] = kn[rows, cols]
                vf_l[b, pl.ds(hd, seq, stride=heads), :] = v[rows, cols]


def _premix(x, mod, l, norm_g, w_in, q_g, k_g, hsum, n_prompt, cond_row, cache_kv, cache_shape,
            seq):
    xs = _as_tuple(x)
    creates_cache = cache_kv is None
    n, d = sum(a.shape[0] for a in xs), xs[0].shape[1]
    pool_w = d // 2
    na_w = d - pool_w
    n_tiles = n // TOKEN_TILE
    n_prompt_tiles = n_prompt // TOKEN_TILE
    last_p = n_prompt_tiles - 1
    tok = lambda w: pl.BlockSpec((TOKEN_TILE, w), lambda i: (i, 0))
    if creates_cache:
        cache_spec = pl.BlockSpec((TOKEN_TILE // seq,) + cache_shape[1:],
                                  lambda i: (jnp.minimum(i, last_p), 0, 0, 0))
        cache_in, cache_in_specs, aliases = (), [], {}
    else:
        cache_spec = pl.BlockSpec((TOKEN_TILE // seq, None) + cache_shape[2:],
                                  lambda i: (jnp.minimum(i, last_p), l, 0, 0))
        cache_in, cache_in_specs = tuple(cache_kv), [pl.BlockSpec(memory_space=pl.ANY)] * 2
        aliases = {len(xs) + 6: 4, len(xs) + 7: 5}
    kernel = functools.partial(_premix_kernel, n_x=len(xs), n_prompt_tiles=n_prompt_tiles,
                               pool_w=pool_w, na_w=na_w, seq=seq, layer=l,
                               creates_cache=creates_cache)
    return pl.pallas_call(
        kernel,
        grid=(n_tiles,),
        in_specs=_x_specs(x, n_prompt_tiles) + [
            pl.BlockSpec((None, None, N_MOD, d), lambda i: (l, cond_row(i), 0, 0)),
            pl.BlockSpec((1, d), lambda i: (0, 0)),
            _resident((d, w_in.shape[1]), lambda i: (0, 0)),
            pl.BlockSpec((1, na_w), lambda i: (0, 0)),
            pl.BlockSpec((1, na_w), lambda i: (0, 0)),
            _resident((na_w, na_w), lambda i: (0, 0)),
        ] + cache_in_specs,
        out_specs=[tok(pool_w), tok(na_w), tok(na_w), tok(na_w), cache_spec, cache_spec],
        out_shape=[
            jax.ShapeDtypeStruct((n, pool_w), F32),
            jax.ShapeDtypeStruct((n, na_w), BF16),
            jax.ShapeDtypeStruct((n, na_w), BF16),
            jax.ShapeDtypeStruct((n, na_w), BF16),
            jax.ShapeDtypeStruct(cache_shape, F32),
            jax.ShapeDtypeStruct(cache_shape, F32),
        ],
        input_output_aliases=aliases,
        compiler_params=_params(),
        name=f"premix{l}",
    )(*xs, mod, norm_g, w_in, q_g, k_g, hsum, *cache_in)


_NT = (((1,), (1,)), ((), ()))
_TN = (((0,), (0,)), ((), ()))


def _block_diag_queries(q2):
    lo = lax.broadcasted_iota(jnp.int32, q2.shape, 1) < HEAD_DIM
    zero = jnp.zeros_like(q2)
    return jnp.concatenate([jnp.where(lo, q2, zero), jnp.where(lo, zero, q2)], axis=0)


def _pick_head_blocks(o, nq):
    lo = lax.broadcasted_iota(jnp.int32, (nq, LANES), 1) < HEAD_DIM
    return jnp.where(lo, o[:nq], o[nq:])


def _pair_attention(q2, k, v):
    nq = q2.shape[0]
    s = lax.dot_general(k, _block_diag_queries(q2), _NT, preferred_element_type=F32)
    p = jnp.exp2(s - jnp.max(s, axis=0, keepdims=True))
    r = 1.0 / jnp.sum(p, axis=0, keepdims=True)
    o = lax.dot_general((p * r).astype(BF16), v, _TN, preferred_element_type=F32)
    return _pick_head_blocks(o, nq)


def _attn_kernel(q_ref, k_ref, v_ref, ckf_ref, cvf_ref, bias_ref, o_ref, s_ref, p_ref,
                 ck_ref, cv_ref, *, n_prompt_tiles, seq, dec_seq, n_pairs):
    i = pl.program_id(0)
    tiles_per_seq = dec_seq // TOKEN_TILE
    rows_per_tile = TOKEN_TILE // GRID_W
    rows = dec_seq // GRID_W
    win_keys = NA_WIN_R * GRID_W
    block_off = (i * TOKEN_TILE) % dec_seq

    @pl.when(i < n_prompt_tiles)
    def _():
        for s in range(TOKEN_TILE // seq):
            start = pl.multiple_of(block_off + s * seq, seq)
            for hp in range(n_pairs):
                lanes = slice(hp * LANES, (hp + 1) * LANES)
                out = _pair_attention(q_ref[s * seq:(s + 1) * seq, lanes],
                                      k_ref[pl.ds(start, seq), lanes],
                                      v_ref[pl.ds(start, seq), lanes])
                o_ref[s * seq:(s + 1) * seq, lanes] = out.astype(o_ref.dtype)

    @pl.when(i >= n_prompt_tiles)
    def _():
        tile_in_seq = (i - n_prompt_tiles) % tiles_per_seq
        row0 = tile_in_seq * rows_per_tile

        @pl.when(tile_in_seq == 0)
        def _():
            past = ck_ref.shape[0]
            heads = 2 * n_pairs
            for src, dst in ((ckf_ref, ck_ref), (cvf_ref, cv_ref)):
                for hp in range(n_pairs):
                    pair = [src[pl.ds(2 * hp + t, past, stride=heads), :] for t in range(2)]
                    dst[:, hp * LANES:(hp + 1) * LANES] = (
                        jnp.concatenate(pair, axis=-1).astype(BF16))

        def indices(rl):
            r = row0 + rl
            r0 = jnp.clip(r - NA_WIN_R // 2, 0, rows - NA_WIN_R)
            return r0 - r + NA_WIN_R - 1, pl.multiple_of(r0 * GRID_W, GRID_W)

        def scores(rl):
            d0, kstart = indices(rl)
            for hp in range(n_pairs):
                lanes = slice(hp * LANES, (hp + 1) * LANES)
                qbd = _block_diag_queries(q_ref[rl * GRID_W:(rl + 1) * GRID_W, lanes])
                bias = bias_ref[hp, pl.ds(d0, NA_WIN_R)].reshape(win_keys, LANES)
                s_ref[rl % 2, hp, 0:win_keys] = lax.dot_general(
                    k_ref[pl.ds(kstart, win_keys), lanes], qbd, _NT,
                    preferred_element_type=F32) + bias
                s_ref[rl % 2, hp, win_keys:] = lax.dot_general(
                    ck_ref[:, lanes], qbd, _NT, preferred_element_type=F32)

        def softmax(rl):
            for hp in range(n_pairs):
                s = s_ref[rl % 2, hp]
                p = jnp.exp2(s - jnp.max(s, axis=0, keepdims=True))
                rr = 1.0 / jnp.sum(p, axis=0, keepdims=True)
                p_ref[rl % 2, hp] = (p * rr).astype(BF16)

        def values(rl):
            _, kstart = indices(rl)
            for hp in range(n_pairs):
                lanes = slice(hp * LANES, (hp + 1) * LANES)
                o = (lax.dot_general(p_ref[rl % 2, hp, 0:win_keys],
                                     v_ref[pl.ds(kstart, win_keys), lanes],
                                     _TN, preferred_element_type=F32)
                     + lax.dot_general(p_ref[rl % 2, hp, win_keys:], cv_ref[:, lanes], _TN,
                                       preferred_element_type=F32))
                o_ref[rl * GRID_W:(rl + 1) * GRID_W, lanes] = (
                    _pick_head_blocks(o, GRID_W).astype(o_ref.dtype))

        for step in range(rows_per_tile + 2):
            if step < rows_per_tile:
                scores(step)
            if 1 <= step <= rows_per_tile:
                softmax(step - 1)
            if step >= 2:
                values(step - 2)


def _attention(q, k, v, ctx_k, ctx_v, bias, l, n_prompt, seq, dec_seq):
    n, na_w = q.shape
    n_tiles = n // TOKEN_TILE
    n_prompt_tiles = n_prompt // TOKEN_TILE
    tiles_per_seq = dec_seq // TOKEN_TILE
    heads = na_w // HEAD_DIM
    past = ctx_k.shape[2] // heads
    n_pairs = na_w // LANES
    n_keys = NA_WIN_R * GRID_W + past
    kv_spec = pl.BlockSpec((dec_seq, na_w), lambda i: (i * TOKEN_TILE // dec_seq, 0))
    ctx_spec = pl.BlockSpec(
        (None, None, past * heads, HEAD_DIM),
        lambda i: (jnp.maximum(i - n_prompt_tiles, 0) // tiles_per_seq, l, 0, 0))
    kernel = functools.partial(_attn_kernel, n_prompt_tiles=n_prompt_tiles, seq=seq,
                               dec_seq=dec_seq, n_pairs=n_pairs)
    return pl.pallas_call(
        kernel,
        grid=(n_tiles,),
        in_specs=[
            pl.BlockSpec((TOKEN_TILE, na_w), lambda i: (i, 0)),
            kv_spec, kv_spec, ctx_spec, ctx_spec,
            _resident(bias.shape, lambda i: (0, 0, 0, 0)),
        ],
        out_specs=pl.BlockSpec((TOKEN_TILE, na_w), lambda i: (i, 0)),
        out_shape=jax.ShapeDtypeStruct((n, na_w), BF16),
        scratch_shapes=[pltpu.VMEM((2, n_pairs, n_keys, LANES), F32),
                        pltpu.VMEM((2, n_pairs, n_keys, LANES), BF16),
                        pltpu.VMEM((past, na_w), BF16),
                        pltpu.VMEM((past, na_w), BF16)],
        compiler_params=_params(),
        name=f"attn{l}",
    )(q, k, v, ctx_k, ctx_v, bias)


def _relative_bias_table(rpb_l):
    heads = rpb_l.shape[0]
    kc = np.arange(GRID_W)[:, None]
    qc = np.arange(GRID_W)[None, :]
    q_start = np.clip(qc - NA_WIN_C // 2, 0, GRID_W - NA_WIN_C)
    valid = (kc >= q_start) & (kc < q_start + NA_WIN_C)
    dc_idx = np.clip(kc - qc, -(NA_WIN_C - 1), NA_WIN_C - 1) + NA_WIN_C - 1
    t = jnp.zeros(rpb_l.shape[:2] + dc_idx.shape, F32)
    for c in range(rpb_l.shape[2]):
        t = jnp.where(dc_idx[None, None] == c, rpb_l[:, :, c, None, None].astype(F32), t)
    t = jnp.where(valid[None, None], t * LOG2_E, -jnp.inf)
    t = t.reshape(heads // 2, 2, 2 * NA_WIN_R - 1, GRID_W, GRID_W)
    return t.transpose(0, 2, 3, 1, 4).reshape(heads // 2, 2 * NA_WIN_R - 1, GRID_W, 2 * GRID_W)


def _mod_static(t, m):
    return t & (m - 1) if m & (m - 1) == 0 else lax.rem(t, m)


def _pool_mix(i, b, pu_ref, pp_ref, pn_ref, ext_ref, wp_ref, ps_ref, *, n_prompt_tiles, seq, dec_seq):
    sub = seq
    h = POOL_HALO
    is_prompt = i < n_prompt_tiles
    seq_len = jnp.where(is_prompt, seq, dec_seq)
    lo, hi = b * sub, (b + 1) * sub
    tok0 = i * TOKEN_TILE + lo
    pos0 = jnp.where(is_prompt, _mod_static(tok0, seq), _mod_static(tok0, dec_seq))
    prev = pp_ref[...] if lo == 0 else pu_ref[lo - h:lo, :]
    nxt = pn_ref[...] if hi == TOKEN_TILE else pu_ref[hi:hi + h, :]
    ext_ref[b, 0:h] = jnp.where(pos0 != 0, prev, 0.0)
    ext_ref[b, h:h + sub] = pu_ref[lo:hi, :]
    ext_ref[b, h + sub:] = jnp.where(pos0 + sub != seq_len, nxt, 0.0)
    left = pos0 + lax.broadcasted_iota(jnp.int32, (sub, LANES), 0)
    right = seq_len - left
    outs = []
    for g, window in enumerate(POOL_WINDOWS):
        half = window // 2
        lanes = slice(g * LANES, (g + 1) * LANES)
        total = ext_ref[b, h - half:h - half + sub, lanes]
        for j in range(1 - half, half):
            total = total + ext_ref[b, h + j:h + j + sub, lanes]
        count = (jnp.minimum(left, half) + jnp.minimum(right, half)).astype(F32)
        pooled = total / count - pu_ref[lo:hi, lanes]
        outs.append(jnp.dot(pooled.astype(BF16), wp_ref[g], preferred_element_type=F32))
    return jnp.concatenate(outs, axis=-1) * ps_ref[...]


def _postmix_chain(i, x_refs, pu_ref, pp_ref, pn_ref, at_ref, mod_ref, wp_ref, ps_ref, wo_ref,
                   g2_ref, ext_ref, tail, **seq_info):
    sub = seq_info["seq"]
    n_sub = TOKEN_TILE // sub
    mod = mod_ref[...]
    is_prompt = i < seq_info["n_prompt_tiles"]

    def rows(b):
        return slice(b * sub, (b + 1) * sub)

    def pool_stage(b, _):
        return _pool_mix(i, b, pu_ref, pp_ref, pn_ref, ext_ref, wp_ref, ps_ref, **seq_info)

    def mix_stage(b, pool_out):
        mixed = jnp.concatenate([pool_out.astype(BF16), at_ref[rows(b), :]], axis=-1)
        mix = jnp.dot(mixed, wo_ref[...], preferred_element_type=F32)
        if len(x_refs) == 1:
            x = x_refs[0][rows(b), :]
        else:
            x = jnp.where(is_prompt, x_refs[0][rows(b), :], x_refs[1][rows(b), :])
        return x + mod[2:3] * mix

    def norm_stage(b, x2):
        inv = lax.rsqrt(jnp.mean(x2 * x2, axis=-1, keepdims=True) + NORM_EPS)
        return x2, (x2 * inv * g2_ref[...]) * (1.0 + mod[4:5]) + mod[3:4]

    def tail_stage(b, x2_h2):
        tail(rows(b), *x2_h2)

    stages = (pool_stage, mix_stage, norm_stage, tail_stage)
    state = [None] * n_sub
    for step in range(n_sub + len(stages) - 1):
        for b in range(n_sub):
            if 0 <= step - b < len(stages):
                state[b] = stages[step - b](b, state[b])
    return mod[5:6]


def _postmix_ffn_kernel(*refs, n_x, **seq_info):
    (pu_ref, pp_ref, pn_ref, at_ref, mod_ref, wp_ref, ps_ref, wo_ref,
     g2_ref, wg_ref, wu_ref, wd_ref, y_ref, ext_ref, h_ref, acc_ref) = refs[n_x:]
    i = pl.program_id(0)

    def tail(rows, x2, h2):
        y_ref[rows, :] = x2
        h_ref[rows, :] = h2.astype(BF16)

    gate2 = _postmix_chain(i, refs[:n_x], pu_ref, pp_ref, pn_ref, at_ref, mod_ref, wp_ref,
                           ps_ref, wo_ref, g2_ref, ext_ref, tail, **seq_info)
    acc_ref[...] = jnp.zeros_like(acc_ref)

    h = h_ref[...]
    for c in range(wg_ref.shape[1] // FF_CHUNK):
        cols = slice(c * FF_CHUNK, (c + 1) * FF_CHUNK)
        a = jnp.dot(h, wg_ref[:, cols], preferred_element_type=F32)
        b = jnp.dot(h, wu_ref[:, cols], preferred_element_type=F32)
        act = (a * jax.nn.sigmoid(a) * b).astype(BF16)
        acc_ref[...] += jnp.dot(act, wd_ref[cols, :], preferred_element_type=F32)
    y_ref[...] = y_ref[...] + gate2 * acc_ref[...]


def _postmix_router_kernel(*refs, n_x, n_experts, **seq_info):
    (pu_ref, pp_ref, pn_ref, at_ref, mod_ref, wp_ref, ps_ref, wo_ref,
     g2_ref, rh_ref, rl_ref, tri_ref, x2_ref, hr_ref, rout_ref, g0_ref, g1_ref, cnt_ref,
     ext_ref) = refs[n_x:]
    i = pl.program_id(0)

    @pl.when(i == 0)
    def _():
        cnt_ref[...] = jnp.zeros_like(cnt_ref)

    def tail(rows, x2, h2):
        x2_ref[rows, :] = x2
        n_rows = rows.stop - rows.start
        for s in range(h2.shape[1] // LANES):
            hr_ref[pl.ds(rows.start * SUBLANES + s, n_rows, stride=SUBLANES), :] = (
                h2[:, s * LANES:(s + 1) * LANES])
        hi = h2.astype(BF16)
        lo = (h2 - hi.astype(F32)).astype(BF16)
        logits = (jnp.dot(hi, rh_ref[...], preferred_element_type=F32)
                  + jnp.dot(lo, rh_ref[...], preferred_element_type=F32)
                  + jnp.dot(hi, rl_ref[...], preferred_element_type=F32))
        lane = lax.broadcasted_iota(jnp.int32, logits.shape, 1).astype(F32)
        logits = jnp.where(lane < n_experts, logits, -jnp.inf)
        m1 = jnp.max(logits, axis=-1, keepdims=True)
        i1 = jnp.min(jnp.where(logits == m1, lane, float(ROUTER_LANES)), axis=-1, keepdims=True)
        rest = jnp.where(lane == i1, -jnp.inf, logits)
        m2 = jnp.max(rest, axis=-1, keepdims=True)
        i2 = jnp.min(jnp.where(rest == m2, lane, float(ROUTER_LANES)), axis=-1, keepdims=True)
        e = jnp.exp(m2 - m1)
        pick0 = jnp.where(lane == i1, 1.0, 0.0)
        pick1 = jnp.where(lane == i2, 1.0, 0.0)
        picks = pick0 + pick1
        before = cnt_ref[...] + jnp.dot(tri_ref[...], picks.astype(BF16),
                                        preferred_element_type=F32)
        rank0 = jnp.sum(before * pick0, axis=-1, keepdims=True)
        rank1 = jnp.sum(before * pick1, axis=-1, keepdims=True)
        cnt_ref[...] = before[n_rows - 1:n_rows, :] + picks[n_rows - 1:n_rows, :]
        packed_t = jnp.where(lane == 0, i1, jnp.where(lane == 1, i2, jnp.where(
            lane == 2, rank0, jnp.where(lane == 3, rank1, 0.0)))).T
        groups = n_rows // LANES
        base = (rows.start // n_rows) * 2 * TOP_K * groups
        for q in range(2 * TOP_K):
            for hh in range(groups):
                rout_ref[base + q * groups + hh:base + q * groups + hh + 1, :] = (
                    packed_t[q:q + 1, hh * LANES:(hh + 1) * LANES])
        g0_ref[rows, :] = jnp.broadcast_to(1.0 / (1.0 + e), (n_rows, LANES))
        g1_ref[rows, :] = jnp.broadcast_to(e / (1.0 + e), (n_rows, LANES))

    _postmix_chain(i, refs[:n_x], pu_ref, pp_ref, pn_ref, at_ref, mod_ref, wp_ref,
                   ps_ref, wo_ref, g2_ref, ext_ref, tail, **seq_info)


def _pool_scratch_shape(seq_info, pool_w):
    sub = seq_info["seq"]
    return (TOKEN_TILE // sub, sub + 2 * POOL_HALO, pool_w)


def _postmix_specs(x, n, d, pool_w, na_w, l, cond_row, n_prompt_tiles):
    n_halo_blocks = n // POOL_HALO
    per_tile = TOKEN_TILE // POOL_HALO
    tok = lambda w: pl.BlockSpec((TOKEN_TILE, w), lambda i: (i, 0))
    return _x_specs(x, n_prompt_tiles) + [
        tok(pool_w),
        pl.BlockSpec((POOL_HALO, pool_w), lambda i: (jnp.maximum(i * per_tile - 1, 0), 0)),
        pl.BlockSpec((POOL_HALO, pool_w),
                     lambda i: (jnp.minimum((i + 1) * per_tile, n_halo_blocks - 1), 0)),
        tok(na_w),
        pl.BlockSpec((None, None, N_MOD, d), lambda i: (l, cond_row(i), 0, 0)),
        _resident((len(POOL_WINDOWS), LANES, LANES), lambda i: (0, 0, 0)),
        pl.BlockSpec((1, pool_w), lambda i: (0, 0)),
        _resident((d, d), lambda i: (0, 0)),
        pl.BlockSpec((1, d), lambda i: (0, 0)),
    ]


def _postmix_ffn(x, pu, attn, mod, l, w_pool, pool_scale, w_out, norm_g, wg, wu, wd,
                 seq_info, cond_row):
    xs = _as_tuple(x)
    n, d = pu.shape[0], xs[0].shape[1]
    pool_w = pu.shape[1]
    d_ff = wg.shape[1]
    kernel = functools.partial(_postmix_ffn_kernel, n_x=len(xs), **seq_info)
    return pl.pallas_call(
        kernel,
        grid=(n // TOKEN_TILE,),
        in_specs=_postmix_specs(x, n, d, pool_w, attn.shape[1], l, cond_row,
                                seq_info["n_prompt_tiles"]) + [
            _resident((d, d_ff), lambda i: (0, 0)),
            _resident((d, d_ff), lambda i: (0, 0)),
            _resident((d_ff, d), lambda i: (0, 0)),
        ],
        out_specs=pl.BlockSpec((TOKEN_TILE, d), lambda i: (i, 0)),
        out_shape=jax.ShapeDtypeStruct((n, d), F32),
        scratch_shapes=[
            pltpu.VMEM(_pool_scratch_shape(seq_info, pool_w), F32),
            pltpu.VMEM((TOKEN_TILE, d), BF16),
            pltpu.VMEM((TOKEN_TILE, d), F32),
        ],
        compiler_params=_params(),
        name=f"postmix_ffn{l}",
    )(*xs, pu, pu, pu, attn, mod, w_pool, pool_scale, w_out, norm_g, wg, wu, wd)


def _postmix_router(x, pu, attn, mod, l, w_pool, pool_scale, w_out, norm_g, r_hi, r_lo,
                    n_experts, seq_info, cond_row):
    xs = _as_tuple(x)
    sub = seq_info["seq"]
    tri = jnp.asarray(np.tril(np.ones((sub, sub)), -1), BF16)
    n, d = pu.shape[0], xs[0].shape[1]
    pool_w = pu.shape[1]
    kernel = functools.partial(_postmix_router_kernel, n_x=len(xs), n_experts=n_experts,
                               **seq_info)
    row_tile = TOKEN_TILE * d // LANES
    return pl.pallas_call(
        kernel,
        grid=(n // TOKEN_TILE,),
        in_specs=_postmix_specs(x, n, d, pool_w, attn.shape[1], l, cond_row,
                                seq_info["n_prompt_tiles"]) + [
            pl.BlockSpec((d, ROUTER_LANES), lambda i: (0, 0)),
            pl.BlockSpec((d, ROUTER_LANES), lambda i: (0, 0)),
            pl.BlockSpec((sub, sub), lambda i: (0, 0)),
        ],
        out_specs=[
            pl.BlockSpec((TOKEN_TILE, d), lambda i: (i, 0)),
            pl.BlockSpec((row_tile, LANES), lambda i: (i, 0)),
            pl.BlockSpec((2 * TOP_K * TOKEN_TILE // LANES, LANES), lambda i: (i, 0)),
            pl.BlockSpec((TOKEN_TILE, LANES), lambda i: (i, 0)),
            pl.BlockSpec((TOKEN_TILE, LANES), lambda i: (i, 0)),
            pl.BlockSpec((1, ROUTER_LANES), lambda i: (0, 0)),
        ],
        out_shape=[
            jax.ShapeDtypeStruct((n, d), F32),
            jax.ShapeDtypeStruct((n * d // LANES, LANES), F32),
            jax.ShapeDtypeStruct((2 * TOP_K * n // LANES, LANES), F32),
            jax.ShapeDtypeStruct((n, LANES), F32),
            jax.ShapeDtypeStruct((n, LANES), F32),
            jax.ShapeDtypeStruct((1, ROUTER_LANES), F32),
        ],
        scratch_shapes=[pltpu.VMEM(_pool_scratch_shape(seq_info, pool_w), F32)],
        compiler_params=_params(),
        name=f"postmix_router{l}",
    )(*xs, pu, pu, pu, attn, mod, w_pool, pool_scale, w_out, norm_g, r_hi, r_lo, tri)


def _moe_kernel(te_ref, na_ref, rf_ref, h_hbm, wg_ref, wu_ref, wd_ref, y_hbm,
                xbuf, ybuf, xb16, acc_ref, gsem, ssem, zsem, *,
                cpr, rows_per_step, n_flat, n_dump_tiles):
    j = pl.program_id(0)
    f = pl.program_id(1)
    n_f = pl.num_programs(1)
    tme = MOE_ROW_TILE
    tile_rows = tme * cpr
    n_active = na_ref[0]
    last_token = n_flat // TOP_K - 1

    def gather_row(tile, slot, row):
        v = rf_ref[(tile + 1) * tme + row]
        tok = jnp.minimum(lax.shift_right_logical(v, 1), last_token)
        return pltpu.make_async_copy(
            h_hbm.at[pl.ds(pl.multiple_of(tok * cpr, cpr), cpr)],
            xbuf.at[slot, pl.ds(pl.multiple_of(row * cpr, cpr), cpr)],
            gsem.at[slot])

    def scatter_row(tile, slot, row):
        v = rf_ref[(tile + 1) * tme + row]
        return pltpu.make_async_copy(
            ybuf.at[slot, pl.ds(pl.multiple_of(row * cpr, cpr), cpr)],
            y_hbm.at[pl.ds(pl.multiple_of(v * cpr, cpr), cpr)],
            ssem.at[slot])

    def wait_gather(slot):
        pltpu.make_async_copy(h_hbm.at[pl.ds(0, tile_rows)], xbuf.at[slot], gsem.at[slot]).wait()

    def wait_scatter(slot):
        pltpu.make_async_copy(ybuf.at[slot], y_hbm.at[pl.ds(0, tile_rows)], ssem.at[slot]).wait()

    def dump_fill(t):
        return pltpu.make_async_copy(
            ybuf.at[1], y_hbm.at[pl.ds((n_flat + t * tme) * cpr, tile_rows)], zsem)

    @pl.when((f == 0) & (j == 0))
    def _():
        ybuf[1] = jnp.zeros(ybuf.shape[1:], ybuf.dtype)
        for t in range(n_dump_tiles):
            dump_fill(t).start()
        for t in range(n_dump_tiles):
            dump_fill(t).wait()

        def body(r, carry):
            gather_row(0, 0, r).start()
            return carry

        lax.fori_loop(0, tme, body, 0)

    def step(par, first, last):
        if first:
            wait_gather(par)
            for s in range(cpr):
                xb16[:, s * LANES:(s + 1) * LANES] = (
                    xbuf[par, pl.ds(s, tme, stride=cpr), :].astype(BF16))
        for t in range(rows_per_step):
            gather_row(j + 1, 1 - par, t * n_f + f).start()
            scatter_row(j - 1, 1 - par, t * n_f + f).start()
        x = xb16[...]
        a = jnp.dot(x, wg_ref[...].astype(BF16), preferred_element_type=F32)
        b = jnp.dot(x, wu_ref[...].astype(BF16), preferred_element_type=F32)
        act = (a * jax.nn.sigmoid(a) * b).astype(BF16)
        part = jnp.dot(act, wd_ref[...].astype(BF16), preferred_element_type=F32)
        total = part if first else acc_ref[...] + part
        if not last:
            acc_ref[...] = total
        else:
            pl.when(j >= 1)(lambda: wait_scatter(par))
            for s in range(cpr):
                ybuf[par, pl.ds(s, tme, stride=cpr), :] = total[:, s * LANES:(s + 1) * LANES]

    for par in range(2):
        on = (j < n_active) & (j % 2 == par)
        pl.when(on & (f == 0))(functools.partial(step, par, True, False))
        pl.when(on & (f > 0) & (f < n_f - 1))(functools.partial(step, par, False, False))
        pl.when(on & (f == n_f - 1))(functools.partial(step, par, False, True))

        @pl.when((j == n_active) & (f == 0) & (j % 2 == par))
        def _(par=par):
            def body(r, carry):
                scatter_row(j - 1, 1 - par, r).start()
                return carry

            lax.fori_loop(0, tme, body, 0)
            wait_gather(par)
            wait_scatter(par)
            wait_scatter(1 - par)


def _moe(tile_expert, n_active, row_flat, h_rows, wg, wu, wd, d, n_flat, n_dump_tiles):
    n_steps = tile_expert.shape[0]
    d_exp = wg.shape[2]
    n_f = d_exp // MOE_F_CHUNK
    cpr = d // LANES
    tile_rows = MOE_ROW_TILE * cpr
    assert MOE_ROW_TILE % n_f == 0 and n_f >= 2

    def f_idx(j, f, na):
        return jnp.where(j < na[0], f, n_f - 1)

    grid_spec = pltpu.PrefetchScalarGridSpec(
        num_scalar_prefetch=3,
        grid=(n_steps, n_f),
        in_specs=[
            pl.BlockSpec(memory_space=pl.ANY),
            pl.BlockSpec((None, d, MOE_F_CHUNK), lambda j, f, te, na, rf: (te[j], 0, f_idx(j, f, na))),
            pl.BlockSpec((None, d, MOE_F_CHUNK), lambda j, f, te, na, rf: (te[j], 0, f_idx(j, f, na))),
            pl.BlockSpec((None, MOE_F_CHUNK, d), lambda j, f, te, na, rf: (te[j], f_idx(j, f, na), 0)),
        ],
        out_specs=pl.BlockSpec(memory_space=pl.ANY),
        scratch_shapes=[
            pltpu.VMEM((2, tile_rows, LANES), F32),
            pltpu.VMEM((2, tile_rows, LANES), F32),
            pltpu.VMEM((MOE_ROW_TILE, d), BF16),
            pltpu.VMEM((MOE_ROW_TILE, d), F32),
            pltpu.SemaphoreType.DMA((2,)),
            pltpu.SemaphoreType.DMA((2,)),
            pltpu.SemaphoreType.DMA,
        ],
    )
    kernel = functools.partial(_moe_kernel, cpr=cpr, rows_per_step=MOE_ROW_TILE // n_f,
                               n_flat=n_flat, n_dump_tiles=n_dump_tiles)
    return pl.pallas_call(
        kernel,
        grid_spec=grid_spec,
        out_shape=jax.ShapeDtypeStruct(((n_flat + n_dump_tiles * MOE_ROW_TILE) * cpr, LANES), F32),
        compiler_params=_params(2),
        name="moe",
    )(tile_expert, n_active, row_flat, h_rows, wg, wu, wd)


def _combine_kernel(y_ref, x2_ref, g0_ref, g1_ref, mod_ref, *o_refs, cpr, n_prompt_tiles):
    i = pl.program_id(0)
    tm = x2_ref.shape[0]
    gate2 = mod_ref[...][5:6]
    stride = TOP_K * cpr

    def write(o_ref):
        g0 = g0_ref[...]
        g1 = g1_ref[...]
        for s in range(cpr):
            lanes = slice(s * LANES, (s + 1) * LANES)
            ya = y_ref[pl.ds(s, tm, stride=stride), :]
            yb = y_ref[pl.ds(cpr + s, tm, stride=stride), :]
            o_ref[:, lanes] = x2_ref[:, lanes] + gate2[:, lanes] * (g0 * ya + g1 * yb)

    if len(o_refs) == 1:
        write(o_refs[0])
    else:
        pl.when(i < n_prompt_tiles)(lambda: write(o_refs[0]))
        pl.when(i >= n_prompt_tiles)(lambda: write(o_refs[1]))


def _combine(y_rows, x2, g0, g1, mod, l, cond_row, n_prompt, split_output):
    n, d = x2.shape
    cpr = d // LANES
    n_prompt_tiles = n_prompt // TOKEN_TILE
    kernel = functools.partial(_combine_kernel, cpr=cpr, n_prompt_tiles=n_prompt_tiles)
    if split_output:
        out_specs = [
            pl.BlockSpec((TOKEN_TILE, d), lambda i: (jnp.minimum(i, n_prompt_tiles - 1), 0)),
            pl.BlockSpec((TOKEN_TILE, d), lambda i: (jnp.maximum(i - n_prompt_tiles, 0), 0))]
        out_shape = [jax.ShapeDtypeStruct((n_prompt, d), F32),
                     jax.ShapeDtypeStruct((n - n_prompt, d), F32)]
    else:
        out_specs = pl.BlockSpec((TOKEN_TILE, d), lambda i: (i, 0))
        out_shape = jax.ShapeDtypeStruct((n, d), F32)
    return pl.pallas_call(
        kernel,
        grid=(n // TOKEN_TILE,),
        in_specs=[
            pl.BlockSpec((TOKEN_TILE * TOP_K * cpr, LANES), lambda i: (i, 0)),
            pl.BlockSpec((TOKEN_TILE, d), lambda i: (i, 0)),
            pl.BlockSpec((TOKEN_TILE, LANES), lambda i: (i, 0)),
            pl.BlockSpec((TOKEN_TILE, LANES), lambda i: (i, 0)),
            pl.BlockSpec((None, None, N_MOD, d), lambda i: (l, cond_row(i), 0, 0)),
        ],
        out_specs=out_specs,
        out_shape=out_shape,
        compiler_params=_params(),
        name="combine",
    )(y_rows, x2, g0, g1, mod)


def _dispatch_plan(expert_ids, ranks, counts, n_tiles_max):
    tme = MOE_ROW_TILE
    n_experts = counts.shape[0]
    n = expert_ids.shape[1]
    n_flat = TOP_K * n
    tiles = (counts + tme - 1) // tme
    tile_end = jnp.cumsum(tiles)
    row_off = (tile_end - tiles) * tme
    token = jnp.arange(n, dtype=jnp.int32)
    pos, flat = [], []
    for k in range(TOP_K):
        off = jnp.sum(jnp.where(expert_ids[k][:, None] == jnp.arange(n_experts)[None, :],
                                row_off[None, :], 0), axis=1)
        pos.append(tme + off + ranks[k])
        flat.append(TOP_K * token + k)
    n_active = tile_end[-1]
    n_steps = n_tiles_max + 1
    tile_ids = jnp.minimum(jnp.arange(n_steps), n_active - 1)
    tile_expert = jnp.sum(tile_ids[:, None] >= tile_end[None, :], axis=1).astype(jnp.int32)
    n_dump_tiles = n_experts + 1
    rf = jnp.full(((n_steps + 1) * tme,), -1, jnp.int32).at[jnp.concatenate(pos)].set(
        jnp.concatenate(flat), unique_indices=True)
    is_pad = rf < 0
    pad_slot = jnp.minimum(jnp.cumsum(is_pad.astype(jnp.int32)) - 1, n_dump_tiles * tme - 1)
    rf = jnp.where(is_pad, n_flat + pad_slot, rf)
    return rf, tile_expert, n_active.astype(jnp.int32).reshape(1), n_dump_tiles


def kernel(x_prompt, x_sample, cache_k, cache_v, c, c_ctx, norm1_g, norm2_g, w_ada, b_ada, w_in,
           q_norm_g, k_norm_g, w_pool, pool_scale, rpb, w_out, ffn_w_gate, ffn_w_up, ffn_w_down,
           moe_router, moe_w_gate, moe_w_up, moe_w_down):
    batch, seq, d = x_prompt.shape
    dec_batch, dec_seq, _ = x_sample.shape
    depth = w_in.shape[0]
    heads = cache_k.shape[3]
    na_w = heads * HEAD_DIM
    pool_w = d - na_w
    n_prompt = batch * seq
    n = n_prompt + dec_batch * dec_seq
    n_experts = moe_router.shape[2]
    assert pool_w == len(POOL_WINDOWS) * LANES and na_w % LANES == 0
    assert TOKEN_TILE % seq == 0 and n_prompt % dec_seq == 0 and dec_seq % TOKEN_TILE == 0
    assert dec_seq // GRID_W >= NA_WIN_R and dec_batch < COND_ROWS
    n_prompt_tiles = n_prompt // TOKEN_TILE
    seq_info = dict(n_prompt_tiles=n_prompt_tiles, seq=seq, dec_seq=dec_seq)

    def cond_row(i):
        start = i * TOKEN_TILE
        return jnp.where(start < n_prompt, dec_batch, (start - n_prompt) // dec_seq)

    cond = jnp.zeros((COND_ROWS, d), F32).at[:dec_batch].set(c).at[dec_batch].set(c_ctx)
    mod = _ada(cond, w_ada, b_ada).reshape(depth, COND_ROWS, N_MOD, d)

    hsum = jnp.asarray(np.kron(np.eye(heads), np.ones((HEAD_DIM, HEAD_DIM))), BF16)
    ctx_k = cache_k.reshape(dec_batch, depth, cache_k.shape[2] * heads, HEAD_DIM)
    ctx_v = cache_v.reshape(dec_batch, depth, cache_v.shape[2] * heads, HEAD_DIM)

    x = (x_prompt.reshape(n_prompt, d), x_sample.reshape(-1, d))
    cache_kv = None
    cache_shape = (batch, depth, seq * heads, HEAD_DIM)
    for l in range(depth):
        pu, q, k, v, *cache_kv = _premix(
            x, mod, l, norm1_g[l][None], w_in[l].astype(BF16),
            jnp.tile(q_norm_g[l], heads)[None], jnp.tile(k_norm_g[l], heads)[None], hsum,
            n_prompt, cond_row, cache_kv, cache_shape, seq)
        attn = _attention(q, k, v, ctx_k, ctx_v, _relative_bias_table(rpb[l]), l,
                          n_prompt, seq, dec_seq)
        mix_args = (mod, l, w_pool[l].astype(BF16), pool_scale[l][None], w_out[l].astype(BF16),
                    norm2_g[l][None])
        li = l // 2
        if l % 2 == 0:
            assert ffn_w_gate.shape[2] % FF_CHUNK == 0
            x = _postmix_ffn(x, pu, attn, *mix_args, ffn_w_gate[li].astype(BF16),
                             ffn_w_up[li].astype(BF16), ffn_w_down[li].astype(BF16),
                             seq_info, cond_row)
        else:
            router = jnp.zeros((d, ROUTER_LANES), F32).at[:, :n_experts].set(moe_router[li])
            r_hi = router.astype(BF16)
            r_lo = (router - r_hi.astype(F32)).astype(BF16)
            x2, h_rows, rout, g0, g1, counts = _postmix_router(
                x, pu, attn, *mix_args, r_hi, r_lo, n_experts, seq_info, cond_row)
            routing = rout.reshape(n // seq, 2 * TOP_K, seq).transpose(1, 0, 2).reshape(
                2 * TOP_K, n).astype(jnp.int32)
            n_tiles_max = (TOP_K * n + n_experts * (MOE_ROW_TILE - 1)) // MOE_ROW_TILE
            row_flat, tile_expert, n_active, n_dump_tiles = _dispatch_plan(
                routing[:TOP_K], routing[TOP_K:], counts[0, :n_experts].astype(jnp.int32),
                n_tiles_max)
            y_rows = _moe(tile_expert, n_active, row_flat, h_rows,
                          moe_w_gate[li], moe_w_up[li], moe_w_down[li], d, TOP_K * n, n_dump_tiles)
            x = _combine(y_rows, x2, g0, g1, mod, l, cond_row, n_prompt,
                         split_output=(l == depth - 1))

    if not isinstance(x, tuple):
        x = (x[:n_prompt], x[n_prompt:])
    y_prompt = x[0].reshape(batch, seq, d)
    y_sample = x[1].reshape(dec_batch, dec_seq, d)
    new_k, new_v = (a.reshape(batch, depth, seq, heads, HEAD_DIM) for a in cache_kv)
    return (y_prompt, y_sample, new_k, new_v)
```

```python
import functools

import numpy as np
import jax
import jax.numpy as jnp
from jax import lax
from jax.experimental import pallas as pl
from jax.experimental.pallas import tpu as pltpu

F32 = jnp.float32
BF16 = jnp.bfloat16

GRID_W = 64
POOL_WINDOWS = (2, 4, 8, 16)
HEAD_DIM = 64
NA_WIN_R = 8
NA_WIN_C = 16
N_MOD = 6
TOP_K = 2
NORM_EPS = 1e-6
LOG2_E = 1.4426950408889634

LANES = 128
SUBLANES = 8
VMEM_LIMIT_BYTES = 56 * 1024 * 1024

TOKEN_TILE = 512
POOL_HALO = 8
FF_CHUNK = 256
MOE_ROW_TILE = 1008
MOE_F_CHUNK = 512
ADA_COL_TILE = 1024
COND_ROWS = 16
ROUTER_LANES = 128


def _params(n_axes=1):
    return pltpu.CompilerParams(
        dimension_semantics=("arbitrary",) * n_axes,
        vmem_limit_bytes=VMEM_LIMIT_BYTES,
    )


def _resident(shape, index_map):
    return pl.BlockSpec(shape, index_map, pipeline_mode=pl.Buffered(1))


def _x_specs(x, n_prompt_tiles):
    if not isinstance(x, tuple):
        return [pl.BlockSpec((TOKEN_TILE, x.shape[1]), lambda i: (i, 0))]
    d = x[0].shape[1]
    return [pl.BlockSpec((TOKEN_TILE, d), lambda i: (jnp.minimum(i, n_prompt_tiles - 1), 0)),
            pl.BlockSpec((TOKEN_TILE, d), lambda i: (jnp.maximum(i - n_prompt_tiles, 0), 0))]


def _load_x(i, x_refs, n_prompt_tiles):
    if len(x_refs) == 1:
        return x_refs[0][...]
    return jnp.where(i < n_prompt_tiles, x_refs[0][...], x_refs[1][...])


def _as_tuple(x):
    return x if isinstance(x, tuple) else (x,)


def _ada_kernel(c_ref, w_ref, b_ref, o_ref):
    c = c_ref[...]
    s = c * jax.nn.sigmoid(c)
    o_ref[...] = jnp.dot(s.astype(BF16), w_ref[...].astype(BF16),
                         preferred_element_type=F32) + b_ref[...]


def _ada(cond, w_ada, b_ada):
    depth, d, width = w_ada.shape
    return pl.pallas_call(
        _ada_kernel,
        grid=(depth, width // ADA_COL_TILE),
        in_specs=[
            pl.BlockSpec((COND_ROWS, d), lambda l, j: (0, 0)),
            pl.BlockSpec((None, d, ADA_COL_TILE), lambda l, j: (l, 0, j)),
            pl.BlockSpec((None, 1, ADA_COL_TILE), lambda l, j: (l, 0, j)),
        ],
        out_specs=pl.BlockSpec((None, COND_ROWS, ADA_COL_TILE), lambda l, j: (l, 0, j)),
        out_shape=jax.ShapeDtypeStruct((depth, COND_ROWS, width), F32),
        compiler_params=_params(2),
        name="ada",
    )(cond, w_ada, b_ada.reshape(depth, 1, width))


def _premix_kernel(*refs, n_x, n_prompt_tiles, pool_w, na_w, seq, layer, creates_cache):
    n_in = n_x + (6 if creates_cache else 8)
    mod_ref, g_ref, w_ref, qg_ref, kg_ref, hsum_ref = refs[n_x:n_x + 6]
    pu_ref, q_ref, k_ref, v_ref, kf_ref, vf_ref = refs[n_in:]
    i = pl.program_id(0)
    x = _load_x(i, refs[:n_x], n_prompt_tiles)
    inv = lax.rsqrt(jnp.mean(x * x, axis=-1, keepdims=True) + NORM_EPS)
    mod = mod_ref[...]
    h = (x * inv * g_ref[...]) * (1.0 + mod[1:2]) + mod[0:1]
    u = jnp.dot(h.astype(BF16), w_ref[...], preferred_element_type=F32)
    pu_ref[...] = u[:, :pool_w]
    q = u[:, pool_w:pool_w + na_w]
    k = u[:, pool_w + na_w:pool_w + 2 * na_w]
    v = u[:, pool_w + 2 * na_w:]

    def head_norm(t, g):
        ms = jnp.dot((t * t).astype(BF16), hsum_ref[...],
                     preferred_element_type=F32) * (1.0 / HEAD_DIM)
        return t * lax.rsqrt(ms + NORM_EPS) * g

    qn = head_norm(q, qg_ref[...])
    kn = head_norm(k, kg_ref[...])
    q_ref[...] = (qn * (HEAD_DIM ** -0.5 * LOG2_E)).astype(BF16)
    k_ref[...] = kn.astype(BF16)
    v_ref[...] = v.astype(BF16)

    @pl.when(i < n_prompt_tiles)
    def _():
        heads = na_w // HEAD_DIM
        if creates_cache:
            for lz in range(kf_ref.shape[1]):
                if lz != layer:
                    kf_ref[:, lz] = jnp.zeros((kf_ref.shape[0],) + kf_ref.shape[2:], F32)
                    vf_ref[:, lz] = jnp.zeros((vf_ref.shape[0],) + vf_ref.shape[2:], F32)
            kf_l, vf_l = kf_ref.at[:, layer], vf_ref.at[:, layer]
        else:
            kf_l, vf_l = kf_ref, vf_ref
        for b in range(TOKEN_TILE // seq):
            for hd in range(heads):
                rows, cols = slice(b * seq, (b + 1) * seq), slice(hd * HEAD_DIM, (hd + 1) * HEAD_DIM)
                kf_l[b, pl.ds(hd, seq, stride=heads), :] = kn[rows, cols]
                vf_l[b, pl.ds(hd, seq, stride=heads), :] = v[rows, cols]


def _premix(x, mod, l, norm_g, w_in, q_g, k_g, hsum, n_prompt, cond_row, cache_kv, cache_shape,
            seq):
    xs = _as_tuple(x)
    creates_cache = cache_kv is None
    n, d = sum(a.shape[0] for a in xs), xs[0].shape[1]
    pool_w = d // 2
    na_w = d - pool_w
    n_tiles = n // TOKEN_TILE
    n_prompt_tiles = n_prompt // TOKEN_TILE
    last_p = n_prompt_tiles - 1
    tok = lambda w: pl.BlockSpec((TOKEN_TILE, w), lambda i: (i, 0))
    if creates_cache:
        cache_spec = pl.BlockSpec((TOKEN_TILE // seq,) + cache_shape[1:],
                                  lambda i: (jnp.minimum(i, last_p), 0, 0, 0))
        cache_in, cache_in_specs, aliases = (), [], {}
    else:
        cache_spec = pl.BlockSpec((TOKEN_TILE // seq, None) + cache_shape[2:],
                                  lambda i: (jnp.minimum(i, last_p), l, 0, 0))
        cache_in, cache_in_specs = tuple(cache_kv), [pl.BlockSpec(memory_space=pl.ANY)] * 2
        aliases = {len(xs) + 6: 4, len(xs) + 7: 5}
    kernel = functools.partial(_premix_kernel, n_x=len(xs), n_prompt_tiles=n_prompt_tiles,
                               pool_w=pool_w, na_w=na_w, seq=seq, layer=l,
                               creates_cache=creates_cache)
    return pl.pallas_call(
        kernel,
        grid=(n_tiles,),
        in_specs=_x_specs(x, n_prompt_tiles) + [
            pl.BlockSpec((None, None, N_MOD, d), lambda i: (l, cond_row(i), 0, 0)),
            pl.BlockSpec((1, d), lambda i: (0, 0)),
            _resident((d, w_in.shape[1]), lambda i: (0, 0)),
            pl.BlockSpec((1, na_w), lambda i: (0, 0)),
            pl.BlockSpec((1, na_w), lambda i: (0, 0)),
            _resident((na_w, na_w), lambda i: (0, 0)),
        ] + cache_in_specs,
        out_specs=[tok(pool_w), tok(na_w), tok(na_w), tok(na_w), cache_spec, cache_spec],
        out_shape=[
            jax.ShapeDtypeStruct((n, pool_w), F32),
            jax.ShapeDtypeStruct((n, na_w), BF16),
            jax.ShapeDtypeStruct((n, na_w), BF16),
            jax.ShapeDtypeStruct((n, na_w), BF16),
            jax.ShapeDtypeStruct(cache_shape, F32),
            jax.ShapeDtypeStruct(cache_shape, F32),
        ],
        input_output_aliases=aliases,
        compiler_params=_params(),
        name=f"premix{l}",
    )(*xs, mod, norm_g, w_in, q_g, k_g, hsum, *cache_in)


_NT = (((1,), (1,)), ((), ()))
_TN = (((0,), (0,)), ((), ()))


def _block_diag_queries(q2):
    lo = lax.broadcasted_iota(jnp.int32, q2.shape, 1) < HEAD_DIM
    zero = jnp.zeros_like(q2)
    return jnp.concatenate([jnp.where(lo, q2, zero), jnp.where(lo, zero, q2)], axis=0)


def _pick_head_blocks(o, nq):
    lo = lax.broadcasted_iota(jnp.int32, (nq, LANES), 1) < HEAD_DIM
    return jnp.where(lo, o[:nq], o[nq:])


def _pair_attention(q2, k, v):
    nq = q2.shape[0]
    s = lax.dot_general(k, _block_diag_queries(q2), _NT, preferred_element_type=F32)
    p = jnp.exp2(s - jnp.max(s, axis=0, keepdims=True))
    r = 1.0 / jnp.sum(p, axis=0, keepdims=True)
    o = lax.dot_general((p * r).astype(BF16), v, _TN, preferred_element_type=F32)
    return _pick_head_blocks(o, nq)


def _attn_kernel(q_ref, k_ref, v_ref, ckf_ref, cvf_ref, bias_ref, o_ref, s_ref, p_ref,
                 ck_ref, cv_ref, *, n_prompt_tiles, seq, dec_seq, n_pairs):
    i = pl.program_id(0)
    tiles_per_seq = dec_seq // TOKEN_TILE
    rows_per_tile = TOKEN_TILE // GRID_W
    rows = dec_seq // GRID_W
    win_keys = NA_WIN_R * GRID_W
    block_off = (i * TOKEN_TILE) % dec_seq

    @pl.when(i < n_prompt_tiles)
    def _():
        for s in range(TOKEN_TILE // seq):
            start = pl.multiple_of(block_off + s * seq, seq)
            for hp in range(n_pairs):
                lanes = slice(hp * LANES, (hp + 1) * LANES)
                out = _pair_attention(q_ref[s * seq:(s + 1) * seq, lanes],
                                      k_ref[pl.ds(start, seq), lanes],
                                      v_ref[pl.ds(start, seq), lanes])
                o_ref[s * seq:(s + 1) * seq, lanes] = out.astype(o_ref.dtype)

    @pl.when(i >= n_prompt_tiles)
    def _():
        tile_in_seq = (i - n_prompt_tiles) % tiles_per_seq
        row0 = tile_in_seq * rows_per_tile

        @pl.when(tile_in_seq == 0)
        def _():
            past = ck_ref.shape[0]
            heads = 2 * n_pairs
            for src, dst in ((ckf_ref, ck_ref), (cvf_ref, cv_ref)):
                for hp in range(n_pairs):
                    pair = [src[pl.ds(2 * hp + t, past, stride=heads), :] for t in range(2)]
                    dst[:, hp * LANES:(hp + 1) * LANES] = (
                        jnp.concatenate(pair, axis=-1).astype(BF16))

        def indices(rl):
            r = row0 + rl
            r0 = jnp.clip(r - NA_WIN_R // 2, 0, rows - NA_WIN_R)
            return r0 - r + NA_WIN_R - 1, pl.multiple_of(r0 * GRID_W, GRID_W)

        def scores(rl):
            d0, kstart = indices(rl)
            for hp in range(n_pairs):
                lanes = slice(hp * LANES, (hp + 1) * LANES)
                qbd = _block_diag_queries(q_ref[rl * GRID_W:(rl + 1) * GRID_W, lanes])
                bias = bias_ref[hp, pl.ds(d0, NA_WIN_R)].reshape(win_keys, LANES)
                s_ref[rl % 2, hp, 0:win_keys] = lax.dot_general(
                    k_ref[pl.ds(kstart, win_keys), lanes], qbd, _NT,
                    preferred_element_type=F32) + bias
                s_ref[rl % 2, hp, win_keys:] = lax.dot_general(
                    ck_ref[:, lanes], qbd, _NT, preferred_element_type=F32)

        def softmax(rl):
            for hp in range(n_pairs):
                s = s_ref[rl % 2, hp]
                p = jnp.exp2(s - jnp.max(s, axis=0, keepdims=True))
                rr = 1.0 / jnp.sum(p, axis=0, keepdims=True)
                p_ref[rl % 2, hp] = (p * rr).astype(BF16)

        def values(rl):
            _, kstart = indices(rl)
            for hp in range(n_pairs):
                lanes = slice(hp * LANES, (hp + 1) * LANES)
                o = (lax.dot_general(p_ref[rl % 2, hp, 0:win_keys],
                                     v_ref[pl.ds(kstart, win_keys), lanes],
                                     _TN, preferred_element_type=F32)
                     + lax.dot_general(p_ref[rl % 2, hp, win_keys:], cv_ref[:, lanes], _TN,
                                       preferred_element_type=F32))
                o_ref[rl * GRID_W:(rl + 1) * GRID_W, lanes] = (
                    _pick_head_blocks(o, GRID_W).astype(o_ref.dtype))

        for step in range(rows_per_tile + 2):
            if step < rows_per_tile:
                scores(step)
            if 1 <= step <= rows_per_tile:
                softmax(step - 1)
            if step >= 2:
                values(step - 2)


def _attention(q, k, v, ctx_k, ctx_v, bias, l, n_prompt, seq, dec_seq):
    n, na_w = q.shape
    n_tiles = n // TOKEN_TILE
    n_prompt_tiles = n_prompt // TOKEN_TILE
    tiles_per_seq = dec_seq // TOKEN_TILE
    heads = na_w // HEAD_DIM
    past = ctx_k.shape[2] // heads
    n_pairs = na_w // LANES
    n_keys = NA_WIN_R * GRID_W + past
    kv_spec = pl.BlockSpec((dec_seq, na_w), lambda i: (i * TOKEN_TILE // dec_seq, 0))
    ctx_spec = pl.BlockSpec(
        (None, None, past * heads, HEAD_DIM),
        lambda i: (jnp.maximum(i - n_prompt_tiles, 0) // tiles_per_seq, l, 0, 0))
    kernel = functools.partial(_attn_kernel, n_prompt_tiles=n_prompt_tiles, seq=seq,
                               dec_seq=dec_seq, n_pairs=n_pairs)
    return pl.pallas_call(
        kernel,
        grid=(n_tiles,),
        in_specs=[
            pl.BlockSpec((TOKEN_TILE, na_w), lambda i: (i, 0)),
            kv_spec, kv_spec, ctx_spec, ctx_spec,
            _resident(bias.shape, lambda i: (0, 0, 0, 0)),
        ],
        out_specs=pl.BlockSpec((TOKEN_TILE, na_w), lambda i: (i, 0)),
        out_shape=jax.ShapeDtypeStruct((n, na_w), BF16),
        scratch_shapes=[pltpu.VMEM((2, n_pairs, n_keys, LANES), F32),
                        pltpu.VMEM((2, n_pairs, n_keys, LANES), BF16),
                        pltpu.VMEM((past, na_w), BF16),
                        pltpu.VMEM((past, na_w), BF16)],
        compiler_params=_params(),
        name=f"attn{l}",
    )(q, k, v, ctx_k, ctx_v, bias)


def _relative_bias_table(rpb_l):
    heads = rpb_l.shape[0]
    kc = np.arange(GRID_W)[:, None]
    qc = np.arange(GRID_W)[None, :]
    q_start = np.clip(qc - NA_WIN_C // 2, 0, GRID_W - NA_WIN_C)
    valid = (kc >= q_start) & (kc < q_start + NA_WIN_C)
    dc_idx = np.clip(kc - qc, -(NA_WIN_C - 1), NA_WIN_C - 1) + NA_WIN_C - 1
    t = jnp.zeros(rpb_l.shape[:2] + dc_idx.shape, F32)
    for c in range(rpb_l.shape[2]):
        t = jnp.where(dc_idx[None, None] == c, rpb_l[:, :, c, None, None].astype(F32), t)
    t = jnp.where(valid[None, None], t * LOG2_E, -jnp.inf)
    t = t.reshape(heads // 2, 2, 2 * NA_WIN_R - 1, GRID_W, GRID_W)
    return t.transpose(0, 2, 3, 1, 4).reshape(heads // 2, 2 * NA_WIN_R - 1, GRID_W, 2 * GRID_W)


def _mod_static(t, m):
    return t & (m - 1) if m & (m - 1) == 0 else lax.rem(t, m)


def _pool_mix(i, b, pu_ref, pp_ref, pn_ref, ext_ref, wp_ref, ps_ref, *, n_prompt_tiles, seq, dec_seq):
    sub = seq
    h = POOL_HALO
    is_prompt = i < n_prompt_tiles
    seq_len = jnp.where(is_prompt, seq, dec_seq)
    lo, hi = b * sub, (b + 1) * sub
    tok0 = i * TOKEN_TILE + lo
    pos0 = jnp.where(is_prompt, _mod_static(tok0, seq), _mod_static(tok0, dec_seq))
    prev = pp_ref[...] if lo == 0 else pu_ref[lo - h:lo, :]
    nxt = pn_ref[...] if hi == TOKEN_TILE else pu_ref[hi:hi + h, :]
    ext_ref[b, 0:h] = jnp.where(pos0 != 0, prev, 0.0)
    ext_ref[b, h:h + sub] = pu_ref[lo:hi, :]
    ext_ref[b, h + sub:] = jnp.where(pos0 + sub != seq_len, nxt, 0.0)
    left = pos0 + lax.broadcasted_iota(jnp.int32, (sub, LANES), 0)
    right = seq_len - left
    outs = []
    for g, window in enumerate(POOL_WINDOWS):
        half = window // 2
        lanes = slice(g * LANES, (g + 1) * LANES)
        total = ext_ref[b, h - half:h - half + sub, lanes]
        for j in range(1 - half, half):
            total = total + ext_ref[b, h + j:h + j + sub, lanes]
        count = (jnp.minimum(left, half) + jnp.minimum(right, half)).astype(F32)
        pooled = total / count - pu_ref[lo:hi, lanes]
        outs.append(jnp.dot(pooled.astype(BF16), wp_ref[g], preferred_element_type=F32))
    return jnp.concatenate(outs, axis=-1) * ps_ref[...]


def _postmix_chain(i, x_refs, pu_ref, pp_ref, pn_ref, at_ref, mod_ref, wp_ref, ps_ref, wo_ref,
                   g2_ref, ext_ref, tail, **seq_info):
    sub = seq_info["seq"]
    n_sub = TOKEN_TILE // sub
    mod = mod_ref[...]
    is_prompt = i < seq_info["n_prompt_tiles"]

    def rows(b):
        return slice(b * sub, (b + 1) * sub)

    def pool_stage(b, _):
        return _pool_mix(i, b, pu_ref, pp_ref, pn_ref, ext_ref, wp_ref, ps_ref, **seq_info)

    def mix_stage(b, pool_out):
        mixed = jnp.concatenate([pool_out.astype(BF16), at_ref[rows(b), :]], axis=-1)
        mix = jnp.dot(mixed, wo_ref[...], preferred_element_type=F32)
        if len(x_refs) == 1:
            x = x_refs[0][rows(b), :]
        else:
            x = jnp.where(is_prompt, x_refs[0][rows(b), :], x_refs[1][rows(b), :])
        return x + mod[2:3] * mix

    def norm_stage(b, x2):
        inv = lax.rsqrt(jnp.mean(x2 * x2, axis=-1, keepdims=True) + NORM_EPS)
        return x2, (x2 * inv * g2_ref[...]) * (1.0 + mod[4:5]) + mod[3:4]

    def tail_stage(b, x2_h2):
        tail(rows(b), *x2_h2)

    stages = (pool_stage, mix_stage, norm_stage, tail_stage)
    state = [None] * n_sub
    for step in range(n_sub + len(stages) - 1):
        for b in range(n_sub):
            if 0 <= step - b < len(stages):
                state[b] = stages[step - b](b, state[b])
    return mod[5:6]


def _postmix_ffn_kernel(*refs, n_x, **seq_info):
    (pu_ref, pp_ref, pn_ref, at_ref, mod_ref, wp_ref, ps_ref, wo_ref,
     g2_ref, wg_ref, wu_ref, wd_ref, y_ref, ext_ref, h_ref, acc_ref) = refs[n_x:]
    i = pl.program_id(0)

    def tail(rows, x2, h2):
        y_ref[rows, :] = x2
        h_ref[rows, :] = h2.astype(BF16)

    gate2 = _postmix_chain(i, refs[:n_x], pu_ref, pp_ref, pn_ref, at_ref, mod_ref, wp_ref,
                           ps_ref, wo_ref, g2_ref, ext_ref, tail, **seq_info)
    acc_ref[...] = jnp.zeros_like(acc_ref)

    h = h_ref[...]
    for c in range(wg_ref.shape[1] // FF_CHUNK):
        cols = slice(c * FF_CHUNK, (c + 1) * FF_CHUNK)
        a = jnp.dot(h, wg_ref[:, cols], preferred_element_type=F32)
        b = jnp.dot(h, wu_ref[:, cols], preferred_element_type=F32)
        act = (a * jax.nn.sigmoid(a) * b).astype(BF16)
        acc_ref[...] += jnp.dot(act, wd_ref[cols, :], preferred_element_type=F32)
    y_ref[...] = y_ref[...] + gate2 * acc_ref[...]


def _postmix_router_kernel(*refs, n_x, n_experts, **seq_info):
    (pu_ref, pp_ref, pn_ref, at_ref, mod_ref, wp_ref, ps_ref, wo_ref,
     g2_ref, rh_ref, rl_ref, tri_ref, x2_ref, hr_ref, rout_ref, g0_ref, g1_ref, cnt_ref,
     ext_ref) = refs[n_x:]
    i = pl.program_id(0)

    @pl.when(i == 0)
    def _():
        cnt_ref[...] = jnp.zeros_like(cnt_ref)

    def tail(rows, x2, h2):
        x2_ref[rows, :] = x2
        n_rows = rows.stop - rows.start
        for s in range(h2.shape[1] // LANES):
            hr_ref[pl.ds(rows.start * SUBLANES + s, n_rows, stride=SUBLANES), :] = (
                h2[:, s * LANES:(s + 1) * LANES])
        hi = h2.astype(BF16)
        lo = (h2 - hi.astype(F32)).astype(BF16)
        logits = (jnp.dot(hi, rh_ref[...], preferred_element_type=F32)
                  + jnp.dot(lo, rh_ref[...], preferred_element_type=F32)
                  + jnp.dot(hi, rl_ref[...], preferred_element_type=F32))
        lane = lax.broadcasted_iota(jnp.int32, logits.shape, 1).astype(F32)
        logits = jnp.where(lane < n_experts, logits, -jnp.inf)
        m1 = jnp.max(logits, axis=-1, keepdims=True)
        i1 = jnp.min(jnp.where(logits == m1, lane, float(ROUTER_LANES)), axis=-1, keepdims=True)
        rest = jnp.where(lane == i1, -jnp.inf, logits)
        m2 = jnp.max(rest, axis=-1, keepdims=True)
        i2 = jnp.min(jnp.where(rest == m2, lane, float(ROUTER_LANES)), axis=-1, keepdims=True)
        e = jnp.exp(m2 - m1)
        pick0 = jnp.where(lane == i1, 1.0, 0.0)
        pick1 = jnp.where(lane == i2, 1.0, 0.0)
        picks = pick0 + pick1
        before = cnt_ref[...] + jnp.dot(tri_ref[...], picks.astype(BF16),
                                        preferred_element_type=F32)
        rank0 = jnp.sum(before * pick0, axis=-1, keepdims=True)
        rank1 = jnp.sum(before * pick1, axis=-1, keepdims=True)
        cnt_ref[...] = before[n_rows - 1:n_rows, :] + picks[n_rows - 1:n_rows, :]
        packed_t = jnp.where(lane == 0, i1, jnp.where(lane == 1, i2, jnp.where(
            lane == 2, rank0, jnp.where(lane == 3, rank1, 0.0)))).T
        groups = n_rows // LANES
        base = (rows.start // n_rows) * 2 * TOP_K * groups
        for q in range(2 * TOP_K):
            for hh in range(groups):
                rout_ref[base + q * groups + hh:base + q * groups + hh + 1, :] = (
                    packed_t[q:q + 1, hh * LANES:(hh + 1) * LANES])
        g0_ref[rows, :] = jnp.broadcast_to(1.0 / (1.0 + e), (n_rows, LANES))
        g1_ref[rows, :] = jnp.broadcast_to(e / (1.0 + e), (n_rows, LANES))

    _postmix_chain(i, refs[:n_x], pu_ref, pp_ref, pn_ref, at_ref, mod_ref, wp_ref,
                   ps_ref, wo_ref, g2_ref, ext_ref, tail, **seq_info)


def _pool_scratch_shape(seq_info, pool_w):
    sub = seq_info["seq"]
    return (TOKEN_TILE // sub, sub + 2 * POOL_HALO, pool_w)


def _postmix_specs(x, n, d, pool_w, na_w, l, cond_row, n_prompt_tiles):
    n_halo_blocks = n // POOL_HALO
    per_tile = TOKEN_TILE // POOL_HALO
    tok = lambda w: pl.BlockSpec((TOKEN_TILE, w), lambda i: (i, 0))
    return _x_specs(x, n_prompt_tiles) + [
        tok(pool_w),
        pl.BlockSpec((POOL_HALO, pool_w), lambda i: (jnp.maximum(i * per_tile - 1, 0), 0)),
        pl.BlockSpec((POOL_HALO, pool_w),
                     lambda i: (jnp.minimum((i + 1) * per_tile, n_halo_blocks - 1), 0)),
        tok(na_w),
        pl.BlockSpec((None, None, N_MOD, d), lambda i: (l, cond_row(i), 0, 0)),
        _resident((len(POOL_WINDOWS), LANES, LANES), lambda i: (0, 0, 0)),
        pl.BlockSpec((1, pool_w), lambda i: (0, 0)),
        _resident((d, d), lambda i: (0, 0)),
        pl.BlockSpec((1, d), lambda i: (0, 0)),
    ]


def _postmix_ffn(x, pu, attn, mod, l, w_pool, pool_scale, w_out, norm_g, wg, wu, wd,
                 seq_info, cond_row):
    xs = _as_tuple(x)
    n, d = pu.shape[0], xs[0].shape[1]
    pool_w = pu.shape[1]
    d_ff = wg.shape[1]
    kernel = functools.partial(_postmix_ffn_kernel, n_x=len(xs), **seq_info)
    return pl.pallas_call(
        kernel,
        grid=(n // TOKEN_TILE,),
        in_specs=_postmix_specs(x, n, d, pool_w, attn.shape[1], l, cond_row,
                                seq_info["n_prompt_tiles"]) + [
            _resident((d, d_ff), lambda i: (0, 0)),
            _resident((d, d_ff), lambda i: (0, 0)),
            _resident((d_ff, d), lambda i: (0, 0)),
        ],
        out_specs=pl.BlockSpec((TOKEN_TILE, d), lambda i: (i, 0)),
        out_shape=jax.ShapeDtypeStruct((n, d), F32),
        scratch_shapes=[
            pltpu.VMEM(_pool_scratch_shape(seq_info, pool_w), F32),
            pltpu.VMEM((TOKEN_TILE, d), BF16),
            pltpu.VMEM((TOKEN_TILE, d), F32),
        ],
        compiler_params=_params(),
        name=f"postmix_ffn{l}",
    )(*xs, pu, pu, pu, attn, mod, w_pool, pool_scale, w_out, norm_g, wg, wu, wd)


def _postmix_router(x, pu, attn, mod, l, w_pool, pool_scale, w_out, norm_g, r_hi, r_lo,
                    n_experts, seq_info, cond_row):
    xs = _as_tuple(x)
    sub = seq_info["seq"]
    tri = jnp.asarray(np.tril(np.ones((sub, sub)), -1), BF16)
    n, d = pu.shape[0], xs[0].shape[1]
    pool_w = pu.shape[1]
    kernel = functools.partial(_postmix_router_kernel, n_x=len(xs), n_experts=n_experts,
                               **seq_info)
    row_tile = TOKEN_TILE * d // LANES
    return pl.pallas_call(
        kernel,
        grid=(n // TOKEN_TILE,),
        in_specs=_postmix_specs(x, n, d, pool_w, attn.shape[1], l, cond_row,
                                seq_info["n_prompt_tiles"]) + [
            pl.BlockSpec((d, ROUTER_LANES), lambda i: (0, 0)),
            pl.BlockSpec((d, ROUTER_LANES), lambda i: (0, 0)),
            pl.BlockSpec((sub, sub), lambda i: (0, 0)),
        ],
        out_specs=[
            pl.BlockSpec((TOKEN_TILE, d), lambda i: (i, 0)),
            pl.BlockSpec((row_tile, LANES), lambda i: (i, 0)),
            pl.BlockSpec((2 * TOP_K * TOKEN_TILE // LANES, LANES), lambda i: (i, 0)),
            pl.BlockSpec((TOKEN_TILE, LANES), lambda i: (i, 0)),
            pl.BlockSpec((TOKEN_TILE, LANES), lambda i: (i, 0)),
            pl.BlockSpec((1, ROUTER_LANES), lambda i: (0, 0)),
        ],
        out_shape=[
            jax.ShapeDtypeStruct((n, d), F32),
            jax.ShapeDtypeStruct((n * d // LANES, LANES), F32),
            jax.ShapeDtypeStruct((2 * TOP_K * n // LANES, LANES), F32),
            jax.ShapeDtypeStruct((n, LANES), F32),
            jax.ShapeDtypeStruct((n, LANES), F32),
            jax.ShapeDtypeStruct((1, ROUTER_LANES), F32),
        ],
        scratch_shapes=[pltpu.VMEM(_pool_scratch_shape(seq_info, pool_w), F32)],
        compiler_params=_params(),
        name=f"postmix_router{l}",
    )(*xs, pu, pu, pu, attn, mod, w_pool, pool_scale, w_out, norm_g, r_hi, r_lo, tri)


def _moe_kernel(te_ref, na_ref, rf_ref, h_hbm, wg_ref, wu_ref, wd_ref, y_hbm,
                xbuf, ybuf, xb16, acc_ref, gsem, ssem, zsem, *,
                cpr, rows_per_step, n_flat, n_dump_tiles):
    j = pl.program_id(0)
    f = pl.program_id(1)
    n_f = pl.num_programs(1)
    tme = MOE_ROW_TILE
    tile_rows = tme * cpr
    n_active = na_ref[0]
    last_token = n_flat // TOP_K - 1

    def gather_row(tile, slot, row):
        v = rf_ref[(tile + 1) * tme + row]
        tok = jnp.minimum(lax.shift_right_logical(v, 1), last_token)
        return pltpu.make_async_copy(
            h_hbm.at[pl.ds(pl.multiple_of(tok * cpr, cpr), cpr)],
            xbuf.at[slot, pl.ds(pl.multiple_of(row * cpr, cpr), cpr)],
            gsem.at[slot])

    def scatter_row(tile, slot, row):
        v = rf_ref[(tile + 1) * tme + row]
        return pltpu.make_async_copy(
            ybuf.at[slot, pl.ds(pl.multiple_of(row * cpr, cpr), cpr)],
            y_hbm.at[pl.ds(pl.multiple_of(v * cpr, cpr), cpr)],
            ssem.at[slot])

    def wait_gather(slot):
        pltpu.make_async_copy(h_hbm.at[pl.ds(0, tile_rows)], xbuf.at[slot], gsem.at[slot]).wait()

    def wait_scatter(slot):
        pltpu.make_async_copy(ybuf.at[slot], y_hbm.at[pl.ds(0, tile_rows)], ssem.at[slot]).wait()

    def dump_fill(t):
        return pltpu.make_async_copy(
            ybuf.at[1], y_hbm.at[pl.ds((n_flat + t * tme) * cpr, tile_rows)], zsem)

    @pl.when((f == 0) & (j == 0))
    def _():
        ybuf[1] = jnp.zeros(ybuf.shape[1:], ybuf.dtype)
        for t in range(n_dump_tiles):
            dump_fill(t).start()
        for t in range(n_dump_tiles):
            dump_fill(t).wait()

        def body(r, carry):
            gather_row(0, 0, r).start()
            return carry

        lax.fori_loop(0, tme, body, 0)

    def step(par, first, last):
        if first:
            wait_gather(par)
            for s in range(cpr):
                xb16[:, s * LANES:(s + 1) * LANES] = (
                    xbuf[par, pl.ds(s, tme, stride=cpr), :].astype(BF16))
        for t in range(rows_per_step):
            gather_row(j + 1, 1 - par, t * n_f + f).start()
            scatter_row(j - 1, 1 - par, t * n_f + f).start(priority=1)
        x = xb16[...]
        a = jnp.dot(x, wg_ref[...].astype(BF16), preferred_element_type=F32)
        b = jnp.dot(x, wu_ref[...].astype(BF16), preferred_element_type=F32)
        act = (a * jax.nn.sigmoid(a) * b).astype(BF16)
        part = jnp.dot(act, wd_ref[...].astype(BF16), preferred_element_type=F32)
        total = part if first else acc_ref[...] + part
        if not last:
            acc_ref[...] = total
        else:
            pl.when(j >= 1)(lambda: wait_scatter(par))
            for s in range(cpr):
                ybuf[par, pl.ds(s, tme, stride=cpr), :] = total[:, s * LANES:(s + 1) * LANES]

    for par in range(2):
        on = (j < n_active) & (j % 2 == par)
        pl.when(on & (f == 0))(functools.partial(step, par, True, False))
        pl.when(on & (f > 0) & (f < n_f - 1))(functools.partial(step, par, False, False))
        pl.when(on & (f == n_f - 1))(functools.partial(step, par, False, True))

        @pl.when((j == n_active) & (f == 0) & (j % 2 == par))
        def _(par=par):
            def body(r, carry):
                scatter_row(j - 1, 1 - par, r).start()
                return carry

            lax.fori_loop(0, tme, body, 0)
            wait_gather(par)
            wait_scatter(par)
            wait_scatter(1 - par)


def _moe(tile_expert, n_active, row_flat, h_rows, wg, wu, wd, d, n_flat, n_dump_tiles):
    n_steps = tile_expert.shape[0]
    d_exp = wg.shape[2]
    n_f = d_exp // MOE_F_CHUNK
    cpr = d // LANES
    tile_rows = MOE_ROW_TILE * cpr
    assert MOE_ROW_TILE % n_f == 0 and n_f >= 2

    def f_idx(j, f, na):
        return jnp.where(j < na[0], f, n_f - 1)

    grid_spec = pltpu.PrefetchScalarGridSpec(
        num_scalar_prefetch=3,
        grid=(n_steps, n_f),
        in_specs=[
            pl.BlockSpec(memory_space=pl.ANY),
            pl.BlockSpec((None, d, MOE_F_CHUNK), lambda j, f, te, na, rf: (te[j], 0, f_idx(j, f, na))),
            pl.BlockSpec((None, d, MOE_F_CHUNK), lambda j, f, te, na, rf: (te[j], 0, f_idx(j, f, na))),
            pl.BlockSpec((None, MOE_F_CHUNK, d), lambda j, f, te, na, rf: (te[j], f_idx(j, f, na), 0)),
        ],
        out_specs=pl.BlockSpec(memory_space=pl.ANY),
        scratch_shapes=[
            pltpu.VMEM((2, tile_rows, LANES), F32),
            pltpu.VMEM((2, tile_rows, LANES), F32),
            pltpu.VMEM((MOE_ROW_TILE, d), BF16),
            pltpu.VMEM((MOE_ROW_TILE, d), F32),
            pltpu.SemaphoreType.DMA((2,)),
            pltpu.SemaphoreType.DMA((2,)),
            pltpu.SemaphoreType.DMA,
        ],
    )
    kernel = functools.partial(_moe_kernel, cpr=cpr, rows_per_step=MOE_ROW_TILE // n_f,
                               n_flat=n_flat, n_dump_tiles=n_dump_tiles)
    return pl.pallas_call(
        kernel,
        grid_spec=grid_spec,
        out_shape=jax.ShapeDtypeStruct(((n_flat + n_dump_tiles * MOE_ROW_TILE) * cpr, LANES), F32),
        compiler_params=_params(2),
        name="moe",
    )(tile_expert, n_active, row_flat, h_rows, wg, wu, wd)


def _combine_kernel(y_ref, x2_ref, g0_ref, g1_ref, mod_ref, *o_refs, cpr, n_prompt_tiles):
    i = pl.program_id(0)
    tm = x2_ref.shape[0]
    gate2 = mod_ref[...][5:6]
    stride = TOP_K * cpr

    def write(o_ref):
        g0 = g0_ref[...]
        g1 = g1_ref[...]
        for s in range(cpr):
            lanes = slice(s * LANES, (s + 1) * LANES)
            ya = y_ref[pl.ds(s, tm, stride=stride), :]
            yb = y_ref[pl.ds(cpr + s, tm, stride=stride), :]
            o_ref[:, lanes] = x2_ref[:, lanes] + gate2[:, lanes] * (g0 * ya + g1 * yb)

    if len(o_refs) == 1:
        write(o_refs[0])
    else:
        pl.when(i < n_prompt_tiles)(lambda: write(o_refs[0]))
        pl.when(i >= n_prompt_tiles)(lambda: write(o_refs[1]))


def _combine(y_rows, x2, g0, g1, mod, l, cond_row, n_prompt, split_output):
    n, d = x2.shape
    cpr = d // LANES
    n_prompt_tiles = n_prompt // TOKEN_TILE
    kernel = functools.partial(_combine_kernel, cpr=cpr, n_prompt_tiles=n_prompt_tiles)
    if split_output:
        out_specs = [
            pl.BlockSpec((TOKEN_TILE, d), lambda i: (jnp.minimum(i, n_prompt_tiles - 1), 0)),
            pl.BlockSpec((TOKEN_TILE, d), lambda i: (jnp.maximum(i - n_prompt_tiles, 0), 0))]
        out_shape = [jax.ShapeDtypeStruct((n_prompt, d), F32),
                     jax.ShapeDtypeStruct((n - n_prompt, d), F32)]
    else:
        out_specs = pl.BlockSpec((TOKEN_TILE, d), lambda i: (i, 0))
        out_shape = jax.ShapeDtypeStruct((n, d), F32)
    return pl.pallas_call(
        kernel,
        grid=(n // TOKEN_TILE,),
        in_specs=[
            pl.BlockSpec((TOKEN_TILE * TOP_K * cpr, LANES), lambda i: (i, 0)),
            pl.BlockSpec((TOKEN_TILE, d), lambda i: (i, 0)),
            pl.BlockSpec((TOKEN_TILE, LANES), lambda i: (i, 0)),
            pl.BlockSpec((TOKEN_TILE, LANES), lambda i: (i, 0)),
            pl.BlockSpec((None, None, N_MOD, d), lambda i: (l, cond_row(i), 0, 0)),
        ],
        out_specs=out_specs,
        out_shape=out_shape,
        compiler_params=_params(),
        name="combine",
    )(y_rows, x2, g0, g1, mod)


def _dispatch_plan(expert_ids, ranks, counts, n_tiles_max):
    tme = MOE_ROW_TILE
    n_experts = counts.shape[0]
    n = expert_ids.shape[1]
    n_flat = TOP_K * n
    tiles = (counts + tme - 1) // tme
    tile_end = jnp.cumsum(tiles)
    row_off = (tile_end - tiles) * tme
    token = jnp.arange(n, dtype=jnp.int32)
    pos, flat = [], []
    for k in range(TOP_K):
        off = jnp.sum(jnp.where(expert_ids[k][:, None] == jnp.arange(n_experts)[None, :],
                                row_off[None, :], 0), axis=1)
        pos.append(tme + off + ranks[k])
        flat.append(TOP_K * token + k)
    n_active = tile_end[-1]
    n_steps = n_tiles_max + 1
    tile_ids = jnp.minimum(jnp.arange(n_steps), n_active - 1)
    tile_expert = jnp.sum(tile_ids[:, None] >= tile_end[None, :], axis=1).astype(jnp.int32)
    n_dump_tiles = n_experts + 1
    rf = jnp.full(((n_steps + 1) * tme,), -1, jnp.int32).at[jnp.concatenate(pos)].set(
        jnp.concatenate(flat), unique_indices=True)
    is_pad = rf < 0
    pad_slot = jnp.minimum(jnp.cumsum(is_pad.astype(jnp.int32)) - 1, n_dump_tiles * tme - 1)
    rf = jnp.where(is_pad, n_flat + pad_slot, rf)
    return rf, tile_expert, n_active.astype(jnp.int32).reshape(1), n_dump_tiles


def kernel(x_prompt, x_sample, cache_k, cache_v, c, c_ctx, norm1_g, norm2_g, w_ada, b_ada, w_in,
           q_norm_g, k_norm_g, w_pool, pool_scale, rpb, w_out, ffn_w_gate, ffn_w_up, ffn_w_down,
           moe_router, moe_w_gate, moe_w_up, moe_w_down):
    batch, seq, d = x_prompt.shape
    dec_batch, dec_seq, _ = x_sample.shape
    depth = w_in.shape[0]
    heads = cache_k.shape[3]
    na_w = heads * HEAD_DIM
    pool_w = d - na_w
    n_prompt = batch * seq
    n = n_prompt + dec_batch * dec_seq
    n_experts = moe_router.shape[2]
    assert pool_w == len(POOL_WINDOWS) * LANES and na_w % LANES == 0
    assert TOKEN_TILE % seq == 0 and n_prompt % dec_seq == 0 and dec_seq % TOKEN_TILE == 0
    assert dec_seq // GRID_W >= NA_WIN_R and dec_batch < COND_ROWS
    n_prompt_tiles = n_prompt // TOKEN_TILE
    seq_info = dict(n_prompt_tiles=n_prompt_tiles, seq=seq, dec_seq=dec_seq)

    def cond_row(i):
        start = i * TOKEN_TILE
        return jnp.where(start < n_prompt, dec_batch, (start - n_prompt) // dec_seq)

    cond = jnp.zeros((COND_ROWS, d), F32).at[:dec_batch].set(c).at[dec_batch].set(c_ctx)
    mod = _ada(cond, w_ada, b_ada).reshape(depth, COND_ROWS, N_MOD, d)

    hsum = jnp.asarray(np.kron(np.eye(heads), np.ones((HEAD_DIM, HEAD_DIM))), BF16)
    ctx_k = cache_k.reshape(dec_batch, depth, cache_k.shape[2] * heads, HEAD_DIM)
    ctx_v = cache_v.reshape(dec_batch, depth, cache_v.shape[2] * heads, HEAD_DIM)

    x = (x_prompt.reshape(n_prompt, d), x_sample.reshape(-1, d))
    cache_kv = None
    cache_shape = (batch, depth, seq * heads, HEAD_DIM)
    for l in range(depth):
        pu, q, k, v, *cache_kv = _premix(
            x, mod, l, norm1_g[l][None], w_in[l].astype(BF16),
            jnp.tile(q_norm_g[l], heads)[None], jnp.tile(k_norm_g[l], heads)[None], hsum,
            n_prompt, cond_row, cache_kv, cache_shape, seq)
        attn = _attention(q, k, v, ctx_k, ctx_v, _relative_bias_table(rpb[l]), l,
                          n_prompt, seq, dec_seq)
        mix_args = (mod, l, w_pool[l].astype(BF16), pool_scale[l][None], w_out[l].astype(BF16),
                    norm2_g[l][None])
        li = l // 2
        if l % 2 == 0:
            assert ffn_w_gate.shape[2] % FF_CHUNK == 0
            x = _postmix_ffn(x, pu, attn, *mix_args, ffn_w_gate[li].astype(BF16),
                             ffn_w_up[li].astype(BF16), ffn_w_down[li].astype(BF16),
                             seq_info, cond_row)
        else:
            router = jnp.zeros((d, ROUTER_LANES), F32).at[:, :n_experts].set(moe_router[li])
            r_hi = router.astype(BF16)
            r_lo = (router - r_hi.astype(F32)).astype(BF16)
            x2, h_rows, rout, g0, g1, counts = _postmix_router(
                x, pu, attn, *mix_args, r_hi, r_lo, n_experts, seq_info, cond_row)
            routing = rout.reshape(n // seq, 2 * TOP_K, seq).transpose(1, 0, 2).reshape(
                2 * TOP_K, n).astype(jnp.int32)
            n_tiles_max = (TOP_K * n + n_experts * (MOE_ROW_TILE - 1)) // MOE_ROW_TILE
            row_flat, tile_expert, n_active, n_dump_tiles = _dispatch_plan(
                routing[:TOP_K], routing[TOP_K:], counts[0, :n_experts].astype(jnp.int32),
                n_tiles_max)
            y_rows = _moe(tile_expert, n_active, row_flat, h_rows,
                          moe_w_gate[li], moe_w_up[li], moe_w_down[li], d, TOP_K * n, n_dump_tiles)
            x = _combine(y_rows, x2, g0, g1, mod, l, cond_row, n_prompt,
                         split_output=(l == depth - 1))

    if not isinstance(x, tuple):
        x = (x[:n_prompt], x[n_prompt:])
    y_prompt = x[0].reshape(batch, seq, d)
    y_sample = x[1].reshape(dec_batch, dec_seq, d)
    new_k, new_v = (a.reshape(batch, depth, seq, heads, HEAD_DIM) for a in cache_kv)
    return (y_prompt, y_sample, new_k, new_v)
```
